```python
import jax, jax.numpy as jnp
from jax import lax
import numpy as np

D_MODEL = 1024
BATCH = 32
SEQ = 256
DEPTH = 2
DEC_BATCH = 4
DEC_SEQ = 4096
PAST_LEN = 512

GRID_W = 64
HEAD_DIM = 64
HA = 4
NA_WIN_R = 8
NA_WIN_C = 16
HB = 8
KVB = 2
HC = 4
KVC = 2
WINDOW = 128
N_KV_ALL = HA + KVB + KVC
Q_BLOCK = 128
ROPE_THETA = 10000.0
IN_SPLITS = (HA * HEAD_DIM, HA * HEAD_DIM, HA * HEAD_DIM,
             HB * HEAD_DIM, KVB * HEAD_DIM, KVB * HEAD_DIM,
             HC * HEAD_DIM, KVC * HEAD_DIM, KVC * HEAD_DIM)
IN_COLS = 2048
N_GROUPS = 4
EXPERTS_PER_GROUP = 8
N_EXPERTS = 32
EXPERT_FF = 512
TOP_K = 2
MOE_BLOCK = 128
EPS = 1e-6
NEG = -1e30

kernel_name = "hybrid_dit_ctx_prefix_step"


def rms_norm(x, gain):
    xf = x.astype(jnp.float32)
    y = xf * lax.rsqrt(jnp.mean(xf * xf, axis=-1, keepdims=True) + EPS)
    return (y * gain.astype(jnp.float32)).astype(x.dtype)


def modulation(cvec, w_ada, b_ada):
    m = jax.nn.silu(cvec) @ w_ada + b_ada
    m = m.reshape((-1, 1, 6 * D_MODEL))
    return jnp.split(m, 6, axis=-1)


def axial_rope(x, rows, cols):
    half = HEAD_DIM // 2
    quarter = half // 2
    freqs = ROPE_THETA ** (-jnp.arange(quarter, dtype=jnp.float32) / quarter)

    def rot(xh, pos):
        ang = pos.astype(jnp.float32)[:, None] * freqs[None, :]
        cos = jnp.cos(ang)[None, :, None, :].astype(x.dtype)
        sin = jnp.sin(ang)[None, :, None, :].astype(x.dtype)
        x1, x2 = xh[..., :quarter], xh[..., quarter:]
        return jnp.concatenate([x1 * cos - x2 * sin, x2 * cos + x1 * sin], axis=-1)

    return jnp.concatenate([rot(x[..., :half], rows), rot(x[..., half:], cols)], axis=-1)


def project_heads(u, w_in, qk_gain):
    B, L, _ = u.shape
    cuts = [int(v) for v in np.cumsum(IN_SPLITS)[:-1]]
    parts = jnp.split(u @ w_in, cuts, axis=-1)
    qa, ka, va, qb, kb, vb, qc, kc, vc = [p.reshape(B, L, -1, HEAD_DIM) for p in parts]
    qa, ka = rms_norm(qa, qk_gain[0]), rms_norm(ka, qk_gain[1])
    qb, kb = rms_norm(qb, qk_gain[2]), rms_norm(kb, qk_gain[3])
    qc, kc = rms_norm(qc, qk_gain[4]), rms_norm(kc, qk_gain[5])
    k_all = jnp.concatenate([ka, kb, kc], axis=2)
    v_all = jnp.concatenate([va, vb, vc], axis=2)
    return qa, qb, qc, k_all, v_all


def split_kv(t):
    return t[:, :, :HA], t[:, :, HA:HA + KVB], t[:, :, HA + KVB:]


def swept_gqa(q, k, v, sink=None):
    B, L, H, dh = q.shape
    KV = k.shape[2]
    G = H // KV
    nq = L // Q_BLOCK
    scale = HEAD_DIM ** -0.5
    qb = q.reshape(B, nq, Q_BLOCK, KV, G, dh).transpose(1, 0, 2, 3, 4, 5)

    def one_block(qi):
        s = jnp.einsum('bqkgd,bskd->bkgqs', qi, k).astype(jnp.float32) * scale
        if sink is not None:
            sk = jnp.broadcast_to(sink.reshape(KV, G, 1, 1).astype(jnp.float32), s.shape[:-1] + (1,))
            p = jax.nn.softmax(jnp.concatenate([s, sk], axis=-1), axis=-1)[..., :-1]
        else:
            p = jax.nn.softmax(s, axis=-1)
        return jnp.einsum('bkgqs,bskd->bqkgd', p.astype(v.dtype), v)

    o = lax.map(one_block, qb)
    return o.transpose(1, 0, 2, 3, 4, 5).reshape(B, L, H * dh)


def neighbourhood_attention(q, k, v, ck, cv, rpb):
    B, L, H, dh = q.shape
    rows = L // GRID_W
    wr = min(NA_WIN_R, rows)
    wc = NA_WIN_C
    scale = HEAD_DIM ** -0.5
    qg = q.reshape(B, rows, GRID_W, H, dh)
    kg = k.reshape(B, rows, GRID_W, H, dh)
    vg = v.reshape(B, rows, GRID_W, H, dh)
    r = jnp.arange(rows)
    row_idx = jnp.clip(r - wr // 2, 0, rows - wr)[:, None] + jnp.arange(wr)[None, :]
    col = jnp.arange(GRID_W)
    col_start = jnp.clip(col - wc // 2, 0, GRID_W - wc)
    col_ok = (col[None, :] >= col_start[:, None]) & (col[None, :] < col_start[:, None] + wc)
    k_blk = kg[:, row_idx]
    v_blk = vg[:, row_idx]
    dr = row_idx - r[:, None] + (NA_WIN_R - 1)
    dc = jnp.clip(col[None, :] - col[:, None], -(wc - 1), wc - 1) + (NA_WIN_C - 1)
    bias = rpb[:, dr[:, None, :, None], dc[None, :, None, :]]
    bias = bias.transpose(1, 0, 2, 3, 4).astype(jnp.float32)
    s_loc = jnp.einsum('brqhd,brjwhd->brhqjw', qg, k_blk).astype(jnp.float32) * scale + bias
    s_loc = jnp.where(col_ok[:, None, :], s_loc, NEG)
    n_loc = wr * GRID_W
    s_loc = s_loc.reshape(B, rows, H, GRID_W, n_loc)
    s_ctx = jnp.einsum('brqhd,bchd->brhqc', qg, ck).astype(jnp.float32) * scale
    p = jax.nn.softmax(jnp.concatenate([s_loc, s_ctx], axis=-1), axis=-1)
    p_loc = p[..., :n_loc].reshape(B, rows, H, GRID_W, wr, GRID_W).astype(v.dtype)
    p_ctx = p[..., n_loc:].astype(v.dtype)
    o = (jnp.einsum('brhqjw,brjwhd->brqhd', p_loc, v_blk)
         + jnp.einsum('brhqc,bchd->brqhd', p_ctx, cv))
    return o.reshape(B, L, H * dh)


def windowed_attention(q, k, v, ck, cv, sink):
    B, L, H, dh = q.shape
    KV = k.shape[2]
    G = H // KV
    nb = L // WINDOW
    scale = HEAD_DIM ** -0.5
    qb = q.reshape(B, nb, WINDOW, KV, G, dh)

    def band(t):
        tp = jnp.pad(t, ((0, 0), (WINDOW, WINDOW), (0, 0), (0, 0))).reshape(B, nb + 2, WINDOW, KV, dh)
        return jnp.concatenate([tp[:, :-2], tp[:, 1:-1], tp[:, 2:]], axis=2)

    kb, vb = band(k), band(v)
    qpos = jnp.arange(L).reshape(nb, WINDOW)
    kpos = (jnp.arange(nb)[:, None] - 1) * WINDOW + jnp.arange(3 * WINDOW)[None, :]
    ok = ((jnp.abs(qpos[:, :, None] - kpos[:, None, :]) <= WINDOW)
          & (kpos[:, None, :] >= 0) & (kpos[:, None, :] < L))
    s_loc = jnp.einsum('bnqkgd,bnskd->bnkgqs', qb, kb).astype(jnp.float32) * scale
    s_loc = jnp.where(ok[None, :, None, None], s_loc, NEG)
    s_ctx = jnp.einsum('bnqkgd,bckd->bnkgqc', qb, ck).astype(jnp.float32) * scale
    sk = jnp.broadcast_to(sink.reshape(1, 1, KV, G, 1, 1).astype(jnp.float32), s_loc.shape[:-1] + (1,))
    p = jax.nn.softmax(jnp.concatenate([s_loc, s_ctx, sk], axis=-1), axis=-1)
    ns = 3 * WINDOW
    nc = ck.shape[1]
    o = (jnp.einsum('bnkgqs,bnskd->bnqkgd', p[..., :ns].astype(v.dtype), vb)
         + jnp.einsum('bnkgqc,bckd->bnqkgd', p[..., ns:ns + nc].astype(v.dtype), cv))
    return o.reshape(B, L, H * dh)


def merge_branches(u, o_a, o_b, o_c, w_br_a, w_br_b, w_br_c, w_gate, w_out):
    g_a, g_b, g_c = jnp.split(jax.nn.sigmoid(u @ w_gate), 3, axis=-1)
    m = g_a * (o_a @ w_br_a) + g_b * (o_b @ w_br_b) + g_c * (o_c @ w_br_c)
    return m @ w_out


def hierarchical_moe(x, w_route_group, w_route_expert, w_exp_gate, w_exp_up, w_exp_down):
    B, L, D = x.shape
    T = B * L
    xt = x.reshape(T, D)
    g_logits = (xt @ w_route_group).astype(jnp.float32)
    g_prob = jax.nn.softmax(g_logits, axis=-1)
    g_sel = jnp.argmax(g_logits, axis=-1).astype(jnp.int32)
    g_w = jnp.take_along_axis(g_prob, g_sel[:, None], axis=-1)
    e_logits = (xt @ w_route_expert).astype(jnp.float32).reshape(T, N_GROUPS, EXPERTS_PER_GROUP)
    e_logits = jnp.take_along_axis(e_logits, g_sel[:, None, None], axis=1)[:, 0]
    top_v, top_i = lax.top_k(e_logits, TOP_K)
    gates = g_w * jax.nn.softmax(top_v, axis=-1)
    expert_id = (g_sel[:, None] * EXPERTS_PER_GROUP + top_i.astype(jnp.int32)).reshape(-1)
    gate_flat = gates.reshape(-1)
    tok_flat = jnp.repeat(jnp.arange(T, dtype=jnp.int32), TOP_K)
    A = T * TOP_K
    order = jnp.argsort(expert_id)
    e_sorted = expert_id[order]
    counts = jnp.bincount(expert_id, length=N_EXPERTS)
    padded = (counts + MOE_BLOCK - 1) // MOE_BLOCK * MOE_BLOCK
    start = jnp.cumsum(counts) - counts
    pad_end = jnp.cumsum(padded)
    pad_start = pad_end - padded
    dest = pad_start[e_sorted] + jnp.arange(A) - start[e_sorted]
    n_blocks = -(-A // MOE_BLOCK) + N_EXPERTS
    P = n_blocks * MOE_BLOCK
    buf_tok = jnp.full((P,), T, jnp.int32).at[dest].set(tok_flat[order])
    buf_gate = jnp.zeros((P,), jnp.float32).at[dest].set(gate_flat[order])
    blk_expert = jnp.minimum(
        jnp.searchsorted(pad_end, jnp.arange(n_blocks) * MOE_BLOCK, side='right'), N_EXPERTS - 1)
    x_pad = jnp.concatenate([xt, jnp.zeros((1, D), xt.dtype)], axis=0)
    xb = x_pad[buf_tok].reshape(n_blocks, MOE_BLOCK, D)

    def run_block(args):
        xi, e = args
        hdn = jax.nn.silu(xi @ w_exp_gate[e]) * (xi @ w_exp_up[e])
        return hdn @ w_exp_down[e]

    yb = lax.map(run_block, (xb, blk_expert)).reshape(P, D)
    y = jax.ops.segment_sum(yb * buf_gate[:, None].astype(yb.dtype), buf_tok, num_segments=T + 1)[:T]
    return y.reshape(B, L, D)


def setup_inputs(seed: int = 0) -> dict:
    key = jax.random.key(seed)
    ks = jax.random.split(key, 24)
    n = jax.random.normal
    f = jnp.float32
    D = D_MODEL
    WA, WB, WC = HA * HEAD_DIM, HB * HEAD_DIM, HC * HEAD_DIM
    return {
        "x_prompt": n(ks[0], (BATCH, SEQ, D), f),
        "x_sample": n(ks[1], (DEC_BATCH, DEC_SEQ, D), f),
        "cache_ctx_k": n(ks[2], (DEC_BATCH, DEPTH, PAST_LEN, N_KV_ALL, HEAD_DIM), f),
        "cache_ctx_v": n(ks[3], (DEC_BATCH, DEPTH, PAST_LEN, N_KV_ALL, HEAD_DIM), f),
        "c": n(ks[4], (DEC_BATCH, D), f),
        "c_ctx": n(ks[5], (D,), f),
        "norm_mix": 1.0 + 0.05 * n(ks[6], (DEPTH, D), f),
        "norm_ffn": 1.0 + 0.05 * n(ks[7], (DEPTH, D), f),
        "w_ada": 0.5 * D ** -0.5 * n(ks[8], (DEPTH, D, 6 * D), f),
        "b_ada": 0.01 * n(ks[9], (DEPTH, 6 * D), f),
        "w_in": D ** -0.5 * n(ks[10], (DEPTH, D, IN_COLS), f),
        "qk_gain": 1.0 + 0.05 * n(ks[11], (DEPTH, 6, HEAD_DIM), f),
        "rpb": 0.5 * n(ks[12], (DEPTH, HA, 2 * NA_WIN_R - 1, 2 * NA_WIN_C - 1), f),
        "sink": n(ks[13], (DEPTH, HC), f),
        "w_br_a": WA ** -0.5 * n(ks[14], (DEPTH, WA, D), f),
        "w_br_b": WB ** -0.5 * n(ks[15], (DEPTH, WB, D), f),
        "w_br_c": WC ** -0.5 * n(ks[16], (DEPTH, WC, D), f),
        "w_gate": D ** -0.5 * n(ks[17], (DEPTH, D, 3 * D), f),
        "w_out": D ** -0.5 * n(ks[18], (DEPTH, D, D), f),
        "w_route_group": D ** -0.5 * n(ks[19], (DEPTH, D, N_GROUPS), f),
        "w_route_expert": D ** -0.5 * n(ks[20], (DEPTH, D, N_GROUPS * EXPERTS_PER_GROUP), f),
        "w_exp_gate": D ** -0.5 * n(ks[21], (DEPTH, N_EXPERTS, D, EXPERT_FF), f),
        "w_exp_up": D ** -0.5 * n(ks[22], (DEPTH, N_EXPERTS, D, EXPERT_FF), f),
        "w_exp_down": EXPERT_FF ** -0.5 * n(ks[23], (DEPTH, N_EXPERTS, EXPERT_FF, D), f),
    }


def reference(x_prompt, x_sample, cache_ctx_k, cache_ctx_v, c, c_ctx, norm_mix, norm_ffn, w_ada, b_ada,
              w_in, qk_gain, rpb, sink, w_br_a, w_br_b, w_br_c, w_gate, w_out,
              w_route_group, w_route_expert, w_exp_gate, w_exp_up, w_exp_down):
    x = x_prompt
    new_k, new_v = [], []
    for l in range(DEPTH):
        sh1, sc1, g1, sh2, sc2, g2 = modulation(c_ctx, w_ada[l], b_ada[l])
        u = rms_norm(x, norm_mix[l]) * (1.0 + sc1) + sh1
        qa, qb, qc, k_all, v_all = project_heads(u, w_in[l], qk_gain[l])
        new_k.append(k_all)
        new_v.append(v_all)
        ka, kb, kc = split_kv(k_all)
        va, vb, vc = split_kv(v_all)
        o_a = swept_gqa(qa, ka, va)
        o_b = swept_gqa(qb, kb, vb)
        o_c = swept_gqa(qc, kc, vc, sink[l])
        x = x + g1 * merge_branches(u, o_a, o_b, o_c, w_br_a[l], w_br_b[l], w_br_c[l], w_gate[l], w_out[l])
        h = rms_norm(x, norm_ffn[l]) * (1.0 + sc2) + sh2
        x = x + g2 * hierarchical_moe(h, w_route_group[l], w_route_expert[l],
                                      w_exp_gate[l], w_exp_up[l], w_exp_down[l])
    y_prompt = x
    new_ctx_k = jnp.stack(new_k, axis=1)
    new_ctx_v = jnp.stack(new_v, axis=1)

    L = x_sample.shape[1]
    t = jnp.arange(L)
    rows, cols = t // GRID_W, t % GRID_W
    x = x_sample
    for l in range(DEPTH):
        sh1, sc1, g1, sh2, sc2, g2 = modulation(c, w_ada[l], b_ada[l])
        u = rms_norm(x, norm_mix[l]) * (1.0 + sc1) + sh1
        qa, qb, qc, k_all, v_all = project_heads(u, w_in[l], qk_gain[l])
        ka, kb, kc = split_kv(k_all)
        va, vb, vc = split_kv(v_all)
        cka, ckb, ckc = split_kv(cache_ctx_k[:, l])
        cva, cvb, cvc = split_kv(cache_ctx_v[:, l])
        o_a = neighbourhood_attention(qa, ka, va, cka, cva, rpb[l])
        k_b = jnp.concatenate([axial_rope(kb, rows, cols), ckb], axis=1)
        v_b = jnp.concatenate([vb, cvb], axis=1)
        o_b = swept_gqa(axial_rope(qb, rows, cols), k_b, v_b)
        o_c = windowed_attention(axial_rope(qc, rows, cols), axial_rope(kc, rows, cols), vc, ckc, cvc, sink[l])
        x = x + g1 * merge_branches(u, o_a, o_b, o_c, w_br_a[l], w_br_b[l], w_br_c[l], w_gate[l], w_out[l])
        h = rms_norm(x, norm_ffn[l]) * (1.0 + sc2) + sh2
        x = x + g2 * hierarchical_moe(h, w_route_group[l], w_route_expert[l],
                                      w_exp_gate[l], w_exp_up[l], w_exp_down[l])
    y_sample = x
    return (y_prompt, y_sample, new_ctx_k, new_ctx_v)
```

```python
import functools

import jax
import jax.numpy as jnp
import numpy as np
from jax import lax
from jax.experimental import pallas as pl
from jax.experimental.pallas import tpu as pltpu

GRID_W = 64
HEAD_DIM = 64
HA, HB, KVB, HC, KVC = 4, 8, 2, 4, 2
NA_WIN_R, NA_WIN_C = 8, 16
WINDOW = 128
ROPE_THETA = 10000.0
N_GROUPS, EXPERTS_PER_GROUP, N_EXPERTS = 4, 8, 32
EPS = 1e-6
NEG = -1e30

LANES = 128
V7X_VMEM_LIMIT = 56 * 1024 * 1024

TM = 512
TQ_B = 256
TK_B = 512
TQ_C = 256
ROWS_A = 4
BM = 256
MOE_ROWS_UNROLL = 8

F32 = jnp.float32
BF16 = jnp.bfloat16


def _cparams(sem, vmem=V7X_VMEM_LIMIT):
    return pltpu.CompilerParams(dimension_semantics=sem, vmem_limit_bytes=vmem)


def _split_bf16(x):
    hi = x.astype(BF16)
    lo = (x - hi.astype(F32)).astype(BF16)
    return hi, lo


def _dot(a, b):
    return jnp.dot(a, b, preferred_element_type=F32)


def _dot_nt(a, b):
    return lax.dot_general(a, b, (((1,), (1,)), ((), ())), preferred_element_type=F32)


def _half_masks(dtype):
    lane = lax.broadcasted_iota(jnp.int32, (1, LANES), 1)
    lo = (lane < HEAD_DIM)
    return lo, jnp.where(lo, 1.0, 0.0).astype(dtype), jnp.where(lo, 0.0, 1.0).astype(dtype)


def _ada_kernel(c_ref, w_ref, b_ref, o_ref):
    c = c_ref[...]
    s = c * (1.0 / (1.0 + jnp.exp(-c)))
    s_hi, s_lo = _split_bf16(s)
    w_hi, w_lo = _split_bf16(w_ref[...])
    o_ref[...] = _dot(s_hi, w_hi) + _dot(s_lo, w_hi) + _dot(s_hi, w_lo) + b_ref[...]


def _ada_call(cvec, w_ada, b_ada):
    depth, d, n6 = w_ada.shape
    tn = 512
    return pl.pallas_call(
        _ada_kernel,
        grid=(depth, n6 // tn),
        in_specs=[
            pl.BlockSpec((8, d), lambda l, j: (0, 0)),
            pl.BlockSpec((None, d, tn), lambda l, j: (l, 0, j)),
            pl.BlockSpec((None, 1, tn), lambda l, j: (l, 0, j)),
        ],
        out_specs=pl.BlockSpec((None, 8, tn), lambda l, j: (l, 0, j)),
        out_shape=jax.ShapeDtypeStruct((depth, 8, n6), F32),
        compiler_params=_cparams(("arbitrary", "arbitrary")),
        name="ada_modulation",
    )(cvec, w_ada, b_ada.reshape(depth, 1, n6))


N_QCHUNK = 8
N_KCHUNK = 4
ROPE_QCHUNKS = (0, 1, 2, 3, 6, 7)
ROPE_KCHUNKS = (2, 3)


def _modulated_norm(x, gain, scale, shift):
    ms = jnp.mean(x * x, axis=-1, keepdims=True)
    return (x * lax.rsqrt(ms + EPS) * gain) * (1.0 + scale) + shift


def _qkv_kernel(*refs, rope, emit_f32):
    x_ref, mod_ref, nm_ref, w_ref, gain_ref = refs[:5]
    pos = 5
    if rope:
        cos_ref, sin_ref = refs[pos:pos + 2]
        pos += 2
    q_ref, k_ref, v_ref = refs[pos:pos + 3]
    pos += 3
    if emit_f32:
        k32_ref, v32_ref = refs[pos:pos + 2]

    x = x_ref[...]
    u = _modulated_norm(x, nm_ref[...], mod_ref[0, 1:2, :], mod_ref[0, 0:1, :])
    y = _dot(u.astype(BF16), w_ref[...])

    lane = lax.broadcasted_iota(jnp.int32, (1, LANES), 1)
    lo = lane < HEAD_DIM
    first = (lane % 32) < 16
    if rope:
        cos_t = cos_ref[...]
        sin_t = sin_ref[...]

    def norm_chunk(j, use_rope):
        yj = y[:, j * LANES:(j + 1) * LANES]
        sq = yj * yj
        s_lo = jnp.sum(jnp.where(lo, sq, 0.0), axis=-1, keepdims=True)
        s_hi = jnp.sum(jnp.where(lo, 0.0, sq), axis=-1, keepdims=True)
        inv = lax.rsqrt(jnp.where(lo, s_lo, s_hi) * (1.0 / HEAD_DIM) + EPS)
        yj = (yj * inv) * gain_ref[:, j * LANES:(j + 1) * LANES]
        if use_rope:
            partner = jnp.where(first, pltpu.roll(yj, LANES - 16, axis=1), pltpu.roll(yj, 16, axis=1))
            yj = yj * cos_t + partner * sin_t
        return yj

    for j in range(N_QCHUNK):
        yj = norm_chunk(j, rope and j in ROPE_QCHUNKS)
        q_ref[:, j * LANES:(j + 1) * LANES] = yj.astype(BF16)
    for j in range(N_KCHUNK):
        yj = norm_chunk(N_QCHUNK + j, rope and j in ROPE_KCHUNKS)
        k_ref[:, j * LANES:(j + 1) * LANES] = yj.astype(BF16)
        if emit_f32:
            k32_ref[:, j * LANES:(j + 1) * LANES] = yj
    v = y[:, (N_QCHUNK + N_KCHUNK) * LANES:]
    v_ref[...] = v.astype(BF16)
    if emit_f32:
        v32_ref[...] = v


def _qkv_call(x_all, mods, norm_mix_l, w_in_p, gain_row, *, tile_off, n_tok, seq_len, row_off,
              per_seq_mod, rope_tabs, emit_f32):
    d = x_all.shape[1]
    nt = n_tok // TM
    tiles_per_seq = max(seq_len // TM, 1)
    rope = rope_tabs is not None

    def mod_map(i):
        return (row_off + i // tiles_per_seq, 0, 0) if per_seq_mod else (row_off, 0, 0)

    in_specs = [
        pl.BlockSpec((TM, d), lambda i: (i + tile_off, 0)),
        pl.BlockSpec((1, 6, d), mod_map),
        pl.BlockSpec((1, d), lambda i: (0, 0)),
        pl.BlockSpec(w_in_p.shape, lambda i: (0, 0)),
        pl.BlockSpec((1, 2048), lambda i: (0, 0)),
    ]
    args = [x_all, mods, norm_mix_l, w_in_p, gain_row]
    if rope:
        in_specs += [pl.BlockSpec((TM, LANES), lambda i: (i % tiles_per_seq, 0))] * 2
        args += list(rope_tabs)
    out_shape = [jax.ShapeDtypeStruct((n_tok, 1024), BF16),
                 jax.ShapeDtypeStruct((n_tok, 512), BF16),
                 jax.ShapeDtypeStruct((n_tok, 512), BF16)]
    out_specs = [pl.BlockSpec((TM, 1024), lambda i: (i, 0)),
                 pl.BlockSpec((TM, 512), lambda i: (i, 0)),
                 pl.BlockSpec((TM, 512), lambda i: (i, 0))]
    if emit_f32:
        out_shape += [jax.ShapeDtypeStruct((n_tok, 512), F32)] * 2
        out_specs += [pl.BlockSpec((TM, 512), lambda i: (i, 0))] * 2
    return pl.pallas_call(
        functools.partial(_qkv_kernel, rope=rope, emit_f32=emit_f32),
        grid=(nt,),
        in_specs=in_specs,
        out_specs=out_specs,
        out_shape=out_shape,
        compiler_params=_cparams(("parallel",)),
        name="qkv_rope" if rope else "qkv_ctx",
    )(*args)


def _softmax_rows(s, sink_col=None):
    m = jnp.max(s, axis=-1, keepdims=True)
    if sink_col is not None:
        m = jnp.maximum(m, sink_col)
    p = jnp.exp(s - m)
    l = jnp.sum(p, axis=-1, keepdims=True)
    if sink_col is not None:
        l = l + jnp.exp(sink_col - m)
    return p, l


def _ctx_attn_kernel(sink_ref, q_ref, k_ref, v_ref, o_ref):
    s_len = q_ref.shape[0]
    _, m_lo, m_hi = _half_masks(BF16)
    lo = lax.broadcasted_iota(jnp.int32, (1, LANES), 1) < HEAD_DIM

    def group(q_chunks, kc, sinks=None):
        n = len(q_chunks)
        k = k_ref[:, kc * LANES:(kc + 1) * LANES]
        v = v_ref[:, kc * LANES:(kc + 1) * LANES]
        qs = jnp.concatenate([q_ref[:, c * LANES:(c + 1) * LANES] * m for m in (m_lo, m_hi) for c in q_chunks],
                             axis=0)
        s = _dot_nt(qs, k)
        sink_col = None
        if sinks is not None:
            row = lax.broadcasted_iota(jnp.int32, (2 * n * s_len, 1), 0)
            sink_col = jnp.zeros((2 * n * s_len, 1), F32)
            for idx, sv in enumerate(sinks):
                sink_col = jnp.where((row >= idx * s_len) & (row < (idx + 1) * s_len), sv, sink_col)
        p, l = _softmax_rows(s, sink_col)
        o = _dot(p.astype(BF16), v) / l
        for idx, c in enumerate(q_chunks):
            o_c = jnp.where(lo, o[idx * s_len:(idx + 1) * s_len], o[(n + idx) * s_len:(n + idx + 1) * s_len])
            o_ref[:, c * LANES:(c + 1) * LANES] = o_c.astype(BF16)

    group([0, 1, 2, 3], 2)
    group([4], 0)
    group([5], 1)
    group([6, 7], 3, sinks=[sink_ref[0], sink_ref[1], sink_ref[2], sink_ref[3]])


def _ctx_attn_call(sink_l, q, k, v, s_len):
    batch = q.shape[0] // s_len
    return pl.pallas_call(
        _ctx_attn_kernel,
        grid_spec=pltpu.PrefetchScalarGridSpec(
            num_scalar_prefetch=1,
            grid=(batch,),
            in_specs=[pl.BlockSpec((s_len, 1024), lambda b, s: (b, 0)),
                      pl.BlockSpec((s_len, 512), lambda b, s: (b, 0)),
                      pl.BlockSpec((s_len, 512), lambda b, s: (b, 0))],
            out_specs=pl.BlockSpec((s_len, 1024), lambda b, s: (b, 0)),
        ),
        out_shape=jax.ShapeDtypeStruct((q.shape[0], 1024), BF16),
        compiler_params=_cparams(("parallel",)),
        name="ctx_attention",
    )(sink_l, q, k, v)


def _dec_b_kernel(q_ref, k_ref, v_ref, o_ref, m_sc, acc_sc):
    n_keys = k_ref.shape[0]
    n_chunks = n_keys // TK_B
    _, m_lo, m_hi = _half_masks(BF16)
    lo = lax.broadcasted_iota(jnp.int32, (1, LANES), 1) < HEAD_DIM
    outs = []
    for kvh, msk in enumerate((m_lo, m_hi)):
        qs = jnp.concatenate([q_ref[:, c * LANES:(c + 1) * LANES] * msk for c in range(4)], axis=0)
        m_sc[...] = jnp.full(m_sc.shape, -jnp.inf, F32)
        acc_sc[...] = jnp.zeros(acc_sc.shape, F32)

        def body(ci, carry):
            off = pl.multiple_of(ci * TK_B, TK_B)
            s = _dot_nt(qs, k_ref[pl.ds(off, TK_B), :])
            m_old = m_sc[...]
            m_new = jnp.maximum(m_old, jnp.max(s, axis=-1, keepdims=True))
            p = jnp.exp(s - m_new).astype(BF16)
            acc_sc[...] = jnp.exp(m_old - m_new) * acc_sc[...] + _dot(p, v_ref[pl.ds(off, TK_B), :])
            m_sc[...] = m_new
            return carry

        lax.fori_loop(0, n_chunks, body, 0)
        acc = acc_sc[...]
        outs.append(acc[:, :LANES] / acc[:, LANES:])
    for c in range(4):
        o_c = jnp.where(lo, outs[0][c * TQ_B:(c + 1) * TQ_B], outs[1][c * TQ_B:(c + 1) * TQ_B])
        o_ref[:, c * LANES:(c + 1) * LANES] = o_c.astype(BF16)


def _dec_b_call(q, k_ext, v_ext):
    nb, n_keys, _ = k_ext.shape
    seq = q.shape[0] // nb
    nq = seq // TQ_B
    return pl.pallas_call(
        _dec_b_kernel,
        grid=(nb, nq),
        in_specs=[pl.BlockSpec((TQ_B, 512), lambda b, i: (b * nq + i, 0)),
                  pl.BlockSpec((None, n_keys, LANES), lambda b, i: (b, 0, 0)),
                  pl.BlockSpec((None, n_keys, 2 * LANES), lambda b, i: (b, 0, 0))],
        out_specs=pl.BlockSpec((TQ_B, 512), lambda b, i: (b * nq + i, 0)),
        out_shape=jax.ShapeDtypeStruct((q.shape[0], 512), BF16),
        scratch_shapes=[pltpu.VMEM((4 * TQ_B, 1), F32), pltpu.VMEM((4 * TQ_B, 2 * LANES), F32)],
        compiler_params=_cparams(("parallel", "parallel")),
        name="dec_dense_attention",
    )(q, k_ext, v_ext)


def _dec_c_kernel(sink_ref, q_ref, k_ref, v_ref, o_ref, *, seq):
    qi = pl.program_id(1)
    n_ctx = k_ref.shape[0] - seq
    win = TQ_C + 2 * WINDOW
    _, m_lo, m_hi = _half_masks(BF16)
    lo = lax.broadcasted_iota(jnp.int32, (1, LANES), 1) < HEAD_DIM
    start = jnp.clip(qi * TQ_C - WINDOW, 0, seq - win)
    start = pl.multiple_of(start, WINDOW)
    k_win = k_ref[pl.ds(start, win), :]
    v_win = v_ref[pl.ds(start, win), :]
    k_ctx = k_ref[seq:seq + n_ctx, :]
    v_ctx = v_ref[seq:seq + n_ctx, :]
    row = lax.broadcasted_iota(jnp.int32, (2 * TQ_C, win), 0)
    col = lax.broadcasted_iota(jnp.int32, (2 * TQ_C, win), 1)
    qpos = qi * TQ_C + jnp.where(row >= TQ_C, row - TQ_C, row)
    ok = jnp.abs(qpos - (start + col)) <= WINDOW
    row1 = lax.broadcasted_iota(jnp.int32, (2 * TQ_C, 1), 0)
    outs = []
    for kvh, msk in enumerate((m_lo, m_hi)):
        qs = jnp.concatenate([q_ref[:, c * LANES:(c + 1) * LANES] * msk for c in range(2)], axis=0)
        s_loc = jnp.where(ok, _dot_nt(qs, k_win), NEG)
        s_ctx = _dot_nt(qs, k_ctx)
        sink_col = jnp.where(row1 < TQ_C, sink_ref[2 * kvh], sink_ref[2 * kvh + 1])
        m = jnp.maximum(jnp.maximum(jnp.max(s_loc, axis=-1, keepdims=True),
                                    jnp.max(s_ctx, axis=-1, keepdims=True)), sink_col)
        acc = (_dot(jnp.exp(s_loc - m).astype(BF16), v_win)
               + _dot(jnp.exp(s_ctx - m).astype(BF16), v_ctx))
        outs.append(acc[:, :LANES] / (acc[:, LANES:] + jnp.exp(sink_col - m)))
    for c in range(2):
        o_c = jnp.where(lo, outs[0][c * TQ_C:(c + 1) * TQ_C], outs[1][c * TQ_C:(c + 1) * TQ_C])
        o_ref[:, c * LANES:(c + 1) * LANES] = o_c.astype(BF16)


def _dec_c_call(sink_l, q, k_ext, v_ext):
    nb, n_keys, _ = k_ext.shape
    seq = q.shape[0] // nb
    nq = seq // TQ_C
    return pl.pallas_call(
        functools.partial(_dec_c_kernel, seq=seq),
        grid_spec=pltpu.PrefetchScalarGridSpec(
            num_scalar_prefetch=1,
            grid=(nb, nq),
            in_specs=[pl.BlockSpec((TQ_C, 256), lambda b, i, s: (b * nq + i, 3)),
                      pl.BlockSpec((None, n_keys, LANES), lambda b, i, s: (b, 0, 0)),
                      pl.BlockSpec((None, n_keys, 2 * LANES), lambda b, i, s: (b, 0, 0))],
            out_specs=pl.BlockSpec((TQ_C, 256), lambda b, i, s: (b * nq + i, 0)),
        ),
        out_shape=jax.ShapeDtypeStruct((q.shape[0], 256), BF16),
        compiler_params=_cparams(("parallel", "parallel")),
        name="dec_window_attention",
    )(sink_l, q, k_ext, v_ext)


N_DR = 2 * NA_WIN_R - 2


def _bias_kernel(rpb_ref, o_ref):
    h = pl.program_id(0)
    d = pl.program_id(1)
    n_dc = 2 * NA_WIN_C - 1
    qc = lax.broadcasted_iota(jnp.int32, (GRID_W, LANES), 0)
    kk = lax.broadcasted_iota(jnp.int32, (GRID_W, LANES), 1)
    half = kk >= GRID_W
    kc = jnp.where(half, kk - GRID_W, kk)
    delta = jnp.clip(kc - qc, -(NA_WIN_C - 1), NA_WIN_C - 1) + (NA_WIN_C - 1)
    col_start = jnp.clip(qc - NA_WIN_C // 2, 0, GRID_W - NA_WIN_C)
    ok = (kc >= col_start) & (kc < col_start + NA_WIN_C)
    acc = jnp.zeros((GRID_W, LANES), F32)
    base = (h * (2 * NA_WIN_R - 1) + d) * n_dc
    for dc in range(n_dc):
        val = jnp.where(half, rpb_ref[base + n_dc + dc], rpb_ref[base + dc])
        acc = jnp.where(delta == dc, val, acc)
    o_ref[...] = jnp.where(ok, acc, NEG)


def _bias_call(rpb_l):
    return pl.pallas_call(
        _bias_kernel,
        grid_spec=pltpu.PrefetchScalarGridSpec(
            num_scalar_prefetch=1,
            grid=(HA, N_DR),
            in_specs=[],
            out_specs=pl.BlockSpec((None, None, GRID_W, LANES), lambda h, d, r: (h, d, 0, 0)),
        ),
        out_shape=jax.ShapeDtypeStruct((HA, N_DR, GRID_W, LANES), F32),
        compiler_params=_cparams(("arbitrary", "arbitrary")),
        name="na_bias_table",
    )(rpb_l.reshape(-1))


def _dec_a_kernel(q_ref, k_ref, v_ref, bias_ref, o_ref, *, seq):
    rb = pl.program_id(2)
    rows = seq // GRID_W
    n_ctx = k_ref.shape[0] - seq
    n_loc = NA_WIN_R * GRID_W
    _, m_lo, m_hi = _half_masks(BF16)
    lo = lax.broadcasted_iota(jnp.int32, (1, LANES), 1) < HEAD_DIM
    k_ctx = k_ref[seq:seq + n_ctx, :]
    v_ctx = v_ref[seq:seq + n_ctx, :]
    for rr in range(ROWS_A):
        r = rb * ROWS_A + rr
        start = jnp.clip(r - NA_WIN_R // 2, 0, rows - NA_WIN_R)
        d0 = start - r + (NA_WIN_R - 1)
        off = pl.multiple_of(start * GRID_W, GRID_W)
        k_win = k_ref[pl.ds(off, n_loc), :]
        v_win = v_ref[pl.ds(off, n_loc), :]
        q_row = q_ref[rr * GRID_W:(rr + 1) * GRID_W, :]
        qs = jnp.concatenate([q_row * m_lo, q_row * m_hi], axis=0)
        bias = jnp.concatenate(
            [jnp.concatenate([bias_ref[hh, d0 + 2 * i] for i in range(NA_WIN_R // 2)], axis=1) for hh in range(2)],
            axis=0)
        s_loc = _dot_nt(qs, k_win) + bias
        s_ctx = _dot_nt(qs, k_ctx)
        m = jnp.maximum(jnp.max(s_loc, axis=-1, keepdims=True), jnp.max(s_ctx, axis=-1, keepdims=True))
        p_loc = jnp.exp(s_loc - m)
        p_ctx = jnp.exp(s_ctx - m)
        l = jnp.sum(p_loc, axis=-1, keepdims=True) + jnp.sum(p_ctx, axis=-1, keepdims=True)
        o = (_dot(p_loc.astype(BF16), v_win) + _dot(p_ctx.astype(BF16), v_ctx)) / l
        o_c = jnp.where(lo, o[:GRID_W], o[GRID_W:])
        o_ref[rr * GRID_W:(rr + 1) * GRID_W, :] = o_c.astype(BF16)


def _dec_a_call(q, k_ext, v_ext, bias_tab):
    nb, n_keys, _ = k_ext.shape
    seq = q.shape[0] // nb
    tq = ROWS_A * GRID_W
    nq = seq // tq
    return pl.pallas_call(
        functools.partial(_dec_a_kernel, seq=seq),
        grid=(2, nb, nq),
        in_specs=[pl.BlockSpec((tq, LANES), lambda p, b, i: (b * nq + i, 4 + p)),
                  pl.BlockSpec((None, n_keys, LANES), lambda p, b, i: (b, 0, p)),
                  pl.BlockSpec((None, n_keys, LANES), lambda p, b, i: (b, 0, p)),
                  pl.BlockSpec((2, N_DR, GRID_W, LANES), lambda p, b, i: (p, 0, 0, 0))],
        out_specs=pl.BlockSpec((tq, LANES), lambda p, b, i: (b * nq + i, p)),
        out_shape=jax.ShapeDtypeStruct((q.shape[0], 2 * LANES), BF16),
        compiler_params=_cparams(("parallel", "parallel", "parallel")),
        name="dec_neighbourhood_attention",
    )(q, k_ext, v_ext, bias_tab)


def _merge_kernel(x_ref, oc_ref, ob_ref, oa_ref, ow_ref, mod_ref, nm_ref, nf_ref, wg_ref, wbb_ref, wba_ref, wbc_ref,
                  wo_ref, wr_hi_ref, wr_lo_ref, xo_ref, h_ref, lg_ref, *, nt_c):
    d = x_ref.shape[1]
    is_ctx = pl.program_id(0) < nt_c
    o_b = jnp.where(is_ctx, oc_ref[:, :512], ob_ref[...])
    o_a = jnp.where(is_ctx, oc_ref[:, 512:768], oa_ref[...])
    o_c = jnp.where(is_ctx, oc_ref[:, 768:], ow_ref[...])
    x = x_ref[...]
    u = _modulated_norm(x, nm_ref[...], mod_ref[0, 1:2, :], mod_ref[0, 0:1, :])
    gates = _dot(u.astype(BF16), wg_ref[...])
    gates = 1.0 / (1.0 + jnp.exp(-gates))
    m = (gates[:, :d] * _dot(o_a, wba_ref[...])
         + gates[:, d:2 * d] * _dot(o_b, wbb_ref[...])
         + gates[:, 2 * d:] * _dot(o_c, wbc_ref[...]))
    x_new = x + mod_ref[0, 2:3, :] * _dot(m.astype(BF16), wo_ref[...])
    xo_ref[...] = x_new
    h = _modulated_norm(x_new, nf_ref[...], mod_ref[0, 4:5, :], mod_ref[0, 3:4, :])
    h_ref[...] = h
    h_hi, h_lo = _split_bf16(h)
    w_hi = wr_hi_ref[...]
    lg_ref[...] = _dot(h_hi, w_hi) + _dot(h_lo, w_hi) + _dot(h_hi, wr_lo_ref[...])


def _merge_call(x_all, o_ctx, o_b, o_a, o_c, mods, nm, nf, wg, wbb, wba, wbc, wo, wr_hi, wr_lo, mod_map):
    t, d = x_all.shape
    nt_c = o_ctx.shape[0] // TM
    const = lambda i: (0, 0)
    full = lambda a: pl.BlockSpec(a.shape, const)
    ctx_map = lambda i: (jnp.minimum(i, nt_c - 1), 0)
    dec_map = lambda i: (jnp.maximum(i - nt_c, 0), 0)
    return pl.pallas_call(
        functools.partial(_merge_kernel, nt_c=nt_c),
        grid=(t // TM,),
        in_specs=[pl.BlockSpec((TM, d), lambda i: (i, 0)),
                  pl.BlockSpec((TM, 1024), ctx_map),
                  pl.BlockSpec((TM, 512), dec_map),
                  pl.BlockSpec((TM, 256), dec_map),
                  pl.BlockSpec((TM, 256), dec_map),
                  pl.BlockSpec((1, 6, d), mod_map),
                  full(nm), full(nf), full(wg), full(wbb), full(wba), full(wbc), full(wo),
                  full(wr_hi), full(wr_lo)],
        out_specs=[pl.BlockSpec((TM, d), lambda i: (i, 0)),
                   pl.BlockSpec((TM, d), lambda i: (i, 0)),
                   pl.BlockSpec((TM, LANES), lambda i: (i, 0))],
        out_shape=[jax.ShapeDtypeStruct((t, d), F32),
                   jax.ShapeDtypeStruct((t, d), F32),
                   jax.ShapeDtypeStruct((t, LANES), F32)],
        compiler_params=_cparams(("parallel",)),
        name="merge_branches",
    )(x_all, o_ctx, o_b, o_a, o_c, mods, nm, nf, wg, wbb, wba, wbc, wo, wr_hi, wr_lo)


def _route_kernel(lg_ref, ri_ref, rg_ref, tab_ref, blk_ref, cnt_sc, *, nb_max):
    i = pl.program_id(0)
    n = pl.num_programs(0)
    tm = lg_ref.shape[0]

    @pl.when(i == 0)
    def _():
        cnt_sc[...] = jnp.zeros(cnt_sc.shape, F32)

    lg = lg_ref[...]
    lane = lax.broadcasted_iota(jnp.int32, (tm, LANES), 1).astype(F32)
    big = float(LANES)
    neg_inf = -jnp.inf
    gl = jnp.where(lane < N_GROUPS, lg, neg_inf)
    gmax = jnp.max(gl, axis=-1, keepdims=True)
    g_sel = jnp.min(jnp.where(gl == gmax, lane, big), axis=-1, keepdims=True)
    g_w = 1.0 / jnp.sum(jnp.exp(gl - gmax), axis=-1, keepdims=True)
    e_lo = N_GROUPS + EXPERTS_PER_GROUP * g_sel
    el = jnp.where((lane >= e_lo) & (lane < e_lo + EXPERTS_PER_GROUP), lg, neg_inf)
    m1 = jnp.max(el, axis=-1, keepdims=True)
    i1 = jnp.min(jnp.where(el == m1, lane, big), axis=-1, keepdims=True)
    el2 = jnp.where(lane == i1, neg_inf, el)
    m2 = jnp.max(el2, axis=-1, keepdims=True)
    i2 = jnp.min(jnp.where(el2 == m2, lane, big), axis=-1, keepdims=True)
    e2x = jnp.exp(m2 - m1)
    den = 1.0 + e2x
    gate1 = g_w * (1.0 / den)
    gate2 = g_w * (e2x / den)
    hit1 = lane == i1
    hit2 = lane == i2
    onehot = jnp.where(hit1 | hit2, 1.0, 0.0).astype(BF16)
    rr = lax.broadcasted_iota(jnp.int32, (tm, tm), 0)
    cc = lax.broadcasted_iota(jnp.int32, (tm, tm), 1)
    tri = jnp.where(cc <= rr, 1.0, 0.0).astype(BF16)
    pref = _dot(tri, onehot) + cnt_sc[...]
    rank1 = jnp.sum(jnp.where(hit1, pref, 0.0), axis=-1, keepdims=True) - 1.0
    rank2 = jnp.sum(jnp.where(hit2, pref, 0.0), axis=-1, keepdims=True) - 1.0
    cnt_sc[...] = pref[tm - 1:tm, :]
    e1 = (i1 - N_GROUPS).astype(jnp.int32)
    e2 = (i2 - N_GROUPS).astype(jnp.int32)
    pk1 = (e1 << 16) | rank1.astype(jnp.int32)
    pk2 = (e2 << 16) | rank2.astype(jnp.int32)
    lane_i = lax.broadcasted_iota(jnp.int32, (tm, LANES), 1)
    ri_ref[...] = jnp.where(lane_i == 0, pk1, jnp.where(lane_i == 1, pk2, 0))
    rg_ref[...] = jnp.where(lane_i == 0, gate1, jnp.where(lane_i == 1, gate2, 0.0))

    @pl.when(i == n - 1)
    def _():
        cnt = cnt_sc[...]
        nblk = jnp.floor((cnt + (BM - 1.0)) * (1.0 / BM))
        l8 = lax.broadcasted_iota(jnp.int32, (8, LANES), 1)
        is_exp = (l8 >= N_GROUPS) & (l8 < N_GROUPS + N_EXPERTS)
        nblk8 = jnp.where(is_exp, jnp.broadcast_to(nblk, (8, LANES)), 0.0)
        r2 = lax.broadcasted_iota(jnp.int32, (LANES, LANES), 0)
        c2 = lax.broadcasted_iota(jnp.int32, (LANES, LANES), 1)
        upper = jnp.where(r2 <= c2, 1.0, 0.0).astype(BF16)
        hi, lo_ = _split_bf16(nblk8)
        blk_end = _dot(hi, upper) + _dot(lo_, upper)
        blk_start = blk_end - nblk8
        row8 = lax.broadcasted_iota(jnp.int32, (8, LANES), 0)
        tab = jnp.where(row8 == 0, blk_start * BM, jnp.where(row8 == 1, blk_end, cnt))
        tab_ref[...] = tab.astype(jnp.int32)
        bi = lax.broadcasted_iota(jnp.int32, (nb_max, LANES), 0).astype(F32)
        ends = blk_end[0:1, :]
        lb = lax.broadcasted_iota(jnp.int32, (nb_max, LANES), 1)
        done = jnp.where((lb >= N_GROUPS) & (lb < N_GROUPS + N_EXPERTS) & (ends <= bi), 1.0, 0.0)
        be = jnp.minimum(jnp.sum(done, axis=-1, keepdims=True), N_EXPERTS - 1.0)
        blk_ref[...] = jnp.broadcast_to(be, (nb_max, LANES)).astype(jnp.int32)


def _route_call(logits, nb_max):
    t = logits.shape[0]
    return pl.pallas_call(
        functools.partial(_route_kernel, nb_max=nb_max),
        grid=(t // TM,),
        in_specs=[pl.BlockSpec((TM, LANES), lambda i: (i, 0))],
        out_specs=[pl.BlockSpec((TM, LANES), lambda i: (i, 0)),
                   pl.BlockSpec((TM, LANES), lambda i: (i, 0)),
                   pl.BlockSpec((8, LANES), lambda i: (0, 0)),
                   pl.BlockSpec((nb_max, LANES), lambda i: (0, 0))],
        out_shape=[jax.ShapeDtypeStruct((t, LANES), jnp.int32),
                   jax.ShapeDtypeStruct((t, LANES), F32),
                   jax.ShapeDtypeStruct((8, LANES), jnp.int32),
                   jax.ShapeDtypeStruct((nb_max, LANES), jnp.int32)],
        scratch_shapes=[pltpu.VMEM((1, LANES), F32)],
        compiler_params=_cparams(("arbitrary",)),
        name="moe_route",
    )(logits)


def _slot(packed_ref, start_ref, a):
    pk = packed_ref[a]
    return start_ref[N_GROUPS + (pk >> 16)] + (pk & 0xFFFF)


def _dispatch_kernel(packed_ref, start_ref, xs_in_ref, h_ref, xs_ref, sem):
    del xs_in_ref
    i = pl.program_id(0)
    tm = h_ref.shape[0]

    def row_copy(r, k):
        dst = _slot(packed_ref, start_ref, 2 * (i * tm + r) + k)
        return pltpu.make_async_copy(h_ref.at[pl.ds(r, 1), :], xs_ref.at[pl.ds(dst, 1), :], sem)

    def body(j, carry):
        for rr in range(MOE_ROWS_UNROLL):
            r = j * MOE_ROWS_UNROLL + rr
            row_copy(r, 0).start()
            row_copy(r, 1).start()
        return carry

    lax.fori_loop(0, tm // MOE_ROWS_UNROLL, body, 0)
    for _ in range(2):
        pltpu.make_async_copy(h_ref, xs_ref.at[pl.ds(0, tm), :], sem).wait()


def _dispatch_call(packed, starts, xs_init, h):
    t, d = h.shape
    return pl.pallas_call(
        _dispatch_kernel,
        grid_spec=pltpu.PrefetchScalarGridSpec(
            num_scalar_prefetch=2,
            grid=(t // TM,),
            in_specs=[pl.BlockSpec(memory_space=pl.ANY),
                      pl.BlockSpec((TM, d), lambda i, p, s: (i, 0))],
            out_specs=pl.BlockSpec(memory_space=pl.ANY),
            scratch_shapes=[pltpu.SemaphoreType.DMA],
        ),
        out_shape=jax.ShapeDtypeStruct(xs_init.shape, xs_init.dtype),
        input_output_aliases={2: 0},
        compiler_params=_cparams(("arbitrary",)),
        name="moe_dispatch",
    )(packed, starts, xs_init, h)


def _expert_kernel(blk_ref, nused_ref, xs_ref, wg_ref, wu_ref, wd_ref, y_ref, wg_sc, wu_sc, wd_sc):
    i = pl.program_id(0)
    prev = blk_ref[jnp.maximum(i - 1, 0)]
    changed = (i == 0) | (blk_ref[i] != prev)

    @pl.when(changed)
    def _():
        wg_sc[...] = wg_ref[...].astype(BF16)
        wu_sc[...] = wu_ref[...].astype(BF16)
        wd_sc[...] = wd_ref[...].astype(BF16)

    @pl.when(i < nused_ref[0])
    def _():
        x = xs_ref[...].astype(BF16)
        g = _dot(x, wg_sc[...])
        u = _dot(x, wu_sc[...])
        hdn = (g * (1.0 / (1.0 + jnp.exp(-g)))) * u
        y_ref[...] = _dot(hdn.astype(BF16), wd_sc[...])

    @pl.when(i >= nused_ref[0])
    def _():
        y_ref[...] = jnp.zeros(y_ref.shape, F32)


def _expert_call(blk_expert, n_used, xs, wg, wu, wd, layer):
    p, d = xs.shape
    ff = wg.shape[-1]
    return pl.pallas_call(
        _expert_kernel,
        grid_spec=pltpu.PrefetchScalarGridSpec(
            num_scalar_prefetch=2,
            grid=(p // BM,),
            in_specs=[pl.BlockSpec((BM, d), lambda i, b, n: (i, 0)),
                      pl.BlockSpec((None, None, d, ff), lambda i, b, n: (layer, b[i], 0, 0)),
                      pl.BlockSpec((None, None, d, ff), lambda i, b, n: (layer, b[i], 0, 0)),
                      pl.BlockSpec((None, None, ff, d), lambda i, b, n: (layer, b[i], 0, 0))],
            out_specs=pl.BlockSpec((BM, d), lambda i, b, n: (i, 0)),
            scratch_shapes=[pltpu.VMEM((d, ff), BF16), pltpu.VMEM((d, ff), BF16), pltpu.VMEM((ff, d), BF16)],
        ),
        out_shape=jax.ShapeDtypeStruct((p, d), F32),
        compiler_params=_cparams(("arbitrary",)),
        name="moe_experts",
    )(blk_expert, n_used, xs, wg, wu, wd)


def _combine_kernel(packed_ref, start_ref, x_ref, rg_ref, mod_ref, y_ref, o_ref, buf, sem):
    i = pl.program_id(0)
    tm = x_ref.shape[0]

    def row_copy(r, k):
        src = _slot(packed_ref, start_ref, 2 * (i * tm + r) + k)
        return pltpu.make_async_copy(y_ref.at[pl.ds(src, 1), :], buf.at[k, pl.ds(r, 1), :], sem)

    def body(j, carry):
        for rr in range(MOE_ROWS_UNROLL):
            r = j * MOE_ROWS_UNROLL + rr
            row_copy(r, 0).start()
            row_copy(r, 1).start()
        return carry

    lax.fori_loop(0, tm // MOE_ROWS_UNROLL, body, 0)
    for k in range(2):
        pltpu.make_async_copy(y_ref.at[pl.ds(0, tm), :], buf.at[k], sem).wait()
    rg = rg_ref[...]
    y = buf[0] * rg[:, 0:1] + buf[1] * rg[:, 1:2]
    o_ref[...] = x_ref[...] + mod_ref[0, 5:6, :] * y


def _combine_call(packed, starts, x_new, route_g, mods, yb, mod_map):
    t, d = x_new.shape
    return pl.pallas_call(
        _combine_kernel,
        grid_spec=pltpu.PrefetchScalarGridSpec(
            num_scalar_prefetch=2,
            grid=(t // TM,),
            in_specs=[pl.BlockSpec((TM, d), lambda i, p, s: (i, 0)),
                      pl.BlockSpec((TM, LANES), lambda i, p, s: (i, 0)),
                      pl.BlockSpec((1, 6, d), lambda i, p, s: mod_map(i)),
                      pl.BlockSpec(memory_space=pl.ANY)],
            out_specs=pl.BlockSpec((TM, d), lambda i, p, s: (i, 0)),
            scratch_shapes=[pltpu.VMEM((2, TM, d), F32), pltpu.SemaphoreType.DMA],
        ),
        out_shape=jax.ShapeDtypeStruct((t, d), F32),
        compiler_params=_cparams(("arbitrary",)),
        name="moe_combine",
    )(packed, starts, x_new, route_g, mods, yb)


def _head_cols(base, heads):
    return np.concatenate([base + h * HEAD_DIM + np.arange(HEAD_DIM) for h in heads])


def _layouts():
    wa, wb, wc = HA * HEAD_DIM, HB * HEAD_DIM, HC * HEAD_DIM
    qa0, ka0, va0 = 0, wa, 2 * wa
    qb0 = 3 * wa
    kb0 = qb0 + wb
    vb0 = kb0 + KVB * HEAD_DIM
    qc0 = vb0 + KVB * HEAD_DIM
    kc0 = qc0 + wc
    vc0 = kc0 + KVC * HEAD_DIM
    b_heads = [h for c in range(4) for h in (c, c + 4)]
    c_heads = [0, 2, 1, 3]
    q_cols = np.concatenate([_head_cols(qb0, b_heads), _head_cols(qa0, range(HA)), _head_cols(qc0, c_heads)])
    k_cols = np.concatenate([ka0 + np.arange(wa), kb0 + np.arange(KVB * HEAD_DIM), kc0 + np.arange(KVC * HEAD_DIM)])
    v_cols = np.concatenate([va0 + np.arange(wa), vb0 + np.arange(KVB * HEAD_DIM), vc0 + np.arange(KVC * HEAD_DIM)])
    in_cols = np.concatenate([q_cols, k_cols, v_cols])
    gain_idx = np.concatenate([np.full(HB, 2), np.full(HA, 0), np.full(HC, 4),
                               np.full(HA, 1), np.full(KVB, 3), np.full(KVC, 5)])
    gain_scale = np.concatenate([np.full(HB + HA + HC, HEAD_DIM ** -0.5), np.ones(HA + KVB + KVC)])
    return in_cols, gain_idx, gain_scale.astype(np.float32), _head_cols(0, b_heads), _head_cols(0, c_heads)


def _rope_tables(seq):
    quarter = HEAD_DIM // 4
    t = jnp.arange(seq)
    rows, cols = t // GRID_W, t % GRID_W
    freqs = ROPE_THETA ** (-jnp.arange(quarter, dtype=F32) / quarter)
    ang_r = rows.astype(F32)[:, None] * freqs[None, :]
    ang_c = cols.astype(F32)[:, None] * freqs[None, :]
    cos = jnp.concatenate([jnp.cos(ang_r), jnp.cos(ang_r), jnp.cos(ang_c), jnp.cos(ang_c)], axis=1)
    sin = jnp.concatenate([-jnp.sin(ang_r), jnp.sin(ang_r), -jnp.sin(ang_c), jnp.sin(ang_c)], axis=1)
    return jnp.tile(cos, (1, 2)), jnp.tile(sin, (1, 2))


def kernel(x_prompt, x_sample, cache_ctx_k, cache_ctx_v, c, c_ctx, norm_mix, norm_ffn, w_ada, b_ada, w_in, qk_gain,
           rpb, sink, w_br_a, w_br_b, w_br_c, w_gate, w_out, w_route_group, w_route_expert, w_exp_gate, w_exp_up,
           w_exp_down):
    batch, seq_c, d = x_prompt.shape
    nb, seq_d, _ = x_sample.shape
    depth = w_in.shape[0]
    t_c, t_d = batch * seq_c, nb * seq_d
    t = t_c + t_d
    n_ctx = cache_ctx_k.shape[2]
    assert t_c % TM == 0 and seq_d % TM == 0 and nb + 1 <= 8 and 2 * t < (1 << 16)
    nt_c = t_c // TM
    tiles_per_seq = seq_d // TM
    nb_max = 2 * t // BM + N_EXPERTS
    nb_max = (nb_max + 7) // 8 * 8

    in_cols, gain_idx, gain_scale, ob_rows, oc_rows = _layouts()
    x_all = jnp.concatenate([x_prompt.reshape(t_c, d), x_sample.reshape(t_d, d)], axis=0)
    cvec = jnp.concatenate([c_ctx[None, :], c, jnp.zeros((8 - 1 - nb, d), F32)], axis=0)
    mods = _ada_call(cvec, w_ada, b_ada).reshape(depth, 8, 6, d)
    rope_tabs = _rope_tables(seq_d)

    def mod_map(i):
        return (jnp.where(i < nt_c, 0, 1 + (i - nt_c) // tiles_per_seq), 0, 0)

    new_k, new_v = [], []
    for l in range(depth):
        w_in_p = w_in[l][:, in_cols].astype(BF16)
        gain_row = (qk_gain[l][gain_idx] * gain_scale[:, None]).reshape(1, -1)
        gain_row = jnp.concatenate([gain_row, jnp.ones((1, 512), F32)], axis=1)
        nm = norm_mix[l].reshape(1, d)
        nf = norm_ffn[l].reshape(1, d)

        q_c, k_c, v_c, k32, v32 = _qkv_call(x_all, mods[l], nm, w_in_p, gain_row, tile_off=0, n_tok=t_c,
                                            seq_len=seq_c, row_off=0, per_seq_mod=False, rope_tabs=None, emit_f32=True)
        new_k.append(k32.reshape(batch, seq_c, 8, HEAD_DIM))
        new_v.append(v32.reshape(batch, seq_c, 8, HEAD_DIM))
        q_d, k_d, v_d = _qkv_call(x_all, mods[l], nm, w_in_p, gain_row, tile_off=nt_c, n_tok=t_d,
                                  seq_len=seq_d, row_off=1, per_seq_mod=True, rope_tabs=rope_tabs, emit_f32=False)

        o_ctx = _ctx_attn_call(sink[l], q_c, k_c, v_c, seq_c)
        ck = cache_ctx_k[:, l].reshape(nb, n_ctx, 512).astype(BF16)
        cv = cache_ctx_v[:, l].reshape(nb, n_ctx, 512).astype(BF16)
        k_ext = jnp.concatenate([k_d.reshape(nb, seq_d, 512), ck], axis=1)
        v_ext = jnp.concatenate([v_d.reshape(nb, seq_d, 512), cv], axis=1)
        ones = jnp.ones((nb, seq_d + n_ctx, LANES), BF16)
        o_b = _dec_b_call(q_d, k_ext[:, :, 256:384], jnp.concatenate([v_ext[:, :, 256:384], ones], axis=2))
        o_c = _dec_c_call(sink[l], q_d, k_ext[:, :, 384:512], jnp.concatenate([v_ext[:, :, 384:512], ones], axis=2))
        o_a = _dec_a_call(q_d, k_ext[:, :, :256], v_ext[:, :, :256], _bias_call(rpb[l]))

        w_route = jnp.concatenate([w_route_group[l], w_route_expert[l],
                                   jnp.zeros((d, LANES - N_GROUPS - N_EXPERTS), F32)], axis=1)
        wr_hi = w_route.astype(BF16)
        wr_lo = (w_route - wr_hi.astype(F32)).astype(BF16)
        x_new, h, logits = _merge_call(
            x_all, o_ctx, o_b, o_a, o_c, mods[l], nm, nf, w_gate[l].astype(BF16), w_br_b[l][ob_rows].astype(BF16),
            w_br_a[l].astype(BF16), w_br_c[l][oc_rows].astype(BF16), w_out[l].astype(BF16), wr_hi, wr_lo, mod_map)

        route_i, route_g, tab, blk = _route_call(logits, nb_max)
        packed = route_i[:, :2].reshape(-1)
        starts = tab[0]
        xs = _dispatch_call(packed, starts, jnp.zeros((nb_max * BM, d), F32), h)
        yb = _expert_call(blk[:, 0], tab[1, N_GROUPS + N_EXPERTS - 1:N_GROUPS + N_EXPERTS], xs,
                          w_exp_gate, w_exp_up, w_exp_down, l)
        x_all = _combine_call(packed, starts, x_new, route_g, mods[l], yb, mod_map)

    y_prompt = x_all[:t_c].reshape(batch, seq_c, d)
    y_sample = x_all[t_c:].reshape(nb, seq_d, d)
    return (y_prompt, y_sample, jnp.stack(new_k, axis=1), jnp.stack(new_v, axis=1))
```

```python
import functools

import jax
import jax.numpy as jnp
import numpy as np
from jax import lax
from jax.experimental import pallas as pl
from jax.experimental.pallas import tpu as pltpu

GRID_W = 64
HEAD_DIM = 64
HA, HB, KVB, HC, KVC = 4, 8, 2, 4, 2
NA_WIN_R, NA_WIN_C = 8, 16
WINDOW = 128
ROPE_THETA = 10000.0
N_GROUPS, EXPERTS_PER_GROUP, N_EXPERTS = 4, 8, 32
EPS = 1e-6
NEG = -1e30
LOG2E = 1.4426950408889634

LANES = 128
V7X_VMEM_LIMIT = 56 * 1024 * 1024

TM = 512
TQ_B = 128
TQ_C = 256
ROWS_A = 4
BM = 256

F32 = jnp.float32
BF16 = jnp.bfloat16


def _cparams(sem, vmem=V7X_VMEM_LIMIT):
    return pltpu.CompilerParams(dimension_semantics=sem, vmem_limit_bytes=vmem)


def _split_bf16(x):
    hi = x.astype(BF16)
    lo = (x - hi.astype(F32)).astype(BF16)
    return hi, lo


def _pack_bf16_pairs(x_bf16):
    n = x_bf16.shape[1] // 2
    lo = lax.bitcast_convert_type(x_bf16[:, :n].astype(F32), jnp.uint32)
    hi = lax.bitcast_convert_type(x_bf16[:, n:].astype(F32), jnp.uint32)
    return (lo >> 16) | hi


def _unpack_bf16_pairs(w_u32):
    lo = lax.bitcast_convert_type(w_u32 << 16, F32)
    hi = lax.bitcast_convert_type(w_u32 & jnp.uint32(0xFFFF0000), F32)
    return jnp.concatenate([lo, hi], axis=1).astype(BF16)


def _dot(a, b):
    return jnp.dot(a, b, preferred_element_type=F32)


def _dot_nt(a, b):
    return lax.dot_general(a, b, (((1,), (1,)), ((), ())), preferred_element_type=F32)


def _half_masks(dtype):
    lane = lax.broadcasted_iota(jnp.int32, (1, LANES), 1)
    lo = (lane < HEAD_DIM)
    return lo, jnp.where(lo, 1.0, 0.0).astype(dtype), jnp.where(lo, 0.0, 1.0).astype(dtype)


def _ada_kernel(c_ref, w_ref, b_ref, o_ref):
    c = c_ref[...]
    s = c * (1.0 / (1.0 + jnp.exp(-c)))
    s_hi, s_lo = _split_bf16(s)
    w_hi, w_lo = _split_bf16(w_ref[...])
    o_ref[...] = _dot(s_hi, w_hi) + _dot(s_lo, w_hi) + _dot(s_hi, w_lo) + b_ref[...]


def _ada_call(cvec, w_ada, b_ada):
    depth, d, n6 = w_ada.shape
    tn = 512
    return pl.pallas_call(
        _ada_kernel,
        grid=(depth, n6 // tn),
        in_specs=[
            pl.BlockSpec((8, d), lambda l, j: (0, 0)),
            pl.BlockSpec((None, d, tn), lambda l, j: (l, 0, j)),
            pl.BlockSpec((None, 1, tn), lambda l, j: (l, 0, j)),
        ],
        out_specs=pl.BlockSpec((None, 8, tn), lambda l, j: (l, 0, j)),
        out_shape=jax.ShapeDtypeStruct((depth, 8, n6), F32),
        compiler_params=_cparams(("arbitrary", "arbitrary")),
        name="ada_modulation",
    )(cvec, w_ada, b_ada.reshape(depth, 1, n6))


N_QCHUNK = 8
N_KCHUNK = 4
ROPE_QCHUNKS = (0, 1, 2, 3, 6, 7)
ROPE_KCHUNKS = (2, 3)


def _modulated_norm(x, gain, scale, shift):
    ms = jnp.mean(x * x, axis=-1, keepdims=True)
    return (x * lax.rsqrt(ms + EPS) * gain) * (1.0 + scale) + shift


def _qkv_kernel(*refs, rope, emit_f32):
    x_ref, mod_ref, nm_ref, w_ref, gain_ref = refs[:5]
    pos = 5
    if rope:
        cos_ref, sin_ref = refs[pos:pos + 2]
        pos += 2
    q_ref, k_ref, v_ref = refs[pos:pos + 3]
    pos += 3
    if emit_f32:
        k32_ref, v32_ref = refs[pos:pos + 2]

    x = x_ref[...]
    u = _modulated_norm(x, nm_ref[...], mod_ref[0, 1:2, :], mod_ref[0, 0:1, :])
    y = _dot(u.astype(BF16), w_ref[...])

    lane = lax.broadcasted_iota(jnp.int32, (1, LANES), 1)
    lo = lane < HEAD_DIM
    first = (lane % 32) < 16
    if rope:
        cos_t = cos_ref[...]
        sin_t = sin_ref[...]

    def norm_chunk(j, use_rope):
        yj = y[:, j * LANES:(j + 1) * LANES]
        sq = yj * yj
        s_lo = jnp.sum(jnp.where(lo, sq, 0.0), axis=-1, keepdims=True)
        s_hi = jnp.sum(jnp.where(lo, 0.0, sq), axis=-1, keepdims=True)
        inv = lax.rsqrt(jnp.where(lo, s_lo, s_hi) * (1.0 / HEAD_DIM) + EPS)
        yj = (yj * inv) * gain_ref[:, j * LANES:(j + 1) * LANES]
        if use_rope:
            partner = jnp.where(first, pltpu.roll(yj, LANES - 16, axis=1), pltpu.roll(yj, 16, axis=1))
            yj = yj * cos_t + partner * sin_t
        return yj

    for j in range(N_QCHUNK):
        yj = norm_chunk(j, rope and j in ROPE_QCHUNKS)
        q_ref[:, j * LANES:(j + 1) * LANES] = yj.astype(BF16)
    for j in range(N_KCHUNK):
        yj = norm_chunk(N_QCHUNK + j, rope and j in ROPE_KCHUNKS)
        k_ref[:, j * LANES:(j + 1) * LANES] = yj.astype(BF16)
        if emit_f32:
            k32_ref[:, j * LANES:(j + 1) * LANES] = yj
    v = y[:, (N_QCHUNK + N_KCHUNK) * LANES:]
    v_ref[...] = v.astype(BF16)
    if emit_f32:
        v32_ref[...] = v


def _qkv_call(x_all, mods, norm_mix_l, w_in_p, gain_row, *, tile_off, n_tok, seq_len, row_off,
              per_seq_mod, rope_tabs, emit_f32):
    d = x_all.shape[1]
    nt = n_tok // TM
    tiles_per_seq = max(seq_len // TM, 1)
    rope = rope_tabs is not None

    def mod_map(i):
        return (row_off + i // tiles_per_seq, 0, 0) if per_seq_mod else (row_off, 0, 0)

    in_specs = [
        pl.BlockSpec((TM, d), lambda i: (i + tile_off, 0)),
        pl.BlockSpec((1, 6, d), mod_map),
        pl.BlockSpec((1, d), lambda i: (0, 0)),
        pl.BlockSpec(w_in_p.shape, lambda i: (0, 0)),
        pl.BlockSpec((1, 2048), lambda i: (0, 0)),
    ]
    args = [x_all, mods, norm_mix_l, w_in_p, gain_row]
    if rope:
        in_specs += [pl.BlockSpec((TM, LANES), lambda i: (i % tiles_per_seq, 0))] * 2
        args += list(rope_tabs)
    out_shape = [jax.ShapeDtypeStruct((n_tok, 1024), BF16),
                 jax.ShapeDtypeStruct((n_tok, 512), BF16),
                 jax.ShapeDtypeStruct((n_tok, 512), BF16)]
    out_specs = [pl.BlockSpec((TM, 1024), lambda i: (i, 0)),
                 pl.BlockSpec((TM, 512), lambda i: (i, 0)),
                 pl.BlockSpec((TM, 512), lambda i: (i, 0))]
    if emit_f32:
        out_shape += [jax.ShapeDtypeStruct((n_tok, 512), F32)] * 2
        out_specs += [pl.BlockSpec((TM, 512), lambda i: (i, 0))] * 2
    return pl.pallas_call(
        functools.partial(_qkv_kernel, rope=rope, emit_f32=emit_f32),
        grid=(nt,),
        in_specs=in_specs,
        out_specs=out_specs,
        out_shape=out_shape,
        compiler_params=_cparams(("parallel",)),
        name="qkv_rope" if rope else "qkv_ctx",
    )(*args)


def _softmax_rows(s, sink_col=None):
    m = jnp.max(s, axis=-1, keepdims=True)
    if sink_col is not None:
        m = jnp.maximum(m, sink_col)
    p = jnp.exp2(s - m)
    l = jnp.sum(p, axis=-1, keepdims=True)
    if sink_col is not None:
        l = l + jnp.exp2(sink_col - m)
    return p, l


def _ctx_attn_kernel(sink_ref, q_ref, k_ref, v_ref, o_ref):
    s_len = q_ref.shape[0]
    _, m_lo, m_hi = _half_masks(BF16)
    lo = lax.broadcasted_iota(jnp.int32, (1, LANES), 1) < HEAD_DIM

    def group(q_chunks, kc, sinks=None):
        n = len(q_chunks)
        k = k_ref[:, kc * LANES:(kc + 1) * LANES]
        v = v_ref[:, kc * LANES:(kc + 1) * LANES]
        qs = jnp.concatenate([q_ref[:, c * LANES:(c + 1) * LANES] * m for m in (m_lo, m_hi) for c in q_chunks],
                             axis=0)
        s = _dot_nt(qs, k)
        sink_col = None
        if sinks is not None:
            row = lax.broadcasted_iota(jnp.int32, (2 * n * s_len, 1), 0)
            sink_col = jnp.zeros((2 * n * s_len, 1), F32)
            for idx, sv in enumerate(sinks):
                sink_col = jnp.where((row >= idx * s_len) & (row < (idx + 1) * s_len), sv, sink_col)
        p, l = _softmax_rows(s, sink_col)
        o = _dot(p.astype(BF16), v) / l
        for idx, c in enumerate(q_chunks):
            o_c = jnp.where(lo, o[idx * s_len:(idx + 1) * s_len], o[(n + idx) * s_len:(n + idx + 1) * s_len])
            o_ref[:, c * LANES:(c + 1) * LANES] = o_c.astype(BF16)

    group([0, 1, 2, 3], 2)
    group([4], 0)
    group([5], 1)
    group([6, 7], 3, sinks=[sink_ref[h] * LOG2E for h in range(HC)])


def _ctx_attn_call(sink_l, q, k, v, s_len):
    batch = q.shape[0] // s_len
    return pl.pallas_call(
        _ctx_attn_kernel,
        grid_spec=pltpu.PrefetchScalarGridSpec(
            num_scalar_prefetch=1,
            grid=(batch,),
            in_specs=[pl.BlockSpec((s_len, 1024), lambda b, s: (b, 0)),
                      pl.BlockSpec((s_len, 512), lambda b, s: (b, 0)),
                      pl.BlockSpec((s_len, 512), lambda b, s: (b, 0))],
            out_specs=pl.BlockSpec((s_len, 1024), lambda b, s: (b, 0)),
        ),
        out_shape=jax.ShapeDtypeStruct((q.shape[0], 1024), BF16),
        compiler_params=_cparams(("parallel",)),
        name="ctx_attention",
    )(sink_l, q, k, v)


def _dec_b_kernel(q_ref, k_ref, v_ref, o_ref):
    _, m_lo, m_hi = _half_masks(BF16)
    lo = lax.broadcasted_iota(jnp.int32, (1, LANES), 1) < HEAD_DIM
    k = k_ref[...]
    v = v_ref[...]
    outs = []
    for msk in (m_lo, m_hi):
        qs = jnp.concatenate([q_ref[:, c * LANES:(c + 1) * LANES] * msk for c in range(4)], axis=0)
        s = _dot_nt(qs, k)
        p = jnp.exp2(s - jnp.max(s, axis=-1, keepdims=True)).astype(BF16)
        acc = _dot(p, v)
        outs.append(acc[:, :LANES] / acc[:, LANES:])
    for c in range(4):
        o_c = jnp.where(lo, outs[0][c * TQ_B:(c + 1) * TQ_B], outs[1][c * TQ_B:(c + 1) * TQ_B])
        o_ref[:, c * LANES:(c + 1) * LANES] = o_c.astype(BF16)


def _dec_b_call(q, k_ext, v_ext):
    nb, n_keys, _ = k_ext.shape
    seq = q.shape[0] // nb
    nq = seq // TQ_B
    return pl.pallas_call(
        _dec_b_kernel,
        grid=(nb, nq),
        in_specs=[pl.BlockSpec((TQ_B, 512), lambda b, i: (b * nq + i, 0)),
                  pl.BlockSpec((None, n_keys, LANES), lambda b, i: (b, 0, 0)),
                  pl.BlockSpec((None, n_keys, 2 * LANES), lambda b, i: (b, 0, 0))],
        out_specs=pl.BlockSpec((TQ_B, 512), lambda b, i: (b * nq + i, 0)),
        out_shape=jax.ShapeDtypeStruct((q.shape[0], 512), BF16),
        compiler_params=_cparams(("parallel", "parallel")),
        name="dec_dense_attention",
    )(q, k_ext, v_ext)


def _dec_c_kernel(sink_ref, q_ref, k_ref, v_ref, o_ref, *, seq):
    qi = pl.program_id(1)
    n_ctx = k_ref.shape[0] - seq
    win = TQ_C + 2 * WINDOW
    _, m_lo, m_hi = _half_masks(BF16)
    lo = lax.broadcasted_iota(jnp.int32, (1, LANES), 1) < HEAD_DIM
    start = jnp.clip(qi * TQ_C - WINDOW, 0, seq - win)
    start = pl.multiple_of(start, WINDOW)
    k_win = k_ref[pl.ds(start, win), :]
    v_win = v_ref[pl.ds(start, win), :]
    k_ctx = k_ref[seq:seq + n_ctx, :]
    v_ctx = v_ref[seq:seq + n_ctx, :]
    row = lax.broadcasted_iota(jnp.int32, (2 * TQ_C, win), 0)
    col = lax.broadcasted_iota(jnp.int32, (2 * TQ_C, win), 1)
    qpos = qi * TQ_C + jnp.where(row >= TQ_C, row - TQ_C, row)
    ok = jnp.abs(qpos - (start + col)) <= WINDOW
    row1 = lax.broadcasted_iota(jnp.int32, (2 * TQ_C, 1), 0)
    outs = []
    for kvh, msk in enumerate((m_lo, m_hi)):
        qs = jnp.concatenate([q_ref[:, c * LANES:(c + 1) * LANES] * msk for c in range(2)], axis=0)
        s_loc = jnp.where(ok, _dot_nt(qs, k_win), NEG)
        s_ctx = _dot_nt(qs, k_ctx)
        sink_col = jnp.where(row1 < TQ_C, sink_ref[2 * kvh] * LOG2E, sink_ref[2 * kvh + 1] * LOG2E)
        m = jnp.maximum(jnp.maximum(jnp.max(s_loc, axis=-1, keepdims=True),
                                    jnp.max(s_ctx, axis=-1, keepdims=True)), sink_col)
        acc = (_dot(jnp.exp2(s_loc - m).astype(BF16), v_win)
               + _dot(jnp.exp2(s_ctx - m).astype(BF16), v_ctx))
        outs.append(acc[:, :LANES] / (acc[:, LANES:] + jnp.exp2(sink_col - m)))
    for c in range(2):
        o_c = jnp.where(lo, outs[0][c * TQ_C:(c + 1) * TQ_C], outs[1][c * TQ_C:(c + 1) * TQ_C])
        o_ref[:, c * LANES:(c + 1) * LANES] = o_c.astype(BF16)


def _dec_c_call(sink_l, q, k_ext, v_ext):
    nb, n_keys, _ = k_ext.shape
    seq = q.shape[0] // nb
    nq = seq // TQ_C
    return pl.pallas_call(
        functools.partial(_dec_c_kernel, seq=seq),
        grid_spec=pltpu.PrefetchScalarGridSpec(
            num_scalar_prefetch=1,
            grid=(nb, nq),
            in_specs=[pl.BlockSpec((TQ_C, 256), lambda b, i, s: (b * nq + i, 3)),
                      pl.BlockSpec((None, n_keys, LANES), lambda b, i, s: (b, 0, 0)),
                      pl.BlockSpec((None, n_keys, 2 * LANES), lambda b, i, s: (b, 0, 0))],
            out_specs=pl.BlockSpec((TQ_C, 256), lambda b, i, s: (b * nq + i, 0)),
        ),
        out_shape=jax.ShapeDtypeStruct((q.shape[0], 256), BF16),
        compiler_params=_cparams(("parallel", "parallel")),
        name="dec_window_attention",
    )(sink_l, q, k_ext, v_ext)


N_DR = 2 * NA_WIN_R - 2


def _bias_kernel(rpb_ref, o_ref):
    h = pl.program_id(0)
    d = pl.program_id(1)
    n_dc = 2 * NA_WIN_C - 1
    qc = lax.broadcasted_iota(jnp.int32, (GRID_W, LANES), 0)
    kk = lax.broadcasted_iota(jnp.int32, (GRID_W, LANES), 1)
    half = kk >= GRID_W
    kc = jnp.where(half, kk - GRID_W, kk)
    delta = jnp.clip(kc - qc, -(NA_WIN_C - 1), NA_WIN_C - 1) + (NA_WIN_C - 1)
    col_start = jnp.clip(qc - NA_WIN_C // 2, 0, GRID_W - NA_WIN_C)
    ok = (kc >= col_start) & (kc < col_start + NA_WIN_C)
    acc = jnp.zeros((GRID_W, LANES), F32)
    base = (h * (2 * NA_WIN_R - 1) + d) * n_dc
    for dc in range(n_dc):
        val = jnp.where(half, rpb_ref[base + n_dc + dc] * LOG2E, rpb_ref[base + dc] * LOG2E)
        acc = jnp.where(delta == dc, val, acc)
    o_ref[...] = jnp.where(ok, acc, NEG)


def _bias_call(rpb_l):
    return pl.pallas_call(
        _bias_kernel,
        grid_spec=pltpu.PrefetchScalarGridSpec(
            num_scalar_prefetch=1,
            grid=(HA, N_DR),
            in_specs=[],
            out_specs=pl.BlockSpec((None, None, GRID_W, LANES), lambda h, d, r: (h, d, 0, 0)),
        ),
        out_shape=jax.ShapeDtypeStruct((HA, N_DR, GRID_W, LANES), F32),
        compiler_params=_cparams(("arbitrary", "arbitrary")),
        name="na_bias_table",
    )(rpb_l.reshape(-1))


def _dec_a_kernel(q_ref, k_ref, v_ref, bias_ref, o_ref, *, seq):
    rb = pl.program_id(2)
    rows = seq // GRID_W
    n_ctx = k_ref.shape[0] - seq
    n_loc = NA_WIN_R * GRID_W
    _, m_lo, m_hi = _half_masks(BF16)
    lo = lax.broadcasted_iota(jnp.int32, (1, LANES), 1) < HEAD_DIM
    k_ctx = k_ref[seq:seq + n_ctx, :]
    v_ctx = v_ref[seq:seq + n_ctx, :]
    for rr in range(ROWS_A):
        r = rb * ROWS_A + rr
        start = jnp.clip(r - NA_WIN_R // 2, 0, rows - NA_WIN_R)
        d0 = start - r + (NA_WIN_R - 1)
        off = pl.multiple_of(start * GRID_W, GRID_W)
        k_win = k_ref[pl.ds(off, n_loc), :]
        v_win = v_ref[pl.ds(off, n_loc), :]
        q_row = q_ref[rr * GRID_W:(rr + 1) * GRID_W, :]
        qs = jnp.concatenate([q_row * m_lo, q_row * m_hi], axis=0)
        bias = jnp.concatenate(
            [jnp.concatenate([bias_ref[hh, d0 + 2 * i] for i in range(NA_WIN_R // 2)], axis=1) for hh in range(2)],
            axis=0)
        s_loc = _dot_nt(qs, k_win) + bias
        s_ctx = _dot_nt(qs, k_ctx)
        m = jnp.maximum(jnp.max(s_loc, axis=-1, keepdims=True), jnp.max(s_ctx, axis=-1, keepdims=True))
        p_loc = jnp.exp2(s_loc - m)
        p_ctx = jnp.exp2(s_ctx - m)
        l = jnp.sum(p_loc, axis=-1, keepdims=True) + jnp.sum(p_ctx, axis=-1, keepdims=True)
        o = (_dot(p_loc.astype(BF16), v_win) + _dot(p_ctx.astype(BF16), v_ctx)) / l
        o_c = jnp.where(lo, o[:GRID_W], o[GRID_W:])
        o_ref[rr * GRID_W:(rr + 1) * GRID_W, :] = o_c.astype(BF16)


def _dec_a_call(q, k_ext, v_ext, bias_tab):
    nb, n_keys, _ = k_ext.shape
    seq = q.shape[0] // nb
    tq = ROWS_A * GRID_W
    nq = seq // tq
    return pl.pallas_call(
        functools.partial(_dec_a_kernel, seq=seq),
        grid=(2, nb, nq),
        in_specs=[pl.BlockSpec((tq, LANES), lambda p, b, i: (b * nq + i, 4 + p)),
                  pl.BlockSpec((None, n_keys, LANES), lambda p, b, i: (b, 0, p)),
                  pl.BlockSpec((None, n_keys, LANES), lambda p, b, i: (b, 0, p)),
                  pl.BlockSpec((2, N_DR, GRID_W, LANES), lambda p, b, i: (p, 0, 0, 0))],
        out_specs=pl.BlockSpec((tq, LANES), lambda p, b, i: (b * nq + i, p)),
        out_shape=jax.ShapeDtypeStruct((q.shape[0], 2 * LANES), BF16),
        compiler_params=_cparams(("parallel", "parallel", "parallel")),
        name="dec_neighbourhood_attention",
    )(q, k_ext, v_ext, bias_tab)


def _merge_kernel(x_ref, oc_ref, ob_ref, oa_ref, ow_ref, mod_ref, nm_ref, nf_ref, wg_ref, wbb_ref, wba_ref, wbc_ref,
                  wo_ref, wr_hi_ref, wr_lo_ref, xo_ref, h_ref, lg_ref, *, nt_c):
    d = x_ref.shape[1]
    is_ctx = pl.program_id(0) < nt_c
    o_b = jnp.where(is_ctx, oc_ref[:, :512], ob_ref[...])
    o_a = jnp.where(is_ctx, oc_ref[:, 512:768], oa_ref[...])
    o_c = jnp.where(is_ctx, oc_ref[:, 768:], ow_ref[...])
    x = x_ref[...]
    u = _modulated_norm(x, nm_ref[...], mod_ref[0, 1:2, :], mod_ref[0, 0:1, :])
    gates = _dot(u.astype(BF16), wg_ref[...])
    gates = 1.0 / (1.0 + jnp.exp(-gates))
    m = (gates[:, :d] * _dot(o_a, wba_ref[...])
         + gates[:, d:2 * d] * _dot(o_b, wbb_ref[...])
         + gates[:, 2 * d:] * _dot(o_c, wbc_ref[...]))
    x_new = x + mod_ref[0, 2:3, :] * _dot(m.astype(BF16), wo_ref[...])
    xo_ref[...] = x_new
    h = _modulated_norm(x_new, nf_ref[...], mod_ref[0, 4:5, :], mod_ref[0, 3:4, :])
    h_hi, h_lo = _split_bf16(h)
    h_ref[...] = _pack_bf16_pairs(h_hi)
    w_hi = wr_hi_ref[...]
    lg_ref[...] = _dot(h_hi, w_hi) + _dot(h_lo, w_hi) + _dot(h_hi, wr_lo_ref[...])


def _merge_call(x_all, o_ctx, o_b, o_a, o_c, mods, nm, nf, wg, wbb, wba, wbc, wo, wr_hi, wr_lo, mod_map):
    t, d = x_all.shape
    nt_c = o_ctx.shape[0] // TM
    const = lambda i: (0, 0)
    full = lambda a: pl.BlockSpec(a.shape, const)
    ctx_map = lambda i: (jnp.minimum(i, nt_c - 1), 0)
    dec_map = lambda i: (jnp.maximum(i - nt_c, 0), 0)
    return pl.pallas_call(
        functools.partial(_merge_kernel, nt_c=nt_c),
        grid=(t // TM,),
        in_specs=[pl.BlockSpec((TM, d), lambda i: (i, 0)),
                  pl.BlockSpec((TM, 1024), ctx_map),
                  pl.BlockSpec((TM, 512), dec_map),
                  pl.BlockSpec((TM, 256), dec_map),
                  pl.BlockSpec((TM, 256), dec_map),
                  pl.BlockSpec((1, 6, d), mod_map),
                  full(nm), full(nf), full(wg), full(wbb), full(wba), full(wbc), full(wo),
                  full(wr_hi), full(wr_lo)],
        out_specs=[pl.BlockSpec((TM, d), lambda i: (i, 0)),
                   pl.BlockSpec((TM, d // 2), lambda i: (i, 0)),
                   pl.BlockSpec((TM, LANES), lambda i: (i, 0))],
        out_shape=[jax.ShapeDtypeStruct((t, d), F32),
                   jax.ShapeDtypeStruct((t, d // 2), jnp.uint32),
                   jax.ShapeDtypeStruct((t, LANES), F32)],
        compiler_params=_cparams(("parallel",)),
        name="merge_branches",
    )(x_all, o_ctx, o_b, o_a, o_c, mods, nm, nf, wg, wbb, wba, wbc, wo, wr_hi, wr_lo)


def _route_kernel(lg_ref, ri_ref, rg_ref, tab_ref, blk_ref, cnt_sc, *, nb_max):
    i = pl.program_id(0)
    n = pl.num_programs(0)
    tm = lg_ref.shape[0]

    @pl.when(i == 0)
    def _():
        cnt_sc[...] = jnp.zeros(cnt_sc.shape, F32)

    lg = lg_ref[...]
    lane = lax.broadcasted_iota(jnp.int32, (tm, LANES), 1).astype(F32)
    big = float(LANES)
    neg_inf = -jnp.inf
    gl = jnp.where(lane < N_GROUPS, lg, neg_inf)
    gmax = jnp.max(gl, axis=-1, keepdims=True)
    g_sel = jnp.min(jnp.where(gl == gmax, lane, big), axis=-1, keepdims=True)
    g_w = 1.0 / jnp.sum(jnp.exp(gl - gmax), axis=-1, keepdims=True)
    e_lo = N_GROUPS + EXPERTS_PER_GROUP * g_sel
    el = jnp.where((lane >= e_lo) & (lane < e_lo + EXPERTS_PER_GROUP), lg, neg_inf)
    m1 = jnp.max(el, axis=-1, keepdims=True)
    i1 = jnp.min(jnp.where(el == m1, lane, big), axis=-1, keepdims=True)
    el2 = jnp.where(lane == i1, neg_inf, el)
    m2 = jnp.max(el2, axis=-1, keepdims=True)
    i2 = jnp.min(jnp.where(el2 == m2, lane, big), axis=-1, keepdims=True)
    e2x = jnp.exp(m2 - m1)
    den = 1.0 + e2x
    gate1 = g_w * (1.0 / den)
    gate2 = g_w * (e2x / den)
    hit1 = lane == i1
    hit2 = lane == i2
    onehot = jnp.where(hit1 | hit2, 1.0, 0.0).astype(BF16)
    rr = lax.broadcasted_iota(jnp.int32, (tm, tm), 0)
    cc = lax.broadcasted_iota(jnp.int32, (tm, tm), 1)
    tri = jnp.where(cc <= rr, 1.0, 0.0).astype(BF16)
    pref = _dot(tri, onehot) + cnt_sc[...]
    rank1 = jnp.sum(jnp.where(hit1, pref, 0.0), axis=-1, keepdims=True) - 1.0
    rank2 = jnp.sum(jnp.where(hit2, pref, 0.0), axis=-1, keepdims=True) - 1.0
    cnt_sc[...] = pref[tm - 1:tm, :]
    e1 = (i1 - N_GROUPS).astype(jnp.int32)
    e2 = (i2 - N_GROUPS).astype(jnp.int32)
    pk1 = (e1 << 16) | rank1.astype(jnp.int32)
    pk2 = (e2 << 16) | rank2.astype(jnp.int32)
    lane_i = lax.broadcasted_iota(jnp.int32, (tm, LANES), 1)
    ri_ref[...] = jnp.where(lane_i == 0, pk1, jnp.where(lane_i == 1, pk2, 0))
    rg_ref[...] = jnp.where(lane_i == 0, gate1, jnp.where(lane_i == 1, gate2, 0.0))

    @pl.when(i == n - 1)
    def _():
        cnt = cnt_sc[...]
        nblk = jnp.floor((cnt + (BM - 1.0)) * (1.0 / BM))
        l8 = lax.broadcasted_iota(jnp.int32, (8, LANES), 1)
        is_exp = (l8 >= N_GROUPS) & (l8 < N_GROUPS + N_EXPERTS)
        nblk8 = jnp.where(is_exp, jnp.broadcast_to(nblk, (8, LANES)), 0.0)
        r2 = lax.broadcasted_iota(jnp.int32, (LANES, LANES), 0)
        c2 = lax.broadcasted_iota(jnp.int32, (LANES, LANES), 1)
        upper = jnp.where(r2 <= c2, 1.0, 0.0).astype(BF16)
        hi, lo_ = _split_bf16(nblk8)
        blk_end = _dot(hi, upper) + _dot(lo_, upper)
        blk_start = blk_end - nblk8
        row8 = lax.broadcasted_iota(jnp.int32, (8, LANES), 0)
        tab = jnp.where(row8 == 0, blk_start * BM, jnp.where(row8 == 1, blk_end, cnt))
        tab_ref[...] = tab.astype(jnp.int32)
        bi = lax.broadcasted_iota(jnp.int32, (nb_max, LANES), 0).astype(F32)
        ends = blk_end[0:1, :]
        lb = lax.broadcasted_iota(jnp.int32, (nb_max, LANES), 1)
        done = jnp.where((lb >= N_GROUPS) & (lb < N_GROUPS + N_EXPERTS) & (ends <= bi), 1.0, 0.0)
        be = jnp.minimum(jnp.sum(done, axis=-1, keepdims=True), N_EXPERTS - 1.0)
        blk_ref[...] = jnp.broadcast_to(be, (nb_max, LANES)).astype(jnp.int32)


def _route_call(logits, nb_max):
    t = logits.shape[0]
    return pl.pallas_call(
        functools.partial(_route_kernel, nb_max=nb_max),
        grid=(t // TM,),
        in_specs=[pl.BlockSpec((TM, LANES), lambda i: (i, 0))],
        out_specs=[pl.BlockSpec((TM, LANES), lambda i: (i, 0)),
                   pl.BlockSpec((TM, LANES), lambda i: (i, 0)),
                   pl.BlockSpec((8, LANES), lambda i: (0, 0)),
                   pl.BlockSpec((nb_max, LANES), lambda i: (0, 0))],
        out_shape=[jax.ShapeDtypeStruct((t, LANES), jnp.int32),
                   jax.ShapeDtypeStruct((t, LANES), F32),
                   jax.ShapeDtypeStruct((8, LANES), jnp.int32),
                   jax.ShapeDtypeStruct((nb_max, LANES), jnp.int32)],
        scratch_shapes=[pltpu.VMEM((1, LANES), F32)],
        compiler_params=_cparams(("arbitrary",)),
        name="moe_route",
    )(logits)


def _slot(packed_ref, start_ref, a):
    pk = packed_ref[a]
    return start_ref[N_GROUPS + (pk >> 16)] + (pk & 0xFFFF)


N_DISPATCH_BUF = 3


def _dispatch_kernel(packed_ref, start_ref, xs_in_ref, h_ref, xs_ref, buf, fsem, ssem):
    del xs_in_ref
    i = pl.program_id(0)
    n = pl.num_programs(0)
    tm = buf.shape[1]

    def fetch(tile, slot):
        return pltpu.make_async_copy(h_ref.at[pl.ds(tile * tm, tm), :], buf.at[slot], fsem.at[slot])

    def scatter_wait(slot):
        for _ in range(2):
            pltpu.make_async_copy(buf.at[slot], xs_ref.at[pl.ds(0, tm), :], ssem.at[slot]).wait()

    @pl.when(i == 0)
    def _():
        fetch(0, 0).start()

        @pl.when(n > 1)
        def _():
            fetch(1, 1).start()

    slot = i % N_DISPATCH_BUF
    fetch(i, slot).wait()
    base = 2 * i * tm
    for r in range(tm):
        for k in range(2):
            dst = _slot(packed_ref, start_ref, base + (2 * r + k))
            pltpu.make_async_copy(buf.at[slot, pl.ds(r, 1), :], xs_ref.at[pl.ds(dst, 1), :],
                                  ssem.at[slot]).start(priority=k)

    @pl.when(i >= 1)
    def _():
        scatter_wait((i - 1) % N_DISPATCH_BUF)

    @pl.when(i + 2 < n)
    def _():
        fetch(i + 2, (i + 2) % N_DISPATCH_BUF).start()

    @pl.when(i == n - 1)
    def _():
        scatter_wait(slot)


def _dispatch_call(packed, starts, xs_init, h):
    t, dw = h.shape
    return pl.pallas_call(
        _dispatch_kernel,
        grid_spec=pltpu.PrefetchScalarGridSpec(
            num_scalar_prefetch=2,
            grid=(t // TM,),
            in_specs=[pl.BlockSpec(memory_space=pl.ANY),
                      pl.BlockSpec(memory_space=pl.ANY)],
            out_specs=pl.BlockSpec(memory_space=pl.ANY),
            scratch_shapes=[pltpu.VMEM((N_DISPATCH_BUF, TM, dw), h.dtype),
                            pltpu.SemaphoreType.DMA((N_DISPATCH_BUF,)),
                            pltpu.SemaphoreType.DMA((N_DISPATCH_BUF,))],
        ),
        out_shape=jax.ShapeDtypeStruct(xs_init.shape, xs_init.dtype),
        input_output_aliases={2: 0},
        compiler_params=_cparams(("arbitrary",)),
        name="moe_dispatch",
    )(packed, starts, xs_init, h)


def _expert_kernel(blk_ref, nused_ref, xs_ref, wg_ref, wu_ref, wd_ref, y_ref, wg_sc, wu_sc, wd_sc):
    i = pl.program_id(0)
    prev = blk_ref[jnp.maximum(i - 1, 0)]
    changed = (i == 0) | (blk_ref[i] != prev)

    @pl.when(changed)
    def _():
        wg_sc[...] = wg_ref[...].astype(BF16)
        wu_sc[...] = wu_ref[...].astype(BF16)
        wd_sc[...] = wd_ref[...].astype(BF16)

    @pl.when(i < nused_ref[0])
    def _():
        x = _unpack_bf16_pairs(xs_ref[...])
        g = _dot(x, wg_sc[...])
        u = _dot(x, wu_sc[...])
        hdn = (g * (1.0 / (1.0 + jnp.exp(-g)))) * u
        y_ref[...] = _dot(hdn.astype(BF16), wd_sc[...])

    @pl.when(i >= nused_ref[0])
    def _():
        y_ref[...] = jnp.zeros(y_ref.shape, F32)


def _expert_call(blk_expert, n_used, xs, wg, wu, wd, layer):
    p, dw = xs.shape
    d, ff = wg.shape[-2:]
    return pl.pallas_call(
        _expert_kernel,
        grid_spec=pltpu.PrefetchScalarGridSpec(
            num_scalar_prefetch=2,
            grid=(p // BM,),
            in_specs=[pl.BlockSpec((BM, dw), lambda i, b, n: (i, 0)),
                      pl.BlockSpec((None, None, d, ff), lambda i, b, n: (layer, b[i], 0, 0)),
                      pl.BlockSpec((None, None, d, ff), lambda i, b, n: (layer, b[i], 0, 0)),
                      pl.BlockSpec((None, None, ff, d), lambda i, b, n: (layer, b[i], 0, 0))],
            out_specs=pl.BlockSpec((BM, d), lambda i, b, n: (i, 0)),
            scratch_shapes=[pltpu.VMEM((d, ff), BF16), pltpu.VMEM((d, ff), BF16), pltpu.VMEM((ff, d), BF16)],
        ),
        out_shape=jax.ShapeDtypeStruct((p, d), F32),
        compiler_params=_cparams(("arbitrary",)),
        name="moe_experts",
    )(blk_expert, n_used, xs, wg, wu, wd)


def _combine_kernel(packed_ref, start_ref, x_ref, rg_ref, mod_ref, y_ref, o_ref, buf, sem):
    j = pl.program_id(0)
    n_tiles = pl.num_programs(0) - 1
    tm = x_ref.shape[0]

    @pl.when(j < n_tiles)
    def _():
        slot = j % 2
        base = 2 * j * tm
        for r in range(tm):
            for k in range(2):
                src = _slot(packed_ref, start_ref, base + (2 * r + k))
                pltpu.make_async_copy(y_ref.at[pl.ds(src, 1), :], buf.at[slot, k, pl.ds(r, 1), :],
                                      sem.at[slot]).start(priority=k)

    @pl.when(j >= 1)
    def _():
        slot = (j - 1) % 2
        for k in range(2):
            pltpu.make_async_copy(y_ref.at[pl.ds(0, tm), :], buf.at[slot, k], sem.at[slot]).wait()
        rg = rg_ref[...]
        y = buf[slot, 0] * rg[:, 0:1] + buf[slot, 1] * rg[:, 1:2]
        o_ref[...] = x_ref[...] + mod_ref[0, 5:6, :] * y


def _combine_call(packed, starts, x_new, route_g, mods, yb, mod_map):
    t, d = x_new.shape
    prev = lambda j: jnp.maximum(j - 1, 0)
    return pl.pallas_call(
        _combine_kernel,
        grid_spec=pltpu.PrefetchScalarGridSpec(
            num_scalar_prefetch=2,
            grid=(t // TM + 1,),
            in_specs=[pl.BlockSpec((TM, d), lambda j, p, s: (prev(j), 0)),
                      pl.BlockSpec((TM, LANES), lambda j, p, s: (prev(j), 0)),
                      pl.BlockSpec((1, 6, d), lambda j, p, s: mod_map(prev(j))),
                      pl.BlockSpec(memory_space=pl.ANY)],
            out_specs=pl.BlockSpec((TM, d), lambda j, p, s: (prev(j), 0)),
            scratch_shapes=[pltpu.VMEM((2, 2, TM, d), F32), pltpu.SemaphoreType.DMA((2,))],
        ),
        out_shape=jax.ShapeDtypeStruct((t, d), F32),
        compiler_params=_cparams(("arbitrary",)),
        name="moe_combine",
    )(packed, starts, x_new, route_g, mods, yb)


def _head_cols(base, heads):
    return np.concatenate([base + h * HEAD_DIM + np.arange(HEAD_DIM) for h in heads])


def _layouts():
    wa, wb, wc = HA * HEAD_DIM, HB * HEAD_DIM, HC * HEAD_DIM
    qa0, ka0, va0 = 0, wa, 2 * wa
    qb0 = 3 * wa
    kb0 = qb0 + wb
    vb0 = kb0 + KVB * HEAD_DIM
    qc0 = vb0 + KVB * HEAD_DIM
    kc0 = qc0 + wc
    vc0 = kc0 + KVC * HEAD_DIM
    b_heads = [h for c in range(4) for h in (c, c + 4)]
    c_heads = [0, 2, 1, 3]
    q_cols = np.concatenate([_head_cols(qb0, b_heads), _head_cols(qa0, range(HA)), _head_cols(qc0, c_heads)])
    k_cols = np.concatenate([ka0 + np.arange(wa), kb0 + np.arange(KVB * HEAD_DIM), kc0 + np.arange(KVC * HEAD_DIM)])
    v_cols = np.concatenate([va0 + np.arange(wa), vb0 + np.arange(KVB * HEAD_DIM), vc0 + np.arange(KVC * HEAD_DIM)])
    in_cols = np.concatenate([q_cols, k_cols, v_cols])
    gain_idx = np.concatenate([np.full(HB, 2), np.full(HA, 0), np.full(HC, 4),
                               np.full(HA, 1), np.full(KVB, 3), np.full(KVC, 5)])
    gain_scale = np.concatenate([np.full(HB + HA + HC, HEAD_DIM ** -0.5 * LOG2E), np.ones(HA + KVB + KVC)])
    return in_cols, gain_idx, gain_scale.astype(np.float32), _head_cols(0, b_heads), _head_cols(0, c_heads)


def _rope_tables(seq):
    quarter = HEAD_DIM // 4
    t = jnp.arange(seq)
    rows, cols = t // GRID_W, t % GRID_W
    freqs = ROPE_THETA ** (-jnp.arange(quarter, dtype=F32) / quarter)
    ang_r = rows.astype(F32)[:, None] * freqs[None, :]
    ang_c = cols.astype(F32)[:, None] * freqs[None, :]
    cos = jnp.concatenate([jnp.cos(ang_r), jnp.cos(ang_r), jnp.cos(ang_c), jnp.cos(ang_c)], axis=1)
    sin = jnp.concatenate([-jnp.sin(ang_r), jnp.sin(ang_r), -jnp.sin(ang_c), jnp.sin(ang_c)], axis=1)
    return jnp.tile(cos, (1, 2)), jnp.tile(sin, (1, 2))


def kernel(x_prompt, x_sample, cache_ctx_k, cache_ctx_v, c, c_ctx, norm_mix, norm_ffn, w_ada, b_ada, w_in, qk_gain,
           rpb, sink, w_br_a, w_br_b, w_br_c, w_gate, w_out, w_route_group, w_route_expert, w_exp_gate, w_exp_up,
           w_exp_down):
    batch, seq_c, d = x_prompt.shape
    nb, seq_d, _ = x_sample.shape
    depth = w_in.shape[0]
    t_c, t_d = batch * seq_c, nb * seq_d
    t = t_c + t_d
    n_ctx = cache_ctx_k.shape[2]
    assert t_c % TM == 0 and seq_d % TM == 0 and nb + 1 <= 8 and 2 * t < (1 << 16)
    nt_c = t_c // TM
    tiles_per_seq = seq_d // TM
    nb_max = 2 * t // BM + N_EXPERTS
    nb_max = (nb_max + 7) // 8 * 8

    in_cols, gain_idx, gain_scale, ob_rows, oc_rows = _layouts()
    x_all = jnp.concatenate([x_prompt.reshape(t_c, d), x_sample.reshape(t_d, d)], axis=0)
    cvec = jnp.concatenate([c_ctx[None, :], c, jnp.zeros((8 - 1 - nb, d), F32)], axis=0)
    mods = _ada_call(cvec, w_ada, b_ada).reshape(depth, 8, 6, d)
    rope_tabs = _rope_tables(seq_d)

    def mod_map(i):
        return (jnp.where(i < nt_c, 0, 1 + (i - nt_c) // tiles_per_seq), 0, 0)

    new_k, new_v = [], []
    for l in range(depth):
        w_in_p = w_in[l][:, in_cols].astype(BF16)
        gain_row = (qk_gain[l][gain_idx] * gain_scale[:, None]).reshape(1, -1)
        gain_row = jnp.concatenate([gain_row, jnp.ones((1, 512), F32)], axis=1)
        nm = norm_mix[l].reshape(1, d)
        nf = norm_ffn[l].reshape(1, d)

        q_c, k_c, v_c, k32, v32 = _qkv_call(x_all, mods[l], nm, w_in_p, gain_row, tile_off=0, n_tok=t_c,
                                            seq_len=seq_c, row_off=0, per_seq_mod=False, rope_tabs=None, emit_f32=True)
        new_k.append(k32.reshape(batch, seq_c, 8, HEAD_DIM))
        new_v.append(v32.reshape(batch, seq_c, 8, HEAD_DIM))
        q_d, k_d, v_d = _qkv_call(x_all, mods[l], nm, w_in_p, gain_row, tile_off=nt_c, n_tok=t_d,
                                  seq_len=seq_d, row_off=1, per_seq_mod=True, rope_tabs=rope_tabs, emit_f32=False)

        o_ctx = _ctx_attn_call(sink[l], q_c, k_c, v_c, seq_c)
        ck = cache_ctx_k[:, l].reshape(nb, n_ctx, 512).astype(BF16)
        cv = cache_ctx_v[:, l].reshape(nb, n_ctx, 512).astype(BF16)
        k_ext = jnp.concatenate([k_d.reshape(nb, seq_d, 512), ck], axis=1)
        v_ext = jnp.concatenate([v_d.reshape(nb, seq_d, 512), cv], axis=1)
        ones = jnp.ones((nb, seq_d + n_ctx, LANES), BF16)
        o_b = _dec_b_call(q_d, k_ext[:, :, 256:384], jnp.concatenate([v_ext[:, :, 256:384], ones], axis=2))
        o_c = _dec_c_call(sink[l], q_d, k_ext[:, :, 384:512], jnp.concatenate([v_ext[:, :, 384:512], ones], axis=2))
        o_a = _dec_a_call(q_d, k_ext[:, :, :256], v_ext[:, :, :256], _bias_call(rpb[l]))

        w_route = jnp.concatenate([w_route_group[l], w_route_expert[l],
                                   jnp.zeros((d, LANES - N_GROUPS - N_EXPERTS), F32)], axis=1)
        wr_hi = w_route.astype(BF16)
        wr_lo = (w_route - wr_hi.astype(F32)).astype(BF16)
        x_new, h, logits = _merge_call(
            x_all, o_ctx, o_b, o_a, o_c, mods[l], nm, nf, w_gate[l].astype(BF16), w_br_b[l][ob_rows].astype(BF16),
            w_br_a[l].astype(BF16), w_br_c[l][oc_rows].astype(BF16), w_out[l].astype(BF16), wr_hi, wr_lo, mod_map)

        route_i, route_g, tab, blk = _route_call(logits, nb_max)
        packed = route_i[:, :2].reshape(-1)
        starts = tab[0]
        xs = _dispatch_call(packed, starts, jnp.zeros((nb_max * BM, d // 2), jnp.uint32), h)
        yb = _expert_call(blk[:, 0], tab[1, N_GROUPS + N_EXPERTS - 1:N_GROUPS + N_EXPERTS], xs,
                          w_exp_gate, w_exp_up, w_exp_down, l)
        x_all = _combine_call(packed, starts, x_new, route_g, mods[l], yb, mod_map)

    y_prompt = x_all[:t_c].reshape(batch, seq_c, d)
    y_sample = x_all[t_c:].reshape(nb, seq_d, d)
    return (y_prompt, y_sample, jnp.stack(new_k, axis=1), jnp.stack(new_v, axis=1))
```

```python
import functools

import jax
import jax.numpy as jnp
import numpy as np
from jax import lax
from jax.experimental import pallas as pl
from jax.experimental.pallas import tpu as pltpu

GRID_W = 64
HEAD_DIM = 64
HA, HB, KVB, HC, KVC = 4, 8, 2, 4, 2
NA_WIN_R, NA_WIN_C = 8, 16
WINDOW = 128
ROPE_THETA = 10000.0
N_GROUPS, EXPERTS_PER_GROUP, N_EXPERTS = 4, 8, 32
EPS = 1e-6
NEG = -1e30
LOG2E = 1.4426950408889634

LANES = 128
V7X_VMEM_LIMIT = 56 * 1024 * 1024

TM = 512
TQ_B = 128
TQ_C = 256
ROWS_A = 4
BM = 512

F32 = jnp.float32
BF16 = jnp.bfloat16


def _cparams(sem, vmem=V7X_VMEM_LIMIT):
    return pltpu.CompilerParams(dimension_semantics=sem, vmem_limit_bytes=vmem)


def _split_bf16(x):
    hi = x.astype(BF16)
    lo = (x - hi.astype(F32)).astype(BF16)
    return hi, lo


def _pack_bf16_pairs(x_bf16):
    n = x_bf16.shape[1] // 2
    lo = lax.bitcast_convert_type(x_bf16[:, :n].astype(F32), jnp.uint32)
    hi = lax.bitcast_convert_type(x_bf16[:, n:].astype(F32), jnp.uint32)
    return (lo >> 16) | hi


def _unpack_bf16_pairs(w_u32):
    lo = lax.bitcast_convert_type(w_u32 << 16, F32)
    hi = lax.bitcast_convert_type(w_u32 & jnp.uint32(0xFFFF0000), F32)
    return jnp.concatenate([lo, hi], axis=1).astype(BF16)


def _dot(a, b):
    return jnp.dot(a, b, preferred_element_type=F32)


def _dot_nt(a, b):
    return lax.dot_general(a, b, (((1,), (1,)), ((), ())), preferred_element_type=F32)


def _half_masks(dtype):
    lane = lax.broadcasted_iota(jnp.int32, (1, LANES), 1)
    lo = (lane < HEAD_DIM)
    return lo, jnp.where(lo, 1.0, 0.0).astype(dtype), jnp.where(lo, 0.0, 1.0).astype(dtype)


def _ada_kernel(c_ref, w_ref, b_ref, o_ref):
    c = c_ref[...]
    s = c * (1.0 / (1.0 + jnp.exp(-c)))
    s_hi, s_lo = _split_bf16(s)
    w_hi, w_lo = _split_bf16(w_ref[...])
    o_ref[...] = _dot(s_hi, w_hi) + _dot(s_lo, w_hi) + _dot(s_hi, w_lo) + b_ref[...]


def _ada_call(cvec, w_ada, b_ada):
    depth, d, n6 = w_ada.shape
    tn = 512
    return pl.pallas_call(
        _ada_kernel,
        grid=(depth, n6 // tn),
        in_specs=[
            pl.BlockSpec((8, d), lambda l, j: (0, 0)),
            pl.BlockSpec((None, d, tn), lambda l, j: (l, 0, j)),
            pl.BlockSpec((None, 1, tn), lambda l, j: (l, 0, j)),
        ],
        out_specs=pl.BlockSpec((None, 8, tn), lambda l, j: (l, 0, j)),
        out_shape=jax.ShapeDtypeStruct((depth, 8, n6), F32),
        compiler_params=_cparams(("arbitrary", "arbitrary")),
        name="ada_modulation",
    )(cvec, w_ada, b_ada.reshape(depth, 1, n6))


N_QCHUNK = 8
N_KCHUNK = 4
ROPE_QCHUNKS = (0, 1, 2, 3, 6, 7)
ROPE_KCHUNKS = (2, 3)
V_EXT_COLS = 6 * LANES


def _modulated_norm(x, gain, scale, shift):
    ms = jnp.mean(x * x, axis=-1, keepdims=True)
    return (x * lax.rsqrt(ms + EPS) * gain) * (1.0 + scale) + shift


def _qkv_kernel(*refs, rope, emit_f32):
    x_ref, mod_ref, nm_ref, w_ref, gain_ref = refs[:5]
    pos = 5
    if rope:
        cos_ref, sin_ref = refs[pos:pos + 2]
        pos += 2
    q_ref, k_ref, v_ref = refs[pos:pos + 3]
    pos += 3
    if emit_f32:
        k32_ref, v32_ref = refs[pos:pos + 2]

    x = x_ref[...]
    u = _modulated_norm(x, nm_ref[...], mod_ref[0, 1:2, :], mod_ref[0, 0:1, :])
    y = _dot(u.astype(BF16), w_ref[...])

    lane = lax.broadcasted_iota(jnp.int32, (1, LANES), 1)
    lo = lane < HEAD_DIM
    first = (lane % 32) < 16
    if rope:
        cos_t = cos_ref[...]
        sin_t = sin_ref[...]

    def norm_chunk(j, use_rope):
        yj = y[:, j * LANES:(j + 1) * LANES]
        sq = yj * yj
        s_lo = jnp.sum(jnp.where(lo, sq, 0.0), axis=-1, keepdims=True)
        s_hi = jnp.sum(jnp.where(lo, 0.0, sq), axis=-1, keepdims=True)
        inv = lax.rsqrt(jnp.where(lo, s_lo, s_hi) * (1.0 / HEAD_DIM) + EPS)
        yj = (yj * inv) * gain_ref[:, j * LANES:(j + 1) * LANES]
        if use_rope:
            partner = jnp.where(first, pltpu.roll(yj, LANES - 16, axis=1), pltpu.roll(yj, 16, axis=1))
            yj = yj * cos_t + partner * sin_t
        return yj

    for j in range(N_QCHUNK):
        yj = norm_chunk(j, rope and j in ROPE_QCHUNKS)
        q_ref[:, j * LANES:(j + 1) * LANES] = yj.astype(BF16)
    for j in range(N_KCHUNK):
        yj = norm_chunk(N_QCHUNK + j, rope and j in ROPE_KCHUNKS)
        k_ref[:, j * LANES:(j + 1) * LANES] = yj.astype(BF16)
        if emit_f32:
            k32_ref[:, j * LANES:(j + 1) * LANES] = yj
    v = y[:, (N_QCHUNK + N_KCHUNK) * LANES:]
    if rope:
        ones = jnp.ones((v.shape[0], LANES), BF16)
        v_ref[:, :3 * LANES] = v[:, :3 * LANES].astype(BF16)
        v_ref[:, 3 * LANES:4 * LANES] = ones
        v_ref[:, 4 * LANES:5 * LANES] = v[:, 3 * LANES:].astype(BF16)
        v_ref[:, 5 * LANES:] = ones
    else:
        v_ref[...] = v.astype(BF16)
    if emit_f32:
        v32_ref[...] = v


def _qkv_call(x_all, mods, norm_mix_l, w_in_p, gain_row, *, tile_off, n_tok, seq_len, row_off,
              per_seq_mod, rope_tabs, emit_f32):
    d = x_all.shape[1]
    nt = n_tok // TM
    tiles_per_seq = max(seq_len // TM, 1)
    rope = rope_tabs is not None

    def mod_map(i):
        return (row_off + i // tiles_per_seq, 0, 0) if per_seq_mod else (row_off, 0, 0)

    in_specs = [
        pl.BlockSpec((TM, d), lambda i: (i + tile_off, 0)),
        pl.BlockSpec((1, 6, d), mod_map),
        pl.BlockSpec((1, d), lambda i: (0, 0)),
        pl.BlockSpec(w_in_p.shape, lambda i: (0, 0)),
        pl.BlockSpec((1, 2048), lambda i: (0, 0)),
    ]
    args = [x_all, mods, norm_mix_l, w_in_p, gain_row]
    if rope:
        in_specs += [pl.BlockSpec((TM, LANES), lambda i: (i % tiles_per_seq, 0))] * 2
        args += list(rope_tabs)
    v_cols = V_EXT_COLS if rope else 512
    out_shape = [jax.ShapeDtypeStruct((n_tok, 1024), BF16),
                 jax.ShapeDtypeStruct((n_tok, 512), BF16),
                 jax.ShapeDtypeStruct((n_tok, v_cols), BF16)]
    out_specs = [pl.BlockSpec((TM, 1024), lambda i: (i, 0)),
                 pl.BlockSpec((TM, 512), lambda i: (i, 0)),
                 pl.BlockSpec((TM, v_cols), lambda i: (i, 0))]
    if emit_f32:
        out_shape += [jax.ShapeDtypeStruct((n_tok, 512), F32)] * 2
        out_specs += [pl.BlockSpec((TM, 512), lambda i: (i, 0))] * 2
    return pl.pallas_call(
        functools.partial(_qkv_kernel, rope=rope, emit_f32=emit_f32),
        grid=(nt,),
        in_specs=in_specs,
        out_specs=out_specs,
        out_shape=out_shape,
        compiler_params=_cparams(("parallel",)),
        name="qkv_rope" if rope else "qkv_ctx",
    )(*args)


def _softmax_rows(s, sink_col=None):
    m = jnp.max(s, axis=-1, keepdims=True)
    if sink_col is not None:
        m = jnp.maximum(m, sink_col)
    p = jnp.exp2(s - m)
    l = jnp.sum(p, axis=-1, keepdims=True)
    if sink_col is not None:
        l = l + jnp.exp2(sink_col - m)
    return p, l


def _ctx_attn_kernel(sink_ref, q_ref, k_ref, v_ref, o_ref):
    s_len = q_ref.shape[0]
    _, m_lo, m_hi = _half_masks(BF16)
    lo = lax.broadcasted_iota(jnp.int32, (1, LANES), 1) < HEAD_DIM

    def group(q_chunks, kc, sinks=None):
        n = len(q_chunks)
        k = k_ref[:, kc * LANES:(kc + 1) * LANES]
        v = v_ref[:, kc * LANES:(kc + 1) * LANES]
        qs = jnp.concatenate([q_ref[:, c * LANES:(c + 1) * LANES] * m for m in (m_lo, m_hi) for c in q_chunks],
                             axis=0)
        s = _dot_nt(qs, k)
        sink_col = None
        if sinks is not None:
            row = lax.broadcasted_iota(jnp.int32, (2 * n * s_len, 1), 0)
            sink_col = jnp.zeros((2 * n * s_len, 1), F32)
            for idx, sv in enumerate(sinks):
                sink_col = jnp.where((row >= idx * s_len) & (row < (idx + 1) * s_len), sv, sink_col)
        p, l = _softmax_rows(s, sink_col)
        o = _dot(p.astype(BF16), v) / l
        for idx, c in enumerate(q_chunks):
            o_c = jnp.where(lo, o[idx * s_len:(idx + 1) * s_len], o[(n + idx) * s_len:(n + idx + 1) * s_len])
            o_ref[:, c * LANES:(c + 1) * LANES] = o_c.astype(BF16)

    group([0, 1, 2, 3], 2)
    group([4], 0)
    group([5], 1)
    group([6, 7], 3, sinks=[sink_ref[h] * LOG2E for h in range(HC)])


def _ctx_attn_call(sink_l, q, k, v, s_len):
    batch = q.shape[0] // s_len
    return pl.pallas_call(
        _ctx_attn_kernel,
        grid_spec=pltpu.PrefetchScalarGridSpec(
            num_scalar_prefetch=1,
            grid=(batch,),
            in_specs=[pl.BlockSpec((s_len, 1024), lambda b, s: (b, 0)),
                      pl.BlockSpec((s_len, 512), lambda b, s: (b, 0)),
                      pl.BlockSpec((s_len, 512), lambda b, s: (b, 0))],
            out_specs=pl.BlockSpec((s_len, 1024), lambda b, s: (b, 0)),
        ),
        out_shape=jax.ShapeDtypeStruct((q.shape[0], 1024), BF16),
        compiler_params=_cparams(("parallel",)),
        name="ctx_attention",
    )(sink_l, q, k, v)


def _dec_b_kernel(q_ref, k_ref, v_ref, o_ref):
    _, m_lo, m_hi = _half_masks(BF16)
    lo = lax.broadcasted_iota(jnp.int32, (1, LANES), 1) < HEAD_DIM
    k = k_ref[...]
    v = v_ref[...]
    outs = []
    for msk in (m_lo, m_hi):
        qs = jnp.concatenate([q_ref[:, c * LANES:(c + 1) * LANES] * msk for c in range(4)], axis=0)
        s = _dot_nt(qs, k)
        p = jnp.exp2(s - jnp.max(s, axis=-1, keepdims=True)).astype(BF16)
        acc = _dot(p, v)
        outs.append(acc[:, :LANES] / acc[:, LANES:])
    for c in range(4):
        o_c = jnp.where(lo, outs[0][c * TQ_B:(c + 1) * TQ_B], outs[1][c * TQ_B:(c + 1) * TQ_B])
        o_ref[:, c * LANES:(c + 1) * LANES] = o_c.astype(BF16)


def _dec_b_call(q, k_all, v_all):
    nb, n_keys, _ = k_all.shape
    seq = q.shape[0] // nb
    nq = seq // TQ_B
    return pl.pallas_call(
        _dec_b_kernel,
        grid=(nb, nq),
        in_specs=[pl.BlockSpec((TQ_B, 512), lambda b, i: (b * nq + i, 0)),
                  pl.BlockSpec((None, n_keys, LANES), lambda b, i: (b, 0, 0)),
                  pl.BlockSpec((None, n_keys, 2 * LANES), lambda b, i: (b, 0, 0))],
        out_specs=pl.BlockSpec((TQ_B, 512), lambda b, i: (b * nq + i, 0)),
        out_shape=jax.ShapeDtypeStruct((q.shape[0], 512), BF16),
        compiler_params=_cparams(("parallel", "parallel")),
        name="dec_dense_attention",
    )(q, k_all, v_all)


def _dec_c_kernel(sink_ref, q_ref, k_ref, v_ref, ck_ref, cv_ref, o_ref):
    qi = pl.program_id(1)
    seq = k_ref.shape[0]
    win = TQ_C + 2 * WINDOW
    _, m_lo, m_hi = _half_masks(BF16)
    lo = lax.broadcasted_iota(jnp.int32, (1, LANES), 1) < HEAD_DIM
    start = jnp.clip(qi * TQ_C - WINDOW, 0, seq - win)
    start = pl.multiple_of(start, WINDOW)
    k_win = k_ref[pl.ds(start, win), :]
    v_win = v_ref[pl.ds(start, win), :]
    k_ctx = ck_ref[...]
    v_ctx = cv_ref[...]
    row = lax.broadcasted_iota(jnp.int32, (2 * TQ_C, win), 0)
    col = lax.broadcasted_iota(jnp.int32, (2 * TQ_C, win), 1)
    qpos = qi * TQ_C + jnp.where(row >= TQ_C, row - TQ_C, row)
    ok = jnp.abs(qpos - (start + col)) <= WINDOW
    row1 = lax.broadcasted_iota(jnp.int32, (2 * TQ_C, 1), 0)
    outs = []
    for kvh, msk in enumerate((m_lo, m_hi)):
        qs = jnp.concatenate([q_ref[:, c * LANES:(c + 1) * LANES] * msk for c in range(2)], axis=0)
        s_loc = jnp.where(ok, _dot_nt(qs, k_win), NEG)
        s_ctx = _dot_nt(qs, k_ctx)
        sink_col = jnp.where(row1 < TQ_C, sink_ref[2 * kvh] * LOG2E, sink_ref[2 * kvh + 1] * LOG2E)
        m = jnp.maximum(jnp.maximum(jnp.max(s_loc, axis=-1, keepdims=True),
                                    jnp.max(s_ctx, axis=-1, keepdims=True)), sink_col)
        acc = (_dot(jnp.exp2(s_loc - m).astype(BF16), v_win)
               + _dot(jnp.exp2(s_ctx - m).astype(BF16), v_ctx))
        outs.append(acc[:, :LANES] / (acc[:, LANES:] + jnp.exp2(sink_col - m)))
    for c in range(2):
        o_c = jnp.where(lo, outs[0][c * TQ_C:(c + 1) * TQ_C], outs[1][c * TQ_C:(c + 1) * TQ_C])
        o_ref[:, c * LANES:(c + 1) * LANES] = o_c.astype(BF16)


def _dec_c_call(sink_l, q, k_d, v_d, ck, cv):
    nb, seq, _ = k_d.shape
    n_ctx = ck.shape[1]
    nq = seq // TQ_C
    return pl.pallas_call(
        _dec_c_kernel,
        grid_spec=pltpu.PrefetchScalarGridSpec(
            num_scalar_prefetch=1,
            grid=(nb, nq),
            in_specs=[pl.BlockSpec((TQ_C, 256), lambda b, i, s: (b * nq + i, 3)),
                      pl.BlockSpec((None, seq, LANES), lambda b, i, s: (b, 0, 3)),
                      pl.BlockSpec((None, seq, 2 * LANES), lambda b, i, s: (b, 0, 2)),
                      pl.BlockSpec((None, n_ctx, LANES), lambda b, i, s: (b, 0, 3)),
                      pl.BlockSpec((None, n_ctx, 2 * LANES), lambda b, i, s: (b, 0, 2))],
            out_specs=pl.BlockSpec((TQ_C, 256), lambda b, i, s: (b * nq + i, 0)),
        ),
        out_shape=jax.ShapeDtypeStruct((q.shape[0], 256), BF16),
        compiler_params=_cparams(("parallel", "parallel")),
        name="dec_window_attention",
    )(sink_l, q, k_d, v_d, ck, cv)


N_DR = 2 * NA_WIN_R - 2


def _bias_kernel(rpb_ref, o_ref):
    h = pl.program_id(0)
    d = pl.program_id(1)
    n_dc = 2 * NA_WIN_C - 1
    qc = lax.broadcasted_iota(jnp.int32, (GRID_W, LANES), 0)
    kk = lax.broadcasted_iota(jnp.int32, (GRID_W, LANES), 1)
    half = kk >= GRID_W
    kc = jnp.where(half, kk - GRID_W, kk)
    delta = jnp.clip(kc - qc, -(NA_WIN_C - 1), NA_WIN_C - 1) + (NA_WIN_C - 1)
    col_start = jnp.clip(qc - NA_WIN_C // 2, 0, GRID_W - NA_WIN_C)
    ok = (kc >= col_start) & (kc < col_start + NA_WIN_C)
    acc = jnp.zeros((GRID_W, LANES), F32)
    base = (h * (2 * NA_WIN_R - 1) + d) * n_dc
    for dc in range(n_dc):
        val = jnp.where(half, rpb_ref[base + n_dc + dc] * LOG2E, rpb_ref[base + dc] * LOG2E)
        acc = jnp.where(delta == dc, val, acc)
    o_ref[...] = jnp.where(ok, acc, NEG)


def _bias_call(rpb_l):
    return pl.pallas_call(
        _bias_kernel,
        grid_spec=pltpu.PrefetchScalarGridSpec(
            num_scalar_prefetch=1,
            grid=(HA, N_DR),
            in_specs=[],
            out_specs=pl.BlockSpec((None, None, GRID_W, LANES), lambda h, d, r: (h, d, 0, 0)),
        ),
        out_shape=jax.ShapeDtypeStruct((HA, N_DR, GRID_W, LANES), F32),
        compiler_params=_cparams(("arbitrary", "arbitrary")),
        name="na_bias_table",
    )(rpb_l.reshape(-1))


def _dec_a_kernel(q_ref, k_ref, v_ref, ck_ref, cv_ref, bias_ref, o_ref):
    rb = pl.program_id(2)
    rows = k_ref.shape[0] // GRID_W
    n_loc = NA_WIN_R * GRID_W
    _, m_lo, m_hi = _half_masks(BF16)
    lo = lax.broadcasted_iota(jnp.int32, (1, LANES), 1) < HEAD_DIM
    qs = [jnp.concatenate([q_ref[rr * GRID_W:(rr + 1) * GRID_W, :] * msk for msk in (m_lo, m_hi)], axis=0)
          for rr in range(ROWS_A)]
    s_ctx = _dot_nt(jnp.concatenate(qs, axis=0), ck_ref[...])
    s_loc, v_win = [], []
    for rr in range(ROWS_A):
        r = rb * ROWS_A + rr
        start = jnp.clip(r - NA_WIN_R // 2, 0, rows - NA_WIN_R)
        d0 = start - r + (NA_WIN_R - 1)
        off = pl.multiple_of(start * GRID_W, GRID_W)
        bias = jnp.concatenate(
            [jnp.concatenate([bias_ref[hh, d0 + 2 * i] for i in range(NA_WIN_R // 2)], axis=1) for hh in range(2)],
            axis=0)
        s_loc.append(_dot_nt(qs[rr], k_ref[pl.ds(off, n_loc), :]) + bias)
        v_win.append(v_ref[pl.ds(off, n_loc), :])
    m = jnp.maximum(jnp.concatenate([jnp.max(s, axis=-1, keepdims=True) for s in s_loc], axis=0),
                    jnp.max(s_ctx, axis=-1, keepdims=True))
    p_ctx = jnp.exp2(s_ctx - m)
    l_ctx = jnp.sum(p_ctx, axis=-1, keepdims=True)
    o_ctx = _dot(p_ctx.astype(BF16), cv_ref[...])
    for rr in range(ROWS_A):
        sl = slice(rr * 2 * GRID_W, (rr + 1) * 2 * GRID_W)
        p_loc = jnp.exp2(s_loc[rr] - m[sl])
        l = jnp.sum(p_loc, axis=-1, keepdims=True) + l_ctx[sl]
        o = (_dot(p_loc.astype(BF16), v_win[rr]) + o_ctx[sl]) / l
        o_c = jnp.where(lo, o[:GRID_W], o[GRID_W:])
        o_ref[rr * GRID_W:(rr + 1) * GRID_W, :] = o_c.astype(BF16)


def _dec_a_call(q, k_d, v_d, ck, cv, bias_tab):
    nb, seq, _ = k_d.shape
    n_ctx = ck.shape[1]
    tq = ROWS_A * GRID_W
    nq = seq // tq
    return pl.pallas_call(
        _dec_a_kernel,
        grid=(2, nb, nq),
        in_specs=[pl.BlockSpec((tq, LANES), lambda p, b, i: (b * nq + i, 4 + p)),
                  pl.BlockSpec((None, seq, LANES), lambda p, b, i: (b, 0, p)),
                  pl.BlockSpec((None, seq, LANES), lambda p, b, i: (b, 0, p)),
                  pl.BlockSpec((None, n_ctx, LANES), lambda p, b, i: (b, 0, p)),
                  pl.BlockSpec((None, n_ctx, LANES), lambda p, b, i: (b, 0, p)),
                  pl.BlockSpec((2, N_DR, GRID_W, LANES), lambda p, b, i: (p, 0, 0, 0))],
        out_specs=pl.BlockSpec((tq, LANES), lambda p, b, i: (b * nq + i, p)),
        out_shape=jax.ShapeDtypeStruct((q.shape[0], 2 * LANES), BF16),
        compiler_params=_cparams(("parallel", "parallel", "parallel")),
        name="dec_neighbourhood_attention",
    )(q, k_d, v_d, ck, cv, bias_tab)


def _merge_kernel(xc_ref, xd_ref, oc_ref, ob_ref, oa_ref, ow_ref, mod_ref, nm_ref, nf_ref, wg_ref, wbb_ref, wba_ref,
                  wbc_ref, wo_ref, wr_hi_ref, wr_lo_ref, xo_ref, h_ref, lg_ref, *, nt_c):
    d = xc_ref.shape[1]
    is_ctx = pl.program_id(0) < nt_c
    o_b = jnp.where(is_ctx, oc_ref[:, :512], ob_ref[...])
    o_a = jnp.where(is_ctx, oc_ref[:, 512:768], oa_ref[...])
    o_c = jnp.where(is_ctx, oc_ref[:, 768:], ow_ref[...])
    x = jnp.where(is_ctx, xc_ref[...], xd_ref[...])
    u = _modulated_norm(x, nm_ref[...], mod_ref[0, 1:2, :], mod_ref[0, 0:1, :])
    gates = _dot(u.astype(BF16), wg_ref[...])
    gates = 1.0 / (1.0 + jnp.exp(-gates))
    m = (gates[:, :d] * _dot(o_a, wba_ref[...])
         + gates[:, d:2 * d] * _dot(o_b, wbb_ref[...])
         + gates[:, 2 * d:] * _dot(o_c, wbc_ref[...]))
    x_new = x + mod_ref[0, 2:3, :] * _dot(m.astype(BF16), wo_ref[...])
    xo_ref[...] = x_new
    h = _modulated_norm(x_new, nf_ref[...], mod_ref[0, 4:5, :], mod_ref[0, 3:4, :])
    h_hi, h_lo = _split_bf16(h)
    h_ref[...] = _pack_bf16_pairs(h_hi)
    w_hi = wr_hi_ref[...]
    lg_ref[...] = _dot(h_hi, w_hi) + _dot(h_lo, w_hi) + _dot(h_hi, wr_lo_ref[...])


def _merge_call(x_c, x_d, x_d_tile_off, o_ctx, o_b, o_a, o_c, mods, nm, nf, wg, wbb, wba, wbc, wo, wr_hi, wr_lo,
                mod_map):
    d = x_c.shape[1]
    nt_c = o_ctx.shape[0] // TM
    t = o_ctx.shape[0] + o_b.shape[0]
    const = lambda i: (0, 0)
    full = lambda a: pl.BlockSpec(a.shape, const)
    ctx_map = lambda i: (jnp.minimum(i, nt_c - 1), 0)
    dec_map = lambda i: (jnp.maximum(i - nt_c, 0), 0)
    return pl.pallas_call(
        functools.partial(_merge_kernel, nt_c=nt_c),
        grid=(t // TM,),
        in_specs=[pl.BlockSpec((TM, d), ctx_map),
                  pl.BlockSpec((TM, d), lambda i: (jnp.maximum(i - nt_c, 0) + x_d_tile_off, 0)),
                  pl.BlockSpec((TM, 1024), ctx_map),
                  pl.BlockSpec((TM, 512), dec_map),
                  pl.BlockSpec((TM, 256), dec_map),
                  pl.BlockSpec((TM, 256), dec_map),
                  pl.BlockSpec((1, 6, d), mod_map),
                  full(nm), full(nf), full(wg), full(wbb), full(wba), full(wbc), full(wo),
                  full(wr_hi), full(wr_lo)],
        out_specs=[pl.BlockSpec((TM, d), lambda i: (i, 0)),
                   pl.BlockSpec((TM, d // 2), lambda i: (i, 0)),
                   pl.BlockSpec((TM, LANES), lambda i: (i, 0))],
        out_shape=[jax.ShapeDtypeStruct((t, d), F32),
                   jax.ShapeDtypeStruct((t, d // 2), jnp.uint32),
                   jax.ShapeDtypeStruct((t, LANES), F32)],
        compiler_params=_cparams(("parallel",)),
        name="merge_branches",
    )(x_c, x_d, o_ctx, o_b, o_a, o_c, mods, nm, nf, wg, wbb, wba, wbc, wo, wr_hi, wr_lo)


def _route_kernel(lg_ref, ri_ref, rg_ref, tab_ref, blk_ref, cnt_sc, *, nb_max):
    i = pl.program_id(0)
    n = pl.num_programs(0)
    tm = lg_ref.shape[0]

    @pl.when(i == 0)
    def _():
        cnt_sc[...] = jnp.zeros(cnt_sc.shape, F32)

    lg = lg_ref[...]
    lane = lax.broadcasted_iota(jnp.int32, (tm, LANES), 1).astype(F32)
    big = float(LANES)
    neg_inf = -jnp.inf
    gl = jnp.where(lane < N_GROUPS, lg, neg_inf)
    gmax = jnp.max(gl, axis=-1, keepdims=True)
    g_sel = jnp.min(jnp.where(gl == gmax, lane, big), axis=-1, keepdims=True)
    g_w = 1.0 / jnp.sum(jnp.exp(gl - gmax), axis=-1, keepdims=True)
    e_lo = N_GROUPS + EXPERTS_PER_GROUP * g_sel
    el = jnp.where((lane >= e_lo) & (lane < e_lo + EXPERTS_PER_GROUP), lg, neg_inf)
    m1 = jnp.max(el, axis=-1, keepdims=True)
    i1 = jnp.min(jnp.where(el == m1, lane, big), axis=-1, keepdims=True)
    el2 = jnp.where(lane == i1, neg_inf, el)
    m2 = jnp.max(el2, axis=-1, keepdims=True)
    i2 = jnp.min(jnp.where(el2 == m2, lane, big), axis=-1, keepdims=True)
    e2x = jnp.exp(m2 - m1)
    den = 1.0 + e2x
    gate1 = g_w * (1.0 / den)
    gate2 = g_w * (e2x / den)
    hit1 = lane == i1
    hit2 = lane == i2
    onehot = jnp.where(hit1 | hit2, 1.0, 0.0).astype(BF16)
    rr = lax.broadcasted_iota(jnp.int32, (tm, tm), 0)
    cc = lax.broadcasted_iota(jnp.int32, (tm, tm), 1)
    tri = jnp.where(cc <= rr, 1.0, 0.0).astype(BF16)
    pref = _dot(tri, onehot) + cnt_sc[...]
    rank1 = jnp.sum(jnp.where(hit1, pref, 0.0), axis=-1, keepdims=True) - 1.0
    rank2 = jnp.sum(jnp.where(hit2, pref, 0.0), axis=-1, keepdims=True) - 1.0
    cnt_sc[...] = pref[tm - 1:tm, :]
    e1 = (i1 - N_GROUPS).astype(jnp.int32)
    e2 = (i2 - N_GROUPS).astype(jnp.int32)
    pk1 = (e1 << 16) | rank1.astype(jnp.int32)
    pk2 = (e2 << 16) | rank2.astype(jnp.int32)
    lane_i = lax.broadcasted_iota(jnp.int32, (tm, LANES), 1)
    ri_ref[...] = jnp.where(lane_i == 0, pk1, jnp.where(lane_i == 1, pk2, 0))
    rg_ref[...] = jnp.where(lane_i == 0, gate1, jnp.where(lane_i == 1, gate2, 0.0))

    @pl.when(i == n - 1)
    def _():
        cnt = cnt_sc[...]
        nblk = jnp.floor((cnt + (BM - 1.0)) * (1.0 / BM))
        l8 = lax.broadcasted_iota(jnp.int32, (8, LANES), 1)
        is_exp = (l8 >= N_GROUPS) & (l8 < N_GROUPS + N_EXPERTS)
        nblk8 = jnp.where(is_exp, jnp.broadcast_to(nblk, (8, LANES)), 0.0)
        r2 = lax.broadcasted_iota(jnp.int32, (LANES, LANES), 0)
        c2 = lax.broadcasted_iota(jnp.int32, (LANES, LANES), 1)
        upper = jnp.where(r2 <= c2, 1.0, 0.0).astype(BF16)
        hi, lo_ = _split_bf16(nblk8)
        blk_end = _dot(hi, upper) + _dot(lo_, upper)
        blk_start = blk_end - nblk8
        row8 = lax.broadcasted_iota(jnp.int32, (8, LANES), 0)
        tab = jnp.where(row8 == 0, blk_start * BM, jnp.where(row8 == 1, blk_end, cnt))
        tab_ref[...] = tab.astype(jnp.int32)
        bi = lax.broadcasted_iota(jnp.int32, (nb_max, LANES), 0).astype(F32)
        ends = blk_end[0:1, :]
        lb = lax.broadcasted_iota(jnp.int32, (nb_max, LANES), 1)
        done = jnp.where((lb >= N_GROUPS) & (lb < N_GROUPS + N_EXPERTS) & (ends <= bi), 1.0, 0.0)
        be = jnp.minimum(jnp.sum(done, axis=-1, keepdims=True), N_EXPERTS - 1.0)
        blk_ref[...] = jnp.broadcast_to(be, (nb_max, LANES)).astype(jnp.int32)


def _route_call(logits, nb_max):
    t = logits.shape[0]
    return pl.pallas_call(
        functools.partial(_route_kernel, nb_max=nb_max),
        grid=(t // TM,),
        in_specs=[pl.BlockSpec((TM, LANES), lambda i: (i, 0))],
        out_specs=[pl.BlockSpec((TM, LANES), lambda i: (i, 0)),
                   pl.BlockSpec((TM, LANES), lambda i: (i, 0)),
                   pl.BlockSpec((8, LANES), lambda i: (0, 0)),
                   pl.BlockSpec((nb_max, LANES), lambda i: (0, 0))],
        out_shape=[jax.ShapeDtypeStruct((t, LANES), jnp.int32),
                   jax.ShapeDtypeStruct((t, LANES), F32),
                   jax.ShapeDtypeStruct((8, LANES), jnp.int32),
                   jax.ShapeDtypeStruct((nb_max, LANES), jnp.int32)],
        scratch_shapes=[pltpu.VMEM((1, LANES), F32)],
        compiler_params=_cparams(("arbitrary",)),
        name="moe_route",
    )(logits)


def _slots_kernel(ri_ref, tab_ref, o_ref):
    tm = ri_ref.shape[0]
    lane = lax.broadcasted_iota(jnp.int32, (tm, LANES), 1)
    starts = tab_ref[0:1, :].astype(F32)
    pk = ri_ref[...]
    out = jnp.zeros((tm, LANES), jnp.int32)
    for k in range(2):
        pkk = pk[:, k:k + 1]
        first = jnp.sum(jnp.where(lane == (pkk >> 16) + N_GROUPS, starts, 0.0), axis=-1, keepdims=True)
        out = jnp.where(lane == k, first.astype(jnp.int32) + (pkk & 0xFFFF), out)
    o_ref[...] = out


def _slots_call(route_i, tab):
    t = route_i.shape[0]
    return pl.pallas_call(
        _slots_kernel,
        grid=(t // TM,),
        in_specs=[pl.BlockSpec((TM, LANES), lambda i: (i, 0)),
                  pl.BlockSpec((8, LANES), lambda i: (0, 0))],
        out_specs=pl.BlockSpec((TM, LANES), lambda i: (i, 0)),
        out_shape=jax.ShapeDtypeStruct((t, LANES), jnp.int32),
        compiler_params=_cparams(("parallel",)),
        name="moe_slots",
    )(route_i, tab)


N_DISPATCH_BUF = 3


def _dispatch_kernel(dest_ref, xs_in_ref, h_ref, xs_ref, buf, fsem, ssem):
    del xs_in_ref
    i = pl.program_id(0)
    n = pl.num_programs(0)
    tm = buf.shape[1]

    def fetch(tile, slot):
        return pltpu.make_async_copy(h_ref.at[pl.ds(tile * tm, tm), :], buf.at[slot], fsem.at[slot])

    def scatter_wait(slot):
        for _ in range(2):
            pltpu.make_async_copy(buf.at[slot], xs_ref.at[pl.ds(0, tm), :], ssem.at[slot]).wait()

    @pl.when(i == 0)
    def _():
        fetch(0, 0).start()

        @pl.when(n > 1)
        def _():
            fetch(1, 1).start()

    slot = i % N_DISPATCH_BUF
    fetch(i, slot).wait()
    base = 2 * i * tm
    for r in range(tm):
        for k in range(2):
            dst = dest_ref[base + (2 * r + k)]
            pltpu.make_async_copy(buf.at[slot, pl.ds(r, 1), :], xs_ref.at[pl.ds(dst, 1), :],
                                  ssem.at[slot]).start(priority=k)

    @pl.when(i >= 1)
    def _():
        scatter_wait((i - 1) % N_DISPATCH_BUF)

    @pl.when(i + 2 < n)
    def _():
        fetch(i + 2, (i + 2) % N_DISPATCH_BUF).start()

    @pl.when(i == n - 1)
    def _():
        scatter_wait(slot)


def _dispatch_call(dest, xs_init, h):
    t, dw = h.shape
    return pl.pallas_call(
        _dispatch_kernel,
        grid_spec=pltpu.PrefetchScalarGridSpec(
            num_scalar_prefetch=1,
            grid=(t // TM,),
            in_specs=[pl.BlockSpec(memory_space=pl.ANY),
                      pl.BlockSpec(memory_space=pl.ANY)],
            out_specs=pl.BlockSpec(memory_space=pl.ANY),
            scratch_shapes=[pltpu.VMEM((N_DISPATCH_BUF, TM, dw), h.dtype),
                            pltpu.SemaphoreType.DMA((N_DISPATCH_BUF,)),
                            pltpu.SemaphoreType.DMA((N_DISPATCH_BUF,))],
        ),
        out_shape=jax.ShapeDtypeStruct(xs_init.shape, xs_init.dtype),
        input_output_aliases={1: 0},
        compiler_params=_cparams(("arbitrary",)),
        name="moe_dispatch",
    )(dest, xs_init, h)


def _expert_kernel(blk_ref, nused_ref, xs_ref, wg_ref, wu_ref, wd_ref, y_ref, wg_sc, wu_sc, wd_sc):
    i = pl.program_id(0)
    prev = blk_ref[jnp.maximum(i - 1, 0)]
    changed = (i == 0) | (blk_ref[i] != prev)

    @pl.when(changed)
    def _():
        wg_sc[...] = wg_ref[...].astype(BF16)
        wu_sc[...] = wu_ref[...].astype(BF16)
        wd_sc[...] = wd_ref[...].astype(BF16)

    @pl.when(i < nused_ref[0])
    def _():
        x = _unpack_bf16_pairs(xs_ref[...])
        g = _dot(x, wg_sc[...])
        u = _dot(x, wu_sc[...])
        hdn = (g * (1.0 / (1.0 + jnp.exp(-g)))) * u
        y_ref[...] = _dot(hdn.astype(BF16), wd_sc[...])

    @pl.when(i >= nused_ref[0])
    def _():
        y_ref[...] = jnp.zeros(y_ref.shape, F32)


def _expert_call(blk_expert, n_used, xs, wg, wu, wd, layer):
    p, dw = xs.shape
    d, ff = wg.shape[-2:]
    return pl.pallas_call(
        _expert_kernel,
        grid_spec=pltpu.PrefetchScalarGridSpec(
            num_scalar_prefetch=2,
            grid=(p // BM,),
            in_specs=[pl.BlockSpec((BM, dw), lambda i, b, n: (i, 0)),
                      pl.BlockSpec((None, None, d, ff), lambda i, b, n: (layer, b[i], 0, 0)),
                      pl.BlockSpec((None, None, d, ff), lambda i, b, n: (layer, b[i], 0, 0)),
                      pl.BlockSpec((None, None, ff, d), lambda i, b, n: (layer, b[i], 0, 0))],
            out_specs=pl.BlockSpec((BM, d), lambda i, b, n: (i, 0)),
            scratch_shapes=[pltpu.VMEM((d, ff), BF16), pltpu.VMEM((d, ff), BF16), pltpu.VMEM((ff, d), BF16)],
        ),
        out_shape=jax.ShapeDtypeStruct((p, d), F32),
        compiler_params=_cparams(("arbitrary",)),
        name="moe_experts",
    )(blk_expert, n_used, xs, wg, wu, wd)


def _combine_kernel(dest_ref, x_ref, rg_ref, mod_ref, y_ref, *rest, nt_c):
    out_refs, (buf, sem) = rest[:-2], rest[-2:]
    j = pl.program_id(0)
    n_tiles = pl.num_programs(0) - 1
    tm = x_ref.shape[0]

    @pl.when(j < n_tiles)
    def _():
        slot = j % 2
        base = 2 * j * tm
        for r in range(tm):
            for k in range(2):
                src = dest_ref[base + (2 * r + k)]
                pltpu.make_async_copy(y_ref.at[pl.ds(src, 1), :], buf.at[slot, k, pl.ds(r, 1), :],
                                      sem.at[slot]).start(priority=k)

    @pl.when(j >= 1)
    def _():
        slot = (j - 1) % 2
        for k in range(2):
            pltpu.make_async_copy(y_ref.at[pl.ds(0, tm), :], buf.at[slot, k], sem.at[slot]).wait()
        rg = rg_ref[...]
        y = buf[slot, 0] * rg[:, 0:1] + buf[slot, 1] * rg[:, 1:2]
        res = x_ref[...] + mod_ref[0, 5:6, :] * y
        if len(out_refs) == 1:
            out_refs[0][...] = res
        else:
            @pl.when(j - 1 < nt_c)
            def _():
                out_refs[0][...] = res

            @pl.when(j - 1 >= nt_c)
            def _():
                out_refs[1][...] = res


def _combine_call(dest, x_new, route_g, mods, yb, mod_map, nt_c, split):
    t, d = x_new.shape
    prev = lambda j: jnp.maximum(j - 1, 0)
    if split:
        out_specs = [pl.BlockSpec((TM, d), lambda j, p: (jnp.minimum(prev(j), nt_c - 1), 0)),
                     pl.BlockSpec((TM, d), lambda j, p: (jnp.maximum(prev(j) - nt_c, 0), 0))]
        out_shape = [jax.ShapeDtypeStruct((nt_c * TM, d), F32), jax.ShapeDtypeStruct((t - nt_c * TM, d), F32)]
    else:
        out_specs = [pl.BlockSpec((TM, d), lambda j, p: (prev(j), 0))]
        out_shape = [jax.ShapeDtypeStruct((t, d), F32)]
    return pl.pallas_call(
        functools.partial(_combine_kernel, nt_c=nt_c),
        grid_spec=pltpu.PrefetchScalarGridSpec(
            num_scalar_prefetch=1,
            grid=(t // TM + 1,),
            in_specs=[pl.BlockSpec((TM, d), lambda j, p: (prev(j), 0)),
                      pl.BlockSpec((TM, LANES), lambda j, p: (prev(j), 0)),
                      pl.BlockSpec((1, 6, d), lambda j, p: mod_map(prev(j))),
                      pl.BlockSpec(memory_space=pl.ANY)],
            out_specs=out_specs,
            scratch_shapes=[pltpu.VMEM((2, 2, TM, d), F32), pltpu.SemaphoreType.DMA((2,))],
        ),
        out_shape=out_shape,
        compiler_params=_cparams(("arbitrary",)),
        name="moe_combine",
    )(dest, x_new, route_g, mods, yb)


def _head_cols(base, heads):
    return np.concatenate([base + h * HEAD_DIM + np.arange(HEAD_DIM) for h in heads])


def _layouts():
    wa, wb, wc = HA * HEAD_DIM, HB * HEAD_DIM, HC * HEAD_DIM
    qa0, ka0, va0 = 0, wa, 2 * wa
    qb0 = 3 * wa
    kb0 = qb0 + wb
    vb0 = kb0 + KVB * HEAD_DIM
    qc0 = vb0 + KVB * HEAD_DIM
    kc0 = qc0 + wc
    vc0 = kc0 + KVC * HEAD_DIM
    b_heads = [h for c in range(4) for h in (c, c + 4)]
    c_heads = [0, 2, 1, 3]
    q_cols = np.concatenate([_head_cols(qb0, b_heads), _head_cols(qa0, range(HA)), _head_cols(qc0, c_heads)])
    k_cols = np.concatenate([ka0 + np.arange(wa), kb0 + np.arange(KVB * HEAD_DIM), kc0 + np.arange(KVC * HEAD_DIM)])
    v_cols = np.concatenate([va0 + np.arange(wa), vb0 + np.arange(KVB * HEAD_DIM), vc0 + np.arange(KVC * HEAD_DIM)])
    in_cols = np.concatenate([q_cols, k_cols, v_cols])
    gain_idx = np.concatenate([np.full(HB, 2), np.full(HA, 0), np.full(HC, 4),
                               np.full(HA, 1), np.full(KVB, 3), np.full(KVC, 5)])
    gain_scale = np.concatenate([np.full(HB + HA + HC, HEAD_DIM ** -0.5 * LOG2E), np.ones(HA + KVB + KVC)])
    return in_cols, gain_idx, gain_scale.astype(np.float32), _head_cols(0, b_heads), _head_cols(0, c_heads)


def _rope_tables(seq):
    quarter = HEAD_DIM // 4
    t = jnp.arange(seq)
    rows, cols = t // GRID_W, t % GRID_W
    freqs = ROPE_THETA ** (-jnp.arange(quarter, dtype=F32) / quarter)
    ang_r = rows.astype(F32)[:, None] * freqs[None, :]
    ang_c = cols.astype(F32)[:, None] * freqs[None, :]
    cos = jnp.concatenate([jnp.cos(ang_r), jnp.cos(ang_r), jnp.cos(ang_c), jnp.cos(ang_c)], axis=1)
    sin = jnp.concatenate([-jnp.sin(ang_r), jnp.sin(ang_r), -jnp.sin(ang_c), jnp.sin(ang_c)], axis=1)
    return jnp.tile(cos, (1, 2)), jnp.tile(sin, (1, 2))


def kernel(x_prompt, x_sample, cache_ctx_k, cache_ctx_v, c, c_ctx, norm_mix, norm_ffn, w_ada, b_ada, w_in, qk_gain,
           rpb, sink, w_br_a, w_br_b, w_br_c, w_gate, w_out, w_route_group, w_route_expert, w_exp_gate, w_exp_up,
           w_exp_down):
    batch, seq_c, d = x_prompt.shape
    nb, seq_d, _ = x_sample.shape
    depth = w_in.shape[0]
    t_c, t_d = batch * seq_c, nb * seq_d
    t = t_c + t_d
    n_ctx = cache_ctx_k.shape[2]
    assert t_c % TM == 0 and seq_d % TM == 0 and nb + 1 <= 8 and 2 * t < (1 << 16)
    nt_c = t_c // TM
    tiles_per_seq = seq_d // TM
    nb_max = 2 * t // BM + N_EXPERTS
    nb_max = (nb_max + 7) // 8 * 8

    in_cols, gain_idx, gain_scale, ob_rows, oc_rows = _layouts()
    x_c, x_d, x_d_off = x_prompt.reshape(t_c, d), x_sample.reshape(t_d, d), 0
    ones_ctx = jnp.ones((nb, n_ctx, LANES), BF16)
    cvec = jnp.concatenate([c_ctx[None, :], c, jnp.zeros((8 - 1 - nb, d), F32)], axis=0)
    mods = _ada_call(cvec, w_ada, b_ada).reshape(depth, 8, 6, d)
    rope_tabs = _rope_tables(seq_d)

    def mod_map(i):
        return (jnp.where(i < nt_c, 0, 1 + (i - nt_c) // tiles_per_seq), 0, 0)

    new_k, new_v = [], []
    for l in range(depth):
        w_in_p = w_in[l][:, in_cols].astype(BF16)
        gain_row = (qk_gain[l][gain_idx] * gain_scale[:, None]).reshape(1, -1)
        gain_row = jnp.concatenate([gain_row, jnp.ones((1, 512), F32)], axis=1)
        nm = norm_mix[l].reshape(1, d)
        nf = norm_ffn[l].reshape(1, d)

        q_c, k_c, v_c, k32, v32 = _qkv_call(x_c, mods[l], nm, w_in_p, gain_row, tile_off=0, n_tok=t_c,
                                            seq_len=seq_c, row_off=0, per_seq_mod=False, rope_tabs=None, emit_f32=True)
        new_k.append(k32.reshape(batch, seq_c, 8, HEAD_DIM))
        new_v.append(v32.reshape(batch, seq_c, 8, HEAD_DIM))
        q_d, k_d, v_d = _qkv_call(x_d, mods[l], nm, w_in_p, gain_row, tile_off=x_d_off, n_tok=t_d,
                                  seq_len=seq_d, row_off=1, per_seq_mod=True, rope_tabs=rope_tabs, emit_f32=False)
        k_d = k_d.reshape(nb, seq_d, 512)
        v_d = v_d.reshape(nb, seq_d, V_EXT_COLS)

        o_ctx = _ctx_attn_call(sink[l], q_c, k_c, v_c, seq_c)
        ck = cache_ctx_k[:, l].reshape(nb, n_ctx, 512).astype(BF16)
        cv = cache_ctx_v[:, l].reshape(nb, n_ctx, 512).astype(BF16)
        cv = jnp.concatenate([cv[:, :, :384], ones_ctx, cv[:, :, 384:], ones_ctx], axis=2)
        o_b = _dec_b_call(q_d, jnp.concatenate([k_d[:, :, 256:384], ck[:, :, 256:384]], axis=1),
                          jnp.concatenate([v_d[:, :, 256:512], cv[:, :, 256:512]], axis=1))
        o_c = _dec_c_call(sink[l], q_d, k_d, v_d, ck, cv)
        o_a = _dec_a_call(q_d, k_d, v_d, ck, cv, _bias_call(rpb[l]))

        w_route = jnp.concatenate([w_route_group[l], w_route_expert[l],
                                   jnp.zeros((d, LANES - N_GROUPS - N_EXPERTS), F32)], axis=1)
        wr_hi = w_route.astype(BF16)
        wr_lo = (w_route - wr_hi.astype(F32)).astype(BF16)
        x_new, h, logits = _merge_call(
            x_c, x_d, x_d_off, o_ctx, o_b, o_a, o_c, mods[l], nm, nf, w_gate[l].astype(BF16), w_br_b[l][ob_rows].astype(BF16),
            w_br_a[l].astype(BF16), w_br_c[l][oc_rows].astype(BF16), w_out[l].astype(BF16), wr_hi, wr_lo, mod_map)

        route_i, route_g, tab, blk = _route_call(logits, nb_max)
        dest = _slots_call(route_i, tab)[:, :2].reshape(-1)
        xs = _dispatch_call(dest, jnp.zeros((nb_max * BM, d // 2), jnp.uint32), h)
        yb = _expert_call(blk[:, 0], tab[1, N_GROUPS + N_EXPERTS - 1:N_GROUPS + N_EXPERTS], xs,
                          w_exp_gate, w_exp_up, w_exp_down, l)
        outs = _combine_call(dest, x_new, route_g, mods[l], yb, mod_map, nt_c, split=(l == depth - 1))
        if l < depth - 1:
            x_c = x_d = outs[0]
            x_d_off = nt_c

    y_prompt = outs[0].reshape(batch, seq_c, d)
    y_sample = outs[1].reshape(nb, seq_d, d)
    return (y_prompt, y_sample, jnp.stack(new_k, axis=1), jnp.stack(new_v, axis=1))
```

```python
import functools

import jax
import jax.numpy as jnp
import numpy as np
from jax import lax
from jax.experimental import pallas as pl
from jax.experimental.pallas import tpu as pltpu

GRID_W = 64
HEAD_DIM = 64
HA, HB, KVB, HC, KVC = 4, 8, 2, 4, 2
NA_WIN_R, NA_WIN_C = 8, 16
WINDOW = 128
ROPE_THETA = 10000.0
N_GROUPS, EXPERTS_PER_GROUP, N_EXPERTS = 4, 8, 32
EPS = 1e-6
NEG = -1e30
LOG2E = 1.4426950408889634

LANES = 128
V7X_VMEM_LIMIT = 56 * 1024 * 1024

TM = 512
TQ_B = 128
TK_B = 256
TQ_C = 256
ROWS_A = 4
BM = 512

F32 = jnp.float32
BF16 = jnp.bfloat16


def _cparams(sem, vmem=V7X_VMEM_LIMIT):
    return pltpu.CompilerParams(dimension_semantics=sem, vmem_limit_bytes=vmem)


def _split_bf16(x):
    hi = x.astype(BF16)
    lo = (x - hi.astype(F32)).astype(BF16)
    return hi, lo


def _pack_bf16_pairs(x_bf16):
    n = x_bf16.shape[1] // 2
    lo = lax.bitcast_convert_type(x_bf16[:, :n].astype(F32), jnp.uint32)
    hi = lax.bitcast_convert_type(x_bf16[:, n:].astype(F32), jnp.uint32)
    return (lo >> 16) | hi


def _unpack_bf16_pairs(w_u32):
    lo = lax.bitcast_convert_type(w_u32 << 16, F32)
    hi = lax.bitcast_convert_type(w_u32 & jnp.uint32(0xFFFF0000), F32)
    return jnp.concatenate([lo, hi], axis=1).astype(BF16)


def _dot(a, b):
    return jnp.dot(a, b, preferred_element_type=F32)


def _dot_nt(a, b):
    return lax.dot_general(a, b, (((1,), (1,)), ((), ())), preferred_element_type=F32)


def _half_masks(dtype):
    lane = lax.broadcasted_iota(jnp.int32, (1, LANES), 1)
    lo = (lane < HEAD_DIM)
    return lo, jnp.where(lo, 1.0, 0.0).astype(dtype), jnp.where(lo, 0.0, 1.0).astype(dtype)


def _ada_kernel(c_ref, w_ref, b_ref, o_ref):
    c = c_ref[...]
    s = c * (1.0 / (1.0 + jnp.exp(-c)))
    s_hi, s_lo = _split_bf16(s)
    w_hi, w_lo = _split_bf16(w_ref[...])
    o_ref[...] = _dot(s_hi, w_hi) + _dot(s_lo, w_hi) + _dot(s_hi, w_lo) + b_ref[...]


def _ada_call(cvec, w_ada, b_ada):
    depth, d, n6 = w_ada.shape
    tn = 512
    return pl.pallas_call(
        _ada_kernel,
        grid=(depth, n6 // tn),
        in_specs=[
            pl.BlockSpec((8, d), lambda l, j: (0, 0)),
            pl.BlockSpec((None, d, tn), lambda l, j: (l, 0, j)),
            pl.BlockSpec((None, 1, tn), lambda l, j: (l, 0, j)),
        ],
        out_specs=pl.BlockSpec((None, 8, tn), lambda l, j: (l, 0, j)),
        out_shape=jax.ShapeDtypeStruct((depth, 8, n6), F32),
        compiler_params=_cparams(("arbitrary", "arbitrary")),
        name="ada_modulation",
    )(cvec, w_ada, b_ada.reshape(depth, 1, n6))


N_QCHUNK = 8
N_KCHUNK = 4
ROPE_QCHUNKS = (0, 1, 2, 3, 6, 7)
ROPE_KCHUNKS = (2, 3)
V_EXT_COLS = 6 * LANES


def _modulated_norm(x, gain, scale, shift):
    ms = jnp.mean(x * x, axis=-1, keepdims=True)
    return (x * lax.rsqrt(ms + EPS) * gain) * (1.0 + scale) + shift


def _qkv_kernel(*refs, rope, emit_f32):
    x_ref, mod_ref, nm_ref, w_ref, gain_ref = refs[:5]
    pos = 5
    if rope:
        cos_ref, sin_ref = refs[pos:pos + 2]
        pos += 2
    q_ref, k_ref, v_ref = refs[pos:pos + 3]
    pos += 3
    if emit_f32:
        k32_ref, v32_ref = refs[pos:pos + 2]

    x = x_ref[...]
    u = _modulated_norm(x, nm_ref[...], mod_ref[0, 1:2, :], mod_ref[0, 0:1, :])
    y = _dot(u.astype(BF16), w_ref[...])

    lane = lax.broadcasted_iota(jnp.int32, (1, LANES), 1)
    lo = lane < HEAD_DIM
    first = (lane % 32) < 16
    if rope:
        cos_t = cos_ref[...]
        sin_t = sin_ref[...]

    def norm_chunk(j, use_rope):
        yj = y[:, j * LANES:(j + 1) * LANES]
        sq = yj * yj
        s_lo = jnp.sum(jnp.where(lo, sq, 0.0), axis=-1, keepdims=True)
        s_hi = jnp.sum(jnp.where(lo, 0.0, sq), axis=-1, keepdims=True)
        inv = lax.rsqrt(jnp.where(lo, s_lo, s_hi) * (1.0 / HEAD_DIM) + EPS)
        yj = (yj * inv) * gain_ref[:, j * LANES:(j + 1) * LANES]
        if use_rope:
            partner = jnp.where(first, pltpu.roll(yj, LANES - 16, axis=1), pltpu.roll(yj, 16, axis=1))
            yj = yj * cos_t + partner * sin_t
        return yj

    for j in range(N_QCHUNK):
        yj = norm_chunk(j, rope and j in ROPE_QCHUNKS)
        q_ref[:, j * LANES:(j + 1) * LANES] = yj.astype(BF16)
    for j in range(N_KCHUNK):
        yj = norm_chunk(N_QCHUNK + j, rope and j in ROPE_KCHUNKS)
        k_ref[:, j * LANES:(j + 1) * LANES] = yj.astype(BF16)
        if emit_f32:
            k32_ref[:, j * LANES:(j + 1) * LANES] = yj
    v = y[:, (N_QCHUNK + N_KCHUNK) * LANES:]
    if rope:
        ones = jnp.ones((v.shape[0], LANES), BF16)
        v_ref[:, :3 * LANES] = v[:, :3 * LANES].astype(BF16)
        v_ref[:, 3 * LANES:4 * LANES] = ones
        v_ref[:, 4 * LANES:5 * LANES] = v[:, 3 * LANES:].astype(BF16)
        v_ref[:, 5 * LANES:] = ones
    else:
        v_ref[...] = v.astype(BF16)
    if emit_f32:
        v32_ref[...] = v


def _qkv_call(x_all, mods, norm_mix_l, w_in_p, gain_row, *, tile_off, n_tok, seq_len, row_off,
              per_seq_mod, rope_tabs, emit_f32):
    d = x_all.shape[1]
    nt = n_tok // TM
    tiles_per_seq = max(seq_len // TM, 1)
    rope = rope_tabs is not None

    def mod_map(i):
        return (row_off + i // tiles_per_seq, 0, 0) if per_seq_mod else (row_off, 0, 0)

    in_specs = [
        pl.BlockSpec((TM, d), lambda i: (i + tile_off, 0)),
        pl.BlockSpec((1, 6, d), mod_map),
        pl.BlockSpec((1, d), lambda i: (0, 0)),
        pl.BlockSpec(w_in_p.shape, lambda i: (0, 0)),
        pl.BlockSpec((1, 2048), lambda i: (0, 0)),
    ]
    args = [x_all, mods, norm_mix_l, w_in_p, gain_row]
    if rope:
        in_specs += [pl.BlockSpec((TM, LANES), lambda i: (i % tiles_per_seq, 0))] * 2
        args += list(rope_tabs)
    v_cols = V_EXT_COLS if rope else 512
    out_shape = [jax.ShapeDtypeStruct((n_tok, 1024), BF16),
                 jax.ShapeDtypeStruct((n_tok, 512), BF16),
                 jax.ShapeDtypeStruct((n_tok, v_cols), BF16)]
    out_specs = [pl.BlockSpec((TM, 1024), lambda i: (i, 0)),
                 pl.BlockSpec((TM, 512), lambda i: (i, 0)),
                 pl.BlockSpec((TM, v_cols), lambda i: (i, 0))]
    if emit_f32:
        out_shape += [jax.ShapeDtypeStruct((n_tok, 512), F32)] * 2
        out_specs += [pl.BlockSpec((TM, 512), lambda i: (i, 0))] * 2
    return pl.pallas_call(
        functools.partial(_qkv_kernel, rope=rope, emit_f32=emit_f32),
        grid=(nt,),
        in_specs=in_specs,
        out_specs=out_specs,
        out_shape=out_shape,
        compiler_params=_cparams(("parallel",)),
        name="qkv_rope" if rope else "qkv_ctx",
    )(*args)


def _softmax_rows(s, sink_col=None):
    m = jnp.max(s, axis=-1, keepdims=True)
    if sink_col is not None:
        m = jnp.maximum(m, sink_col)
    p = jnp.exp2(s - m)
    l = jnp.sum(p, axis=-1, keepdims=True)
    if sink_col is not None:
        l = l + jnp.exp2(sink_col - m)
    return p, l


def _ctx_attn_kernel(sink_ref, q_ref, k_ref, v_ref, o_ref):
    s_len = q_ref.shape[0]
    _, m_lo, m_hi = _half_masks(BF16)
    lo = lax.broadcasted_iota(jnp.int32, (1, LANES), 1) < HEAD_DIM

    def group(q_chunks, kc, sinks=None):
        n = len(q_chunks)
        k = k_ref[:, kc * LANES:(kc + 1) * LANES]
        v = v_ref[:, kc * LANES:(kc + 1) * LANES]
        qs = jnp.concatenate([q_ref[:, c * LANES:(c + 1) * LANES] * m for m in (m_lo, m_hi) for c in q_chunks],
                             axis=0)
        s = _dot_nt(qs, k)
        sink_col = None
        if sinks is not None:
            row = lax.broadcasted_iota(jnp.int32, (2 * n * s_len, 1), 0)
            sink_col = jnp.zeros((2 * n * s_len, 1), F32)
            for idx, sv in enumerate(sinks):
                sink_col = jnp.where((row >= idx * s_len) & (row < (idx + 1) * s_len), sv, sink_col)
        p, l = _softmax_rows(s, sink_col)
        o = _dot(p.astype(BF16), v) / l
        for idx, c in enumerate(q_chunks):
            o_c = jnp.where(lo, o[idx * s_len:(idx + 1) * s_len], o[(n + idx) * s_len:(n + idx + 1) * s_len])
            o_ref[:, c * LANES:(c + 1) * LANES] = o_c.astype(BF16)

    group([0, 1, 2, 3], 2)
    group([4], 0)
    group([5], 1)
    group([6, 7], 3, sinks=[sink_ref[h] * LOG2E for h in range(HC)])


def _ctx_attn_call(sink_l, q, k, v, s_len):
    batch = q.shape[0] // s_len
    return pl.pallas_call(
        _ctx_attn_kernel,
        grid_spec=pltpu.PrefetchScalarGridSpec(
            num_scalar_prefetch=1,
            grid=(batch,),
            in_specs=[pl.BlockSpec((s_len, 1024), lambda b, s: (b, 0)),
                      pl.BlockSpec((s_len, 512), lambda b, s: (b, 0)),
                      pl.BlockSpec((s_len, 512), lambda b, s: (b, 0))],
            out_specs=pl.BlockSpec((s_len, 1024), lambda b, s: (b, 0)),
        ),
        out_shape=jax.ShapeDtypeStruct((q.shape[0], 1024), BF16),
        compiler_params=_cparams(("parallel",)),
        name="ctx_attention",
    )(sink_l, q, k, v)


def _dec_b_kernel(q_ref, k_ref, v_ref, o_ref):
    n_keys = k_ref.shape[0]
    n_chunks = n_keys // TK_B
    _, m_lo, m_hi = _half_masks(BF16)
    lo = lax.broadcasted_iota(jnp.int32, (1, LANES), 1) < HEAD_DIM

    def stacked_q(msk):
        return jnp.concatenate([q_ref[:, c * LANES:(c + 1) * LANES] * msk for c in range(4)], axis=0)

    def write(outs):
        for c in range(4):
            o_c = jnp.where(lo, outs[0][c * TQ_B:(c + 1) * TQ_B], outs[1][c * TQ_B:(c + 1) * TQ_B])
            o_ref[:, c * LANES:(c + 1) * LANES] = o_c.astype(BF16)

    outs, sums = [], []
    for msk in (m_lo, m_hi):
        qs = stacked_q(msk)
        s = _dot_nt(qs, k_ref[0:TK_B, :])
        m = jnp.max(s, axis=-1, keepdims=True)
        acc = _dot(jnp.exp2(s - m).astype(BF16), v_ref[0:TK_B, :])
        for t in range(1, n_chunks):
            s = _dot_nt(qs, k_ref[t * TK_B:(t + 1) * TK_B, :])
            acc = acc + _dot(jnp.exp2(s - m).astype(BF16), v_ref[t * TK_B:(t + 1) * TK_B, :])
        outs.append(acc[:, :LANES] / acc[:, LANES:])
        sums.append(jnp.sum(acc * 0.0))
    write(outs)

    @pl.when(jnp.logical_not(sums[0] + sums[1] == 0.0))
    def _():
        exact = []
        for msk in (m_lo, m_hi):
            s = _dot_nt(stacked_q(msk), k_ref[...])
            acc = _dot(jnp.exp2(s - jnp.max(s, axis=-1, keepdims=True)).astype(BF16), v_ref[...])
            exact.append(acc[:, :LANES] / acc[:, LANES:])
        write(exact)


def _dec_b_call(q, k_all, v_all):
    nb, n_keys, _ = k_all.shape
    seq = q.shape[0] // nb
    nq = seq // TQ_B
    return pl.pallas_call(
        _dec_b_kernel,
        grid=(nb, nq),
        in_specs=[pl.BlockSpec((TQ_B, 512), lambda b, i: (b * nq + i, 0)),
                  pl.BlockSpec((None, n_keys, LANES), lambda b, i: (b, 0, 0)),
                  pl.BlockSpec((None, n_keys, 2 * LANES), lambda b, i: (b, 0, 0))],
        out_specs=pl.BlockSpec((TQ_B, 512), lambda b, i: (b * nq + i, 0)),
        out_shape=jax.ShapeDtypeStruct((q.shape[0], 512), BF16),
        compiler_params=_cparams(("parallel", "parallel")),
        name="dec_dense_attention",
    )(q, k_all, v_all)


def _dec_c_kernel(sink_ref, q_ref, k_ref, v_ref, ck_ref, cv_ref, o_ref):
    qi = pl.program_id(1)
    seq = k_ref.shape[0]
    win = TQ_C + 2 * WINDOW
    _, m_lo, m_hi = _half_masks(BF16)
    lo = lax.broadcasted_iota(jnp.int32, (1, LANES), 1) < HEAD_DIM
    start = jnp.clip(qi * TQ_C - WINDOW, 0, seq - win)
    start = pl.multiple_of(start, WINDOW)
    k_win = k_ref[pl.ds(start, win), :]
    v_win = v_ref[pl.ds(start, win), :]
    k_ctx = ck_ref[...]
    v_ctx = cv_ref[...]
    row = lax.broadcasted_iota(jnp.int32, (2 * TQ_C, win), 0)
    col = lax.broadcasted_iota(jnp.int32, (2 * TQ_C, win), 1)
    qpos = qi * TQ_C + jnp.where(row >= TQ_C, row - TQ_C, row)
    ok = jnp.abs(qpos - (start + col)) <= WINDOW
    row1 = lax.broadcasted_iota(jnp.int32, (2 * TQ_C, 1), 0)
    outs = []
    for kvh, msk in enumerate((m_lo, m_hi)):
        qs = jnp.concatenate([q_ref[:, c * LANES:(c + 1) * LANES] * msk for c in range(2)], axis=0)
        s_loc = jnp.where(ok, _dot_nt(qs, k_win), NEG)
        s_ctx = _dot_nt(qs, k_ctx)
        sink_col = jnp.where(row1 < TQ_C, sink_ref[2 * kvh] * LOG2E, sink_ref[2 * kvh + 1] * LOG2E)
        m = jnp.maximum(jnp.maximum(jnp.max(s_loc, axis=-1, keepdims=True),
                                    jnp.max(s_ctx, axis=-1, keepdims=True)), sink_col)
        acc = (_dot(jnp.exp2(s_loc - m).astype(BF16), v_win)
               + _dot(jnp.exp2(s_ctx - m).astype(BF16), v_ctx))
        outs.append(acc[:, :LANES] / (acc[:, LANES:] + jnp.exp2(sink_col - m)))
    for c in range(2):
        o_c = jnp.where(lo, outs[0][c * TQ_C:(c + 1) * TQ_C], outs[1][c * TQ_C:(c + 1) * TQ_C])
        o_ref[:, c * LANES:(c + 1) * LANES] = o_c.astype(BF16)


def _dec_c_call(sink_l, q, k_d, v_d, ck, cv):
    nb, seq, _ = k_d.shape
    n_ctx = ck.shape[1]
    nq = seq // TQ_C
    return pl.pallas_call(
        _dec_c_kernel,
        grid_spec=pltpu.PrefetchScalarGridSpec(
            num_scalar_prefetch=1,
            grid=(nb, nq),
            in_specs=[pl.BlockSpec((TQ_C, 256), lambda b, i, s: (b * nq + i, 3)),
                      pl.BlockSpec((None, seq, LANES), lambda b, i, s: (b, 0, 3)),
                      pl.BlockSpec((None, seq, 2 * LANES), lambda b, i, s: (b, 0, 2)),
                      pl.BlockSpec((None, n_ctx, LANES), lambda b, i, s: (b, 0, 3)),
                      pl.BlockSpec((None, n_ctx, 2 * LANES), lambda b, i, s: (b, 0, 2))],
            out_specs=pl.BlockSpec((TQ_C, 256), lambda b, i, s: (b * nq + i, 0)),
        ),
        out_shape=jax.ShapeDtypeStruct((q.shape[0], 256), BF16),
        compiler_params=_cparams(("parallel", "parallel")),
        name="dec_window_attention",
    )(sink_l, q, k_d, v_d, ck, cv)


N_DR = 2 * NA_WIN_R - 2


def _bias_kernel(rpb_ref, o_ref):
    h = pl.program_id(0)
    d = pl.program_id(1)
    n_dc = 2 * NA_WIN_C - 1
    qc = lax.broadcasted_iota(jnp.int32, (GRID_W, LANES), 0)
    kk = lax.broadcasted_iota(jnp.int32, (GRID_W, LANES), 1)
    half = kk >= GRID_W
    kc = jnp.where(half, kk - GRID_W, kk)
    delta = jnp.clip(kc - qc, -(NA_WIN_C - 1), NA_WIN_C - 1) + (NA_WIN_C - 1)
    col_start = jnp.clip(qc - NA_WIN_C // 2, 0, GRID_W - NA_WIN_C)
    ok = (kc >= col_start) & (kc < col_start + NA_WIN_C)
    acc = jnp.zeros((GRID_W, LANES), F32)
    base = (h * (2 * NA_WIN_R - 1) + d) * n_dc
    for dc in range(n_dc):
        val = jnp.where(half, rpb_ref[base + n_dc + dc] * LOG2E, rpb_ref[base + dc] * LOG2E)
        acc = jnp.where(delta == dc, val, acc)
    o_ref[...] = jnp.where(ok, acc, NEG)


def _bias_call(rpb_l):
    return pl.pallas_call(
        _bias_kernel,
        grid_spec=pltpu.PrefetchScalarGridSpec(
            num_scalar_prefetch=1,
            grid=(HA, N_DR),
            in_specs=[],
            out_specs=pl.BlockSpec((None, None, GRID_W, LANES), lambda h, d, r: (h, d, 0, 0)),
        ),
        out_shape=jax.ShapeDtypeStruct((HA, N_DR, GRID_W, LANES), F32),
        compiler_params=_cparams(("arbitrary", "arbitrary")),
        name="na_bias_table",
    )(rpb_l.reshape(-1))


def _dec_a_kernel(q_ref, k_ref, v_ref, ck_ref, cv_ref, bias_ref, o_ref):
    rb = pl.program_id(2)
    rows = k_ref.shape[0] // GRID_W
    n_loc = NA_WIN_R * GRID_W
    _, m_lo, m_hi = _half_masks(BF16)
    lo = lax.broadcasted_iota(jnp.int32, (1, LANES), 1) < HEAD_DIM
    qs = [jnp.concatenate([q_ref[rr * GRID_W:(rr + 1) * GRID_W, :] * msk for msk in (m_lo, m_hi)], axis=0)
          for rr in range(ROWS_A)]
    s_ctx = _dot_nt(jnp.concatenate(qs, axis=0), ck_ref[...])
    s_loc, v_win = [], []
    for rr in range(ROWS_A):
        r = rb * ROWS_A + rr
        start = jnp.clip(r - NA_WIN_R // 2, 0, rows - NA_WIN_R)
        d0 = start - r + (NA_WIN_R - 1)
        off = pl.multiple_of(start * GRID_W, GRID_W)
        bias = jnp.concatenate(
            [jnp.concatenate([bias_ref[hh, d0 + 2 * i] for i in range(NA_WIN_R // 2)], axis=1) for hh in range(2)],
            axis=0)
        s_loc.append(_dot_nt(qs[rr], k_ref[pl.ds(off, n_loc), :]) + bias)
        v_win.append(v_ref[pl.ds(off, n_loc), :])
    m = jnp.maximum(jnp.concatenate([jnp.max(s, axis=-1, keepdims=True) for s in s_loc], axis=0),
                    jnp.max(s_ctx, axis=-1, keepdims=True))
    p_ctx = jnp.exp2(s_ctx - m)
    l_ctx = jnp.sum(p_ctx, axis=-1, keepdims=True)
    o_ctx = _dot(p_ctx.astype(BF16), cv_ref[...])
    for rr in range(ROWS_A):
        sl = slice(rr * 2 * GRID_W, (rr + 1) * 2 * GRID_W)
        p_loc = jnp.exp2(s_loc[rr] - m[sl])
        l = jnp.sum(p_loc, axis=-1, keepdims=True) + l_ctx[sl]
        o = (_dot(p_loc.astype(BF16), v_win[rr]) + o_ctx[sl]) / l
        o_c = jnp.where(lo, o[:GRID_W], o[GRID_W:])
        o_ref[rr * GRID_W:(rr + 1) * GRID_W, :] = o_c.astype(BF16)


def _dec_a_call(q, k_d, v_d, ck, cv, bias_tab):
    nb, seq, _ = k_d.shape
    n_ctx = ck.shape[1]
    tq = ROWS_A * GRID_W
    nq = seq // tq
    return pl.pallas_call(
        _dec_a_kernel,
        grid=(2, nb, nq),
        in_specs=[pl.BlockSpec((tq, LANES), lambda p, b, i: (b * nq + i, 4 + p)),
                  pl.BlockSpec((None, seq, LANES), lambda p, b, i: (b, 0, p)),
                  pl.BlockSpec((None, seq, LANES), lambda p, b, i: (b, 0, p)),
                  pl.BlockSpec((None, n_ctx, LANES), lambda p, b, i: (b, 0, p)),
                  pl.BlockSpec((None, n_ctx, LANES), lambda p, b, i: (b, 0, p)),
                  pl.BlockSpec((2, N_DR, GRID_W, LANES), lambda p, b, i: (p, 0, 0, 0))],
        out_specs=pl.BlockSpec((tq, LANES), lambda p, b, i: (b * nq + i, p)),
        out_shape=jax.ShapeDtypeStruct((q.shape[0], 2 * LANES), BF16),
        compiler_params=_cparams(("parallel", "parallel", "parallel")),
        name="dec_neighbourhood_attention",
    )(q, k_d, v_d, ck, cv, bias_tab)


def _merge_kernel(xc_ref, xd_ref, oc_ref, ob_ref, oa_ref, ow_ref, mod_ref, nm_ref, nf_ref, wg_ref, wbb_ref, wba_ref,
                  wbc_ref, wo_ref, wr_hi_ref, wr_lo_ref, xo_ref, h_ref, lg_ref, *, nt_c):
    d = xc_ref.shape[1]
    is_ctx = pl.program_id(0) < nt_c
    o_b = jnp.where(is_ctx, oc_ref[:, :512], ob_ref[...])
    o_a = jnp.where(is_ctx, oc_ref[:, 512:768], oa_ref[...])
    o_c = jnp.where(is_ctx, oc_ref[:, 768:], ow_ref[...])
    x = jnp.where(is_ctx, xc_ref[...], xd_ref[...])
    u = _modulated_norm(x, nm_ref[...], mod_ref[0, 1:2, :], mod_ref[0, 0:1, :])
    gates = _dot(u.astype(BF16), wg_ref[...])
    gates = 1.0 / (1.0 + jnp.exp(-gates))
    m = (gates[:, :d] * _dot(o_a, wba_ref[...])
         + gates[:, d:2 * d] * _dot(o_b, wbb_ref[...])
         + gates[:, 2 * d:] * _dot(o_c, wbc_ref[...]))
    x_new = x + mod_ref[0, 2:3, :] * _dot(m.astype(BF16), wo_ref[...])
    xo_ref[...] = x_new
    h = _modulated_norm(x_new, nf_ref[...], mod_ref[0, 4:5, :], mod_ref[0, 3:4, :])
    h_hi, h_lo = _split_bf16(h)
    h_ref[...] = _pack_bf16_pairs(h_hi)
    w_hi = wr_hi_ref[...]
    lg_ref[...] = _dot(h_hi, w_hi) + _dot(h_lo, w_hi) + _dot(h_hi, wr_lo_ref[...])


def _merge_call(x_c, x_d, x_d_tile_off, o_ctx, o_b, o_a, o_c, mods, nm, nf, wg, wbb, wba, wbc, wo, wr_hi, wr_lo,
                mod_map):
    d = x_c.shape[1]
    nt_c = o_ctx.shape[0] // TM
    t = o_ctx.shape[0] + o_b.shape[0]
    const = lambda i: (0, 0)
    full = lambda a: pl.BlockSpec(a.shape, const)
    ctx_map = lambda i: (jnp.minimum(i, nt_c - 1), 0)
    dec_map = lambda i: (jnp.maximum(i - nt_c, 0), 0)
    return pl.pallas_call(
        functools.partial(_merge_kernel, nt_c=nt_c),
        grid=(t // TM,),
        in_specs=[pl.BlockSpec((TM, d), ctx_map),
                  pl.BlockSpec((TM, d), lambda i: (jnp.maximum(i - nt_c, 0) + x_d_tile_off, 0)),
                  pl.BlockSpec((TM, 1024), ctx_map),
                  pl.BlockSpec((TM, 512), dec_map),
                  pl.BlockSpec((TM, 256), dec_map),
                  pl.BlockSpec((TM, 256), dec_map),
                  pl.BlockSpec((1, 6, d), mod_map),
                  full(nm), full(nf), full(wg), full(wbb), full(wba), full(wbc), full(wo),
                  full(wr_hi), full(wr_lo)],
        out_specs=[pl.BlockSpec((TM, d), lambda i: (i, 0)),
                   pl.BlockSpec((TM, d // 2), lambda i: (i, 0)),
                   pl.BlockSpec((TM, LANES), lambda i: (i, 0))],
        out_shape=[jax.ShapeDtypeStruct((t, d), F32),
                   jax.ShapeDtypeStruct((t, d // 2), jnp.uint32),
                   jax.ShapeDtypeStruct((t, LANES), F32)],
        compiler_params=_cparams(("parallel",)),
        name="merge_branches",
    )(x_c, x_d, o_ctx, o_b, o_a, o_c, mods, nm, nf, wg, wbb, wba, wbc, wo, wr_hi, wr_lo)


def _route_kernel(lg_ref, ri_ref, rg_ref, tab_ref, blk_ref, cnt_sc, *, nb_max):
    i = pl.program_id(0)
    n = pl.num_programs(0)
    tm = lg_ref.shape[0]

    @pl.when(i == 0)
    def _():
        cnt_sc[...] = jnp.zeros(cnt_sc.shape, F32)

    lg = lg_ref[...]
    lane = lax.broadcasted_iota(jnp.int32, (tm, LANES), 1).astype(F32)
    big = float(LANES)
    neg_inf = -jnp.inf
    gl = jnp.where(lane < N_GROUPS, lg, neg_inf)
    gmax = jnp.max(gl, axis=-1, keepdims=True)
    g_sel = jnp.min(jnp.where(gl == gmax, lane, big), axis=-1, keepdims=True)
    g_w = 1.0 / jnp.sum(jnp.exp(gl - gmax), axis=-1, keepdims=True)
    e_lo = N_GROUPS + EXPERTS_PER_GROUP * g_sel
    el = jnp.where((lane >= e_lo) & (lane < e_lo + EXPERTS_PER_GROUP), lg, neg_inf)
    m1 = jnp.max(el, axis=-1, keepdims=True)
    i1 = jnp.min(jnp.where(el == m1, lane, big), axis=-1, keepdims=True)
    el2 = jnp.where(lane == i1, neg_inf, el)
    m2 = jnp.max(el2, axis=-1, keepdims=True)
    i2 = jnp.min(jnp.where(el2 == m2, lane, big), axis=-1, keepdims=True)
    e2x = jnp.exp(m2 - m1)
    den = 1.0 + e2x
    gate1 = g_w * (1.0 / den)
    gate2 = g_w * (e2x / den)
    hit1 = lane == i1
    hit2 = lane == i2
    onehot = jnp.where(hit1 | hit2, 1.0, 0.0).astype(BF16)
    rr = lax.broadcasted_iota(jnp.int32, (tm, tm), 0)
    cc = lax.broadcasted_iota(jnp.int32, (tm, tm), 1)
    tri = jnp.where(cc <= rr, 1.0, 0.0).astype(BF16)
    pref = _dot(tri, onehot) + cnt_sc[...]
    rank1 = jnp.sum(jnp.where(hit1, pref, 0.0), axis=-1, keepdims=True) - 1.0
    rank2 = jnp.sum(jnp.where(hit2, pref, 0.0), axis=-1, keepdims=True) - 1.0
    cnt_sc[...] = pref[tm - 1:tm, :]
    e1 = (i1 - N_GROUPS).astype(jnp.int32)
    e2 = (i2 - N_GROUPS).astype(jnp.int32)
    pk1 = (e1 << 16) | rank1.astype(jnp.int32)
    pk2 = (e2 << 16) | rank2.astype(jnp.int32)
    lane_i = lax.broadcasted_iota(jnp.int32, (tm, LANES), 1)
    ri_ref[...] = jnp.where(lane_i == 0, pk1, jnp.where(lane_i == 1, pk2, 0))
    rg_ref[...] = jnp.where(lane_i == 0, gate1, jnp.where(lane_i == 1, gate2, 0.0))

    @pl.when(i == n - 1)
    def _():
        cnt = cnt_sc[...]
        nblk = jnp.floor((cnt + (BM - 1.0)) * (1.0 / BM))
        l8 = lax.broadcasted_iota(jnp.int32, (8, LANES), 1)
        is_exp = (l8 >= N_GROUPS) & (l8 < N_GROUPS + N_EXPERTS)
        nblk8 = jnp.where(is_exp, jnp.broadcast_to(nblk, (8, LANES)), 0.0)
        r2 = lax.broadcasted_iota(jnp.int32, (LANES, LANES), 0)
        c2 = lax.broadcasted_iota(jnp.int32, (LANES, LANES), 1)
        upper = jnp.where(r2 <= c2, 1.0, 0.0).astype(BF16)
        hi, lo_ = _split_bf16(nblk8)
        blk_end = _dot(hi, upper) + _dot(lo_, upper)
        blk_start = blk_end - nblk8
        row8 = lax.broadcasted_iota(jnp.int32, (8, LANES), 0)
        tab = jnp.where(row8 == 0, blk_start * BM, jnp.where(row8 == 1, blk_end, cnt))
        tab_ref[...] = tab.astype(jnp.int32)
        bi = lax.broadcasted_iota(jnp.int32, (nb_max, LANES), 0).astype(F32)
        ends = blk_end[0:1, :]
        lb = lax.broadcasted_iota(jnp.int32, (nb_max, LANES), 1)
        done = jnp.where((lb >= N_GROUPS) & (lb < N_GROUPS + N_EXPERTS) & (ends <= bi), 1.0, 0.0)
        be = jnp.minimum(jnp.sum(done, axis=-1, keepdims=True), N_EXPERTS - 1.0)
        blk_ref[...] = jnp.broadcast_to(be, (nb_max, LANES)).astype(jnp.int32)


def _route_call(logits, nb_max):
    t = logits.shape[0]
    return pl.pallas_call(
        functools.partial(_route_kernel, nb_max=nb_max),
        grid=(t // TM,),
        in_specs=[pl.BlockSpec((TM, LANES), lambda i: (i, 0))],
        out_specs=[pl.BlockSpec((TM, LANES), lambda i: (i, 0)),
                   pl.BlockSpec((TM, LANES), lambda i: (i, 0)),
                   pl.BlockSpec((8, LANES), lambda i: (0, 0)),
                   pl.BlockSpec((nb_max, LANES), lambda i: (0, 0))],
        out_shape=[jax.ShapeDtypeStruct((t, LANES), jnp.int32),
                   jax.ShapeDtypeStruct((t, LANES), F32),
                   jax.ShapeDtypeStruct((8, LANES), jnp.int32),
                   jax.ShapeDtypeStruct((nb_max, LANES), jnp.int32)],
        scratch_shapes=[pltpu.VMEM((1, LANES), F32)],
        compiler_params=_cparams(("arbitrary",)),
        name="moe_route",
    )(logits)


def _slots_kernel(ri_ref, tab_ref, o_ref):
    tm = ri_ref.shape[0]
    lane = lax.broadcasted_iota(jnp.int32, (tm, LANES), 1)
    starts = tab_ref[0:1, :].astype(F32)
    pk = ri_ref[...]
    out = jnp.zeros((tm, LANES), jnp.int32)
    for k in range(2):
        pkk = pk[:, k:k + 1]
        first = jnp.sum(jnp.where(lane == (pkk >> 16) + N_GROUPS, starts, 0.0), axis=-1, keepdims=True)
        out = jnp.where(lane == k, first.astype(jnp.int32) + (pkk & 0xFFFF), out)
    o_ref[...] = out


def _slots_call(route_i, tab):
    t = route_i.shape[0]
    return pl.pallas_call(
        _slots_kernel,
        grid=(t // TM,),
        in_specs=[pl.BlockSpec((TM, LANES), lambda i: (i, 0)),
                  pl.BlockSpec((8, LANES), lambda i: (0, 0))],
        out_specs=pl.BlockSpec((TM, LANES), lambda i: (i, 0)),
        out_shape=jax.ShapeDtypeStruct((t, LANES), jnp.int32),
        compiler_params=_cparams(("parallel",)),
        name="moe_slots",
    )(route_i, tab)


N_DISPATCH_BUF = 3


def _dispatch_kernel(dest_ref, xs_in_ref, h_ref, xs_ref, buf, fsem, ssem):
    del xs_in_ref
    i = pl.program_id(0)
    n = pl.num_programs(0)
    tm = buf.shape[1]

    def fetch(tile, slot):
        return pltpu.make_async_copy(h_ref.at[pl.ds(tile * tm, tm), :], buf.at[slot], fsem.at[slot])

    def scatter_wait(slot):
        for _ in range(2):
            pltpu.make_async_copy(buf.at[slot], xs_ref.at[pl.ds(0, tm), :], ssem.at[slot]).wait()

    @pl.when(i == 0)
    def _():
        fetch(0, 0).start()

        @pl.when(n > 1)
        def _():
            fetch(1, 1).start()

    slot = i % N_DISPATCH_BUF
    fetch(i, slot).wait()
    base = 2 * i * tm
    for r in range(tm):
        for k in range(2):
            dst = dest_ref[base + (2 * r + k)]
            pltpu.make_async_copy(buf.at[slot, pl.ds(r, 1), :], xs_ref.at[pl.ds(dst, 1), :],
                                  ssem.at[slot]).start(priority=k)

    @pl.when(i >= 1)
    def _():
        scatter_wait((i - 1) % N_DISPATCH_BUF)

    @pl.when(i + 2 < n)
    def _():
        fetch(i + 2, (i + 2) % N_DISPATCH_BUF).start()

    @pl.when(i == n - 1)
    def _():
        scatter_wait(slot)


def _dispatch_call(dest, xs_init, h):
    t, dw = h.shape
    return pl.pallas_call(
        _dispatch_kernel,
        grid_spec=pltpu.PrefetchScalarGridSpec(
            num_scalar_prefetch=1,
            grid=(t // TM,),
            in_specs=[pl.BlockSpec(memory_space=pl.ANY),
                      pl.BlockSpec(memory_space=pl.ANY)],
            out_specs=pl.BlockSpec(memory_space=pl.ANY),
            scratch_shapes=[pltpu.VMEM((N_DISPATCH_BUF, TM, dw), h.dtype),
                            pltpu.SemaphoreType.DMA((N_DISPATCH_BUF,)),
                            pltpu.SemaphoreType.DMA((N_DISPATCH_BUF,))],
        ),
        out_shape=jax.ShapeDtypeStruct(xs_init.shape, xs_init.dtype),
        input_output_aliases={1: 0},
        compiler_params=_cparams(("arbitrary",)),
        name="moe_dispatch",
    )(dest, xs_init, h)


def _expert_kernel(blk_ref, nused_ref, xs_ref, wg_ref, wu_ref, wd_ref, y_ref, wg_sc, wu_sc, wd_sc):
    i = pl.program_id(0)
    prev = blk_ref[jnp.maximum(i - 1, 0)]
    changed = (i == 0) | (blk_ref[i] != prev)

    @pl.when(changed)
    def _():
        wg_sc[...] = wg_ref[...].astype(BF16)
        wu_sc[...] = wu_ref[...].astype(BF16)
        wd_sc[...] = wd_ref[...].astype(BF16)

    @pl.when(i < nused_ref[0])
    def _():
        x = _unpack_bf16_pairs(xs_ref[...])
        g = _dot(x, wg_sc[...])
        u = _dot(x, wu_sc[...])
        hdn = (g * (1.0 / (1.0 + jnp.exp(-g)))) * u
        y_ref[...] = _dot(hdn.astype(BF16), wd_sc[...])

    @pl.when(i >= nused_ref[0])
    def _():
        y_ref[...] = jnp.zeros(y_ref.shape, F32)


def _expert_call(blk_expert, n_used, xs, wg, wu, wd, layer):
    p, dw = xs.shape
    d, ff = wg.shape[-2:]
    return pl.pallas_call(
        _expert_kernel,
        grid_spec=pltpu.PrefetchScalarGridSpec(
            num_scalar_prefetch=2,
            grid=(p // BM,),
            in_specs=[pl.BlockSpec((BM, dw), lambda i, b, n: (i, 0)),
                      pl.BlockSpec((None, None, d, ff), lambda i, b, n: (layer, b[i], 0, 0)),
                      pl.BlockSpec((None, None, d, ff), lambda i, b, n: (layer, b[i], 0, 0)),
                      pl.BlockSpec((None, None, ff, d), lambda i, b, n: (layer, b[i], 0, 0))],
            out_specs=pl.BlockSpec((BM, d), lambda i, b, n: (i, 0)),
            scratch_shapes=[pltpu.VMEM((d, ff), BF16), pltpu.VMEM((d, ff), BF16), pltpu.VMEM((ff, d), BF16)],
        ),
        out_shape=jax.ShapeDtypeStruct((p, d), F32),
        compiler_params=_cparams(("arbitrary",)),
        name="moe_experts",
    )(blk_expert, n_used, xs, wg, wu, wd)


def _combine_kernel(dest_ref, x_ref, rg_ref, mod_ref, y_ref, *rest, nt_c):
    out_refs, (buf, sem) = rest[:-2], rest[-2:]
    j = pl.program_id(0)
    n_tiles = pl.num_programs(0) - 1
    tm = x_ref.shape[0]

    @pl.when(j < n_tiles)
    def _():
        slot = j % 2
        base = 2 * j * tm
        for r in range(tm):
            for k in range(2):
                src = dest_ref[base + (2 * r + k)]
                pltpu.make_async_copy(y_ref.at[pl.ds(src, 1), :], buf.at[slot, k, pl.ds(r, 1), :],
                                      sem.at[slot]).start(priority=k)

    @pl.when(j >= 1)
    def _():
        slot = (j - 1) % 2
        for k in range(2):
            pltpu.make_async_copy(y_ref.at[pl.ds(0, tm), :], buf.at[slot, k], sem.at[slot]).wait()
        rg = rg_ref[...]
        y = buf[slot, 0] * rg[:, 0:1] + buf[slot, 1] * rg[:, 1:2]
        res = x_ref[...] + mod_ref[0, 5:6, :] * y
        if len(out_refs) == 1:
            out_refs[0][...] = res
        else:
            @pl.when(j - 1 < nt_c)
            def _():
                out_refs[0][...] = res

            @pl.when(j - 1 >= nt_c)
            def _():
                out_refs[1][...] = res


def _combine_call(dest, x_new, route_g, mods, yb, mod_map, nt_c, split):
    t, d = x_new.shape
    prev = lambda j: jnp.maximum(j - 1, 0)
    if split:
        out_specs = [pl.BlockSpec((TM, d), lambda j, p: (jnp.minimum(prev(j), nt_c - 1), 0)),
                     pl.BlockSpec((TM, d), lambda j, p: (jnp.maximum(prev(j) - nt_c, 0), 0))]
        out_shape = [jax.ShapeDtypeStruct((nt_c * TM, d), F32), jax.ShapeDtypeStruct((t - nt_c * TM, d), F32)]
    else:
        out_specs = [pl.BlockSpec((TM, d), lambda j, p: (prev(j), 0))]
        out_shape = [jax.ShapeDtypeStruct((t, d), F32)]
    return pl.pallas_call(
        functools.partial(_combine_kernel, nt_c=nt_c),
        grid_spec=pltpu.PrefetchScalarGridSpec(
            num_scalar_prefetch=1,
            grid=(t // TM + 1,),
            in_specs=[pl.BlockSpec((TM, d), lambda j, p: (prev(j), 0)),
                      pl.BlockSpec((TM, LANES), lambda j, p: (prev(j), 0)),
                      pl.BlockSpec((1, 6, d), lambda j, p: mod_map(prev(j))),
                      pl.BlockSpec(memory_space=pl.ANY)],
            out_specs=out_specs,
            scratch_shapes=[pltpu.VMEM((2, 2, TM, d), F32), pltpu.SemaphoreType.DMA((2,))],
        ),
        out_shape=out_shape,
        compiler_params=_cparams(("arbitrary",)),
        name="moe_combine",
    )(dest, x_new, route_g, mods, yb)


def _head_cols(base, heads):
    return np.concatenate([base + h * HEAD_DIM + np.arange(HEAD_DIM) for h in heads])


def _layouts():
    wa, wb, wc = HA * HEAD_DIM, HB * HEAD_DIM, HC * HEAD_DIM
    qa0, ka0, va0 = 0, wa, 2 * wa
    qb0 = 3 * wa
    kb0 = qb0 + wb
    vb0 = kb0 + KVB * HEAD_DIM
    qc0 = vb0 + KVB * HEAD_DIM
    kc0 = qc0 + wc
    vc0 = kc0 + KVC * HEAD_DIM
    b_heads = [h for c in range(4) for h in (c, c + 4)]
    c_heads = [0, 2, 1, 3]
    q_cols = np.concatenate([_head_cols(qb0, b_heads), _head_cols(qa0, range(HA)), _head_cols(qc0, c_heads)])
    k_cols = np.concatenate([ka0 + np.arange(wa), kb0 + np.arange(KVB * HEAD_DIM), kc0 + np.arange(KVC * HEAD_DIM)])
    v_cols = np.concatenate([va0 + np.arange(wa), vb0 + np.arange(KVB * HEAD_DIM), vc0 + np.arange(KVC * HEAD_DIM)])
    in_cols = np.concatenate([q_cols, k_cols, v_cols])
    gain_idx = np.concatenate([np.full(HB, 2), np.full(HA, 0), np.full(HC, 4),
                               np.full(HA, 1), np.full(KVB, 3), np.full(KVC, 5)])
    gain_scale = np.concatenate([np.full(HB + HA + HC, HEAD_DIM ** -0.5 * LOG2E), np.ones(HA + KVB + KVC)])
    return in_cols, gain_idx, gain_scale.astype(np.float32), _head_cols(0, b_heads), _head_cols(0, c_heads)


def _rope_tables(seq):
    quarter = HEAD_DIM // 4
    t = jnp.arange(seq)
    rows, cols = t // GRID_W, t % GRID_W
    freqs = ROPE_THETA ** (-jnp.arange(quarter, dtype=F32) / quarter)
    ang_r = rows.astype(F32)[:, None] * freqs[None, :]
    ang_c = cols.astype(F32)[:, None] * freqs[None, :]
    cos = jnp.concatenate([jnp.cos(ang_r), jnp.cos(ang_r), jnp.cos(ang_c), jnp.cos(ang_c)], axis=1)
    sin = jnp.concatenate([-jnp.sin(ang_r), jnp.sin(ang_r), -jnp.sin(ang_c), jnp.sin(ang_c)], axis=1)
    return jnp.tile(cos, (1, 2)), jnp.tile(sin, (1, 2))


def kernel(x_prompt, x_sample, cache_ctx_k, cache_ctx_v, c, c_ctx, norm_mix, norm_ffn, w_ada, b_ada, w_in, qk_gain,
           rpb, sink, w_br_a, w_br_b, w_br_c, w_gate, w_out, w_route_group, w_route_expert, w_exp_gate, w_exp_up,
           w_exp_down):
    batch, seq_c, d = x_prompt.shape
    nb, seq_d, _ = x_sample.shape
    depth = w_in.shape[0]
    t_c, t_d = batch * seq_c, nb * seq_d
    t = t_c + t_d
    n_ctx = cache_ctx_k.shape[2]
    assert t_c % TM == 0 and seq_d % TM == 0 and nb + 1 <= 8 and 2 * t < (1 << 16)
    nt_c = t_c // TM
    tiles_per_seq = seq_d // TM
    nb_max = 2 * t // BM + N_EXPERTS
    nb_max = (nb_max + 7) // 8 * 8

    in_cols, gain_idx, gain_scale, ob_rows, oc_rows = _layouts()
    x_c, x_d, x_d_off = x_prompt.reshape(t_c, d), x_sample.reshape(t_d, d), 0
    ones_ctx = jnp.ones((nb, n_ctx, LANES), BF16)
    cvec = jnp.concatenate([c_ctx[None, :], c, jnp.zeros((8 - 1 - nb, d), F32)], axis=0)
    mods = _ada_call(cvec, w_ada, b_ada).reshape(depth, 8, 6, d)
    rope_tabs = _rope_tables(seq_d)

    def mod_map(i):
        return (jnp.where(i < nt_c, 0, 1 + (i - nt_c) // tiles_per_seq), 0, 0)

    new_k, new_v = [], []
    for l in range(depth):
        w_in_p = w_in[l][:, in_cols].astype(BF16)
        gain_row = (qk_gain[l][gain_idx] * gain_scale[:, None]).reshape(1, -1)
        gain_row = jnp.concatenate([gain_row, jnp.ones((1, 512), F32)], axis=1)
        nm = norm_mix[l].reshape(1, d)
        nf = norm_ffn[l].reshape(1, d)

        q_c, k_c, v_c, k32, v32 = _qkv_call(x_c, mods[l], nm, w_in_p, gain_row, tile_off=0, n_tok=t_c,
                                            seq_len=seq_c, row_off=0, per_seq_mod=False, rope_tabs=None, emit_f32=True)
        new_k.append(k32.reshape(batch, seq_c, 8, HEAD_DIM))
        new_v.append(v32.reshape(batch, seq_c, 8, HEAD_DIM))
        q_d, k_d, v_d = _qkv_call(x_d, mods[l], nm, w_in_p, gain_row, tile_off=x_d_off, n_tok=t_d,
                                  seq_len=seq_d, row_off=1, per_seq_mod=True, rope_tabs=rope_tabs, emit_f32=False)
        k_d = k_d.reshape(nb, seq_d, 512)
        v_d = v_d.reshape(nb, seq_d, V_EXT_COLS)

        o_ctx = _ctx_attn_call(sink[l], q_c, k_c, v_c, seq_c)
        ck = cache_ctx_k[:, l].reshape(nb, n_ctx, 512).astype(BF16)
        cv = cache_ctx_v[:, l].reshape(nb, n_ctx, 512).astype(BF16)
        cv = jnp.concatenate([cv[:, :, :384], ones_ctx, cv[:, :, 384:], ones_ctx], axis=2)
        o_b = _dec_b_call(q_d, jnp.concatenate([k_d[:, :, 256:384], ck[:, :, 256:384]], axis=1),
                          jnp.concatenate([v_d[:, :, 256:512], cv[:, :, 256:512]], axis=1))
        o_c = _dec_c_call(sink[l], q_d, k_d, v_d, ck, cv)
        o_a = _dec_a_call(q_d, k_d, v_d, ck, cv, _bias_call(rpb[l]))

        w_route = jnp.concatenate([w_route_group[l], w_route_expert[l],
                                   jnp.zeros((d, LANES - N_GROUPS - N_EXPERTS), F32)], axis=1)
        wr_hi = w_route.astype(BF16)
        wr_lo = (w_route - wr_hi.astype(F32)).astype(BF16)
        x_new, h, logits = _merge_call(
            x_c, x_d, x_d_off, o_ctx, o_b, o_a, o_c, mods[l], nm, nf, w_gate[l].astype(BF16), w_br_b[l][ob_rows].astype(BF16),
            w_br_a[l].astype(BF16), w_br_c[l][oc_rows].astype(BF16), w_out[l].astype(BF16), wr_hi, wr_lo, mod_map)

        route_i, route_g, tab, blk = _route_call(logits, nb_max)
        dest = _slots_call(route_i, tab)[:, :2].reshape(-1)
        xs = _dispatch_call(dest, jnp.zeros((nb_max * BM, d // 2), jnp.uint32), h)
        yb = _expert_call(blk[:, 0], tab[1, N_GROUPS + N_EXPERTS - 1:N_GROUPS + N_EXPERTS], xs,
                          w_exp_gate, w_exp_up, w_exp_down, l)
        outs = _combine_call(dest, x_new, route_g, mods[l], yb, mod_map, nt_c, split=(l == depth - 1))
        if l < depth - 1:
            x_c = x_d = outs[0]
            x_d_off = nt_c

    y_prompt = outs[0].reshape(batch, seq_c, d)
    y_sample = outs[1].reshape(nb, seq_d, d)
    return (y_prompt, y_sample, jnp.stack(new_k, axis=1), jnp.stack(new_v, axis=1))
```

```python
import functools

import jax
import jax.numpy as jnp
import numpy as np
from jax import lax
from jax.experimental import pallas as pl
from jax.experimental.pallas import tpu as pltpu

GRID_W = 64
HEAD_DIM = 64
HA, HB, KVB, HC, KVC = 4, 8, 2, 4, 2
NA_WIN_R, NA_WIN_C = 8, 16
WINDOW = 128
ROPE_THETA = 10000.0
N_GROUPS, EXPERTS_PER_GROUP, N_EXPERTS = 4, 8, 32
EPS = 1e-6
NEG = -1e30
LOG2E = 1.4426950408889634

LANES = 128
V7X_VMEM_LIMIT = 56 * 1024 * 1024

TM = 512
TQ_B = 128
TK_B = 256
TQ_C = 256
ROWS_A = 8
BM = 512
ROW_CHAINS = 2

F32 = jnp.float32
BF16 = jnp.bfloat16


def _cparams(sem, vmem=V7X_VMEM_LIMIT):
    return pltpu.CompilerParams(dimension_semantics=sem, vmem_limit_bytes=vmem)


def _split_bf16(x):
    hi = x.astype(BF16)
    lo = (x - hi.astype(F32)).astype(BF16)
    return hi, lo


def _pack_bf16_pairs(x_bf16):
    n = x_bf16.shape[1] // 2
    lo = lax.bitcast_convert_type(x_bf16[:, :n].astype(F32), jnp.uint32)
    hi = lax.bitcast_convert_type(x_bf16[:, n:].astype(F32), jnp.uint32)
    return (lo >> 16) | hi


def _unpack_bf16_pairs(w_u32):
    lo = lax.bitcast_convert_type(w_u32 << 16, F32)
    hi = lax.bitcast_convert_type(w_u32 & jnp.uint32(0xFFFF0000), F32)
    return jnp.concatenate([lo, hi], axis=1).astype(BF16)


def _dot(a, b):
    return jnp.dot(a, b, preferred_element_type=F32)


def _dot_nt(a, b):
    return lax.dot_general(a, b, (((1,), (1,)), ((), ())), preferred_element_type=F32)


def _half_masks(dtype):
    lane = lax.broadcasted_iota(jnp.int32, (1, LANES), 1)
    lo = (lane < HEAD_DIM)
    return lo, jnp.where(lo, 1.0, 0.0).astype(dtype), jnp.where(lo, 0.0, 1.0).astype(dtype)


def _ada_kernel(c_ref, w_ref, b_ref, o_ref):
    c = c_ref[...]
    s = c * (1.0 / (1.0 + jnp.exp(-c)))
    s_hi, s_lo = _split_bf16(s)
    w_hi, w_lo = _split_bf16(w_ref[...])
    o_ref[...] = _dot(s_hi, w_hi) + _dot(s_lo, w_hi) + _dot(s_hi, w_lo) + b_ref[...]


def _ada_call(cvec, w_ada, b_ada):
    depth, d, n6 = w_ada.shape
    tn = 512
    return pl.pallas_call(
        _ada_kernel,
        grid=(depth, n6 // tn),
        in_specs=[
            pl.BlockSpec((8, d), lambda l, j: (0, 0)),
            pl.BlockSpec((None, d, tn), lambda l, j: (l, 0, j)),
            pl.BlockSpec((None, 1, tn), lambda l, j: (l, 0, j)),
        ],
        out_specs=pl.BlockSpec((None, 8, tn), lambda l, j: (l, 0, j)),
        out_shape=jax.ShapeDtypeStruct((depth, 8, n6), F32),
        compiler_params=_cparams(("arbitrary", "arbitrary")),
        name="ada_modulation",
    )(cvec, w_ada, b_ada.reshape(depth, 1, n6))


N_QCHUNK = 8
N_KCHUNK = 4
ROPE_QCHUNKS = (0, 1, 2, 3, 6, 7)
ROPE_KCHUNKS = (2, 3)
V_EXT_COLS = 6 * LANES


def _modulated_norm(x, gain, scale, shift):
    ms = jnp.mean(x * x, axis=-1, keepdims=True)
    return (x * lax.rsqrt(ms + EPS) * gain) * (1.0 + scale) + shift


def _qkv_kernel(*refs, rope, emit_f32):
    x_ref, mod_ref, nm_ref, w_ref, gain_ref = refs[:5]
    pos = 5
    if rope:
        cos_ref, sin_ref = refs[pos:pos + 2]
        pos += 2
    q_ref, k_ref, v_ref = refs[pos:pos + 3]
    pos += 3
    if emit_f32:
        k32_ref, v32_ref = refs[pos:pos + 2]

    x = x_ref[...]
    u = _modulated_norm(x, nm_ref[...], mod_ref[0, 1:2, :], mod_ref[0, 0:1, :])
    y = _dot(u.astype(BF16), w_ref[...])

    lane = lax.broadcasted_iota(jnp.int32, (1, LANES), 1)
    lo = lane < HEAD_DIM
    first = (lane % 32) < 16
    if rope:
        cos_t = cos_ref[...]
        sin_t = sin_ref[...]

    def norm_chunk(j, use_rope):
        yj = y[:, j * LANES:(j + 1) * LANES]
        sq = yj * yj
        s_lo = jnp.sum(jnp.where(lo, sq, 0.0), axis=-1, keepdims=True)
        s_hi = jnp.sum(jnp.where(lo, 0.0, sq), axis=-1, keepdims=True)
        inv = lax.rsqrt(jnp.where(lo, s_lo, s_hi) * (1.0 / HEAD_DIM) + EPS)
        yj = (yj * inv) * gain_ref[:, j * LANES:(j + 1) * LANES]
        if use_rope:
            partner = jnp.where(first, pltpu.roll(yj, LANES - 16, axis=1), pltpu.roll(yj, 16, axis=1))
            yj = yj * cos_t + partner * sin_t
        return yj

    for j in range(N_QCHUNK):
        yj = norm_chunk(j, rope and j in ROPE_QCHUNKS)
        q_ref[:, j * LANES:(j + 1) * LANES] = yj.astype(BF16)
    for j in range(N_KCHUNK):
        yj = norm_chunk(N_QCHUNK + j, rope and j in ROPE_KCHUNKS)
        k_ref[:, j * LANES:(j + 1) * LANES] = yj.astype(BF16)
        if emit_f32:
            k32_ref[:, j * LANES:(j + 1) * LANES] = yj
    v = y[:, (N_QCHUNK + N_KCHUNK) * LANES:]
    if rope:
        ones = jnp.ones((v.shape[0], LANES), BF16)
        v_ref[:, :3 * LANES] = v[:, :3 * LANES].astype(BF16)
        v_ref[:, 3 * LANES:4 * LANES] = ones
        v_ref[:, 4 * LANES:5 * LANES] = v[:, 3 * LANES:].astype(BF16)
        v_ref[:, 5 * LANES:] = ones
    else:
        v_ref[...] = v.astype(BF16)
    if emit_f32:
        v32_ref[...] = v


def _qkv_call(x_all, mods, norm_mix_l, w_in_p, gain_row, *, tile_off, n_tok, seq_len, row_off,
              per_seq_mod, rope_tabs, emit_f32):
    d = x_all.shape[1]
    nt = n_tok // TM
    tiles_per_seq = max(seq_len // TM, 1)
    rope = rope_tabs is not None

    def mod_map(i):
        return (row_off + i // tiles_per_seq, 0, 0) if per_seq_mod else (row_off, 0, 0)

    in_specs = [
        pl.BlockSpec((TM, d), lambda i: (i + tile_off, 0)),
        pl.BlockSpec((1, 6, d), mod_map),
        pl.BlockSpec((1, d), lambda i: (0, 0)),
        pl.BlockSpec(w_in_p.shape, lambda i: (0, 0)),
        pl.BlockSpec((1, 2048), lambda i: (0, 0)),
    ]
    args = [x_all, mods, norm_mix_l, w_in_p, gain_row]
    if rope:
        in_specs += [pl.BlockSpec((TM, LANES), lambda i: (i % tiles_per_seq, 0))] * 2
        args += list(rope_tabs)
    v_cols = V_EXT_COLS if rope else 512
    out_shape = [jax.ShapeDtypeStruct((n_tok, 1024), BF16),
                 jax.ShapeDtypeStruct((n_tok, 512), BF16),
                 jax.ShapeDtypeStruct((n_tok, v_cols), BF16)]
    out_specs = [pl.BlockSpec((TM, 1024), lambda i: (i, 0)),
                 pl.BlockSpec((TM, 512), lambda i: (i, 0)),
                 pl.BlockSpec((TM, v_cols), lambda i: (i, 0))]
    if emit_f32:
        out_shape += [jax.ShapeDtypeStruct((n_tok, 512), F32)] * 2
        out_specs += [pl.BlockSpec((TM, 512), lambda i: (i, 0))] * 2
    return pl.pallas_call(
        functools.partial(_qkv_kernel, rope=rope, emit_f32=emit_f32),
        grid=(nt,),
        in_specs=in_specs,
        out_specs=out_specs,
        out_shape=out_shape,
        compiler_params=_cparams(("parallel",)),
        name="qkv_rope" if rope else "qkv_ctx",
    )(*args)


def _softmax_rows(s, sink_col=None):
    m = jnp.max(s, axis=-1, keepdims=True)
    if sink_col is not None:
        m = jnp.maximum(m, sink_col)
    p = jnp.exp2(s - m)
    l = jnp.sum(p, axis=-1, keepdims=True)
    if sink_col is not None:
        l = l + jnp.exp2(sink_col - m)
    return p, l


def _ctx_attn_kernel(sink_ref, q_ref, k_ref, v_ref, o_ref):
    s_len = q_ref.shape[0]
    _, m_lo, m_hi = _half_masks(BF16)
    lo = lax.broadcasted_iota(jnp.int32, (1, LANES), 1) < HEAD_DIM

    def group(q_chunks, kc, sinks=None):
        n = len(q_chunks)
        k = k_ref[:, kc * LANES:(kc + 1) * LANES]
        v = v_ref[:, kc * LANES:(kc + 1) * LANES]
        qs = jnp.concatenate([q_ref[:, c * LANES:(c + 1) * LANES] * m for m in (m_lo, m_hi) for c in q_chunks],
                             axis=0)
        s = _dot_nt(qs, k)
        sink_col = None
        if sinks is not None:
            row = lax.broadcasted_iota(jnp.int32, (2 * n * s_len, 1), 0)
            sink_col = jnp.zeros((2 * n * s_len, 1), F32)
            for idx, sv in enumerate(sinks):
                sink_col = jnp.where((row >= idx * s_len) & (row < (idx + 1) * s_len), sv, sink_col)
        p, l = _softmax_rows(s, sink_col)
        o = _dot(p.astype(BF16), v) / l
        for idx, c in enumerate(q_chunks):
            o_c = jnp.where(lo, o[idx * s_len:(idx + 1) * s_len], o[(n + idx) * s_len:(n + idx + 1) * s_len])
            o_ref[:, c * LANES:(c + 1) * LANES] = o_c.astype(BF16)

    group([0, 1, 2, 3], 2)
    group([4], 0)
    group([5], 1)
    group([6, 7], 3, sinks=[sink_ref[h] * LOG2E for h in range(HC)])


def _ctx_attn_call(sink_l, q, k, v, s_len):
    batch = q.shape[0] // s_len
    return pl.pallas_call(
        _ctx_attn_kernel,
        grid_spec=pltpu.PrefetchScalarGridSpec(
            num_scalar_prefetch=1,
            grid=(batch,),
            in_specs=[pl.BlockSpec((s_len, 1024), lambda b, s: (b, 0)),
                      pl.BlockSpec((s_len, 512), lambda b, s: (b, 0)),
                      pl.BlockSpec((s_len, 512), lambda b, s: (b, 0))],
            out_specs=pl.BlockSpec((s_len, 1024), lambda b, s: (b, 0)),
        ),
        out_shape=jax.ShapeDtypeStruct((q.shape[0], 1024), BF16),
        compiler_params=_cparams(("parallel",)),
        name="ctx_attention",
    )(sink_l, q, k, v)


def _dec_b_kernel(q_ref, k_ref, v_ref, o_ref):
    n_keys = k_ref.shape[0]
    n_chunks = n_keys // TK_B
    _, m_lo, m_hi = _half_masks(BF16)
    lo = lax.broadcasted_iota(jnp.int32, (1, LANES), 1) < HEAD_DIM

    def stacked_q(msk):
        return jnp.concatenate([q_ref[:, c * LANES:(c + 1) * LANES] * msk for c in range(4)], axis=0)

    def write(outs):
        for c in range(4):
            o_c = jnp.where(lo, outs[0][c * TQ_B:(c + 1) * TQ_B], outs[1][c * TQ_B:(c + 1) * TQ_B])
            o_ref[:, c * LANES:(c + 1) * LANES] = o_c.astype(BF16)

    outs, sums = [], []
    for msk in (m_lo, m_hi):
        qs = stacked_q(msk)
        s = _dot_nt(qs, k_ref[0:TK_B, :])
        m = jnp.max(s, axis=-1, keepdims=True)
        acc = _dot(jnp.exp2(s - m).astype(BF16), v_ref[0:TK_B, :])
        for t in range(1, n_chunks):
            s = _dot_nt(qs, k_ref[t * TK_B:(t + 1) * TK_B, :])
            acc = acc + _dot(jnp.exp2(s - m).astype(BF16), v_ref[t * TK_B:(t + 1) * TK_B, :])
        outs.append(acc[:, :LANES] / acc[:, LANES:])
        sums.append(jnp.sum(acc * 0.0))
    write(outs)

    @pl.when(jnp.logical_not(sums[0] + sums[1] == 0.0))
    def _():
        exact = []
        for msk in (m_lo, m_hi):
            s = _dot_nt(stacked_q(msk), k_ref[...])
            acc = _dot(jnp.exp2(s - jnp.max(s, axis=-1, keepdims=True)).astype(BF16), v_ref[...])
            exact.append(acc[:, :LANES] / acc[:, LANES:])
        write(exact)


def _dec_b_call(q, k_all, v_all):
    nb, n_keys, _ = k_all.shape
    seq = q.shape[0] // nb
    nq = seq // TQ_B
    return pl.pallas_call(
        _dec_b_kernel,
        grid=(nb, nq),
        in_specs=[pl.BlockSpec((TQ_B, 512), lambda b, i: (b * nq + i, 0)),
                  pl.BlockSpec((None, n_keys, LANES), lambda b, i: (b, 0, 0)),
                  pl.BlockSpec((None, n_keys, 2 * LANES), lambda b, i: (b, 0, 0))],
        out_specs=pl.BlockSpec((TQ_B, 512), lambda b, i: (b * nq + i, 0)),
        out_shape=jax.ShapeDtypeStruct((q.shape[0], 512), BF16),
        compiler_params=_cparams(("parallel", "parallel")),
        name="dec_dense_attention",
    )(q, k_all, v_all)


def _dec_c_kernel(sink_ref, q_ref, k_ref, v_ref, ck_ref, cv_ref, o_ref):
    qi = pl.program_id(1)
    seq = k_ref.shape[0]
    win = TQ_C + 2 * WINDOW
    _, m_lo, m_hi = _half_masks(BF16)
    lo = lax.broadcasted_iota(jnp.int32, (1, LANES), 1) < HEAD_DIM
    start = jnp.clip(qi * TQ_C - WINDOW, 0, seq - win)
    start = pl.multiple_of(start, WINDOW)
    k_win = k_ref[pl.ds(start, win), :]
    v_win = v_ref[pl.ds(start, win), :]
    k_ctx = ck_ref[...]
    v_ctx = cv_ref[...]
    row = lax.broadcasted_iota(jnp.int32, (2 * TQ_C, win), 0)
    col = lax.broadcasted_iota(jnp.int32, (2 * TQ_C, win), 1)
    qpos = qi * TQ_C + jnp.where(row >= TQ_C, row - TQ_C, row)
    ok = jnp.abs(qpos - (start + col)) <= WINDOW
    row1 = lax.broadcasted_iota(jnp.int32, (2 * TQ_C, 1), 0)
    def attend(exact_shift):
        outs, finite = [], []
        for kvh, msk in enumerate((m_lo, m_hi)):
            qs = jnp.concatenate([q_ref[:, c * LANES:(c + 1) * LANES] * msk for c in range(2)], axis=0)
            s_loc = jnp.where(ok, _dot_nt(qs, k_win), NEG)
            s_ctx = _dot_nt(qs, k_ctx)
            sink_col = jnp.where(row1 < TQ_C, sink_ref[2 * kvh] * LOG2E, sink_ref[2 * kvh + 1] * LOG2E)
            m = sink_col
            if exact_shift:
                m = jnp.maximum(jnp.maximum(jnp.max(s_loc, axis=-1, keepdims=True),
                                            jnp.max(s_ctx, axis=-1, keepdims=True)), sink_col)
            acc = (_dot(jnp.exp2(s_loc - m).astype(BF16), v_win)
                   + _dot(jnp.exp2(s_ctx - m).astype(BF16), v_ctx))
            outs.append(acc[:, :LANES] / (acc[:, LANES:] + jnp.exp2(sink_col - m)))
            finite.append(jnp.sum(acc * 0.0))
        for c in range(2):
            o_c = jnp.where(lo, outs[0][c * TQ_C:(c + 1) * TQ_C], outs[1][c * TQ_C:(c + 1) * TQ_C])
            o_ref[:, c * LANES:(c + 1) * LANES] = o_c.astype(BF16)
        return finite[0] + finite[1]

    check = attend(exact_shift=False)

    @pl.when(jnp.logical_not(check == 0.0))
    def _():
        attend(exact_shift=True)


def _dec_c_call(sink_l, q, k_d, v_d, ck, cv):
    nb, seq, _ = k_d.shape
    n_ctx = ck.shape[1]
    nq = seq // TQ_C
    return pl.pallas_call(
        _dec_c_kernel,
        grid_spec=pltpu.PrefetchScalarGridSpec(
            num_scalar_prefetch=1,
            grid=(nb, nq),
            in_specs=[pl.BlockSpec((TQ_C, 256), lambda b, i, s: (b * nq + i, 3)),
                      pl.BlockSpec((None, seq, LANES), lambda b, i, s: (b, 0, 3)),
                      pl.BlockSpec((None, seq, 2 * LANES), lambda b, i, s: (b, 0, 2)),
                      pl.BlockSpec((None, n_ctx, LANES), lambda b, i, s: (b, 0, 3)),
                      pl.BlockSpec((None, n_ctx, 2 * LANES), lambda b, i, s: (b, 0, 2))],
            out_specs=pl.BlockSpec((TQ_C, 256), lambda b, i, s: (b * nq + i, 0)),
        ),
        out_shape=jax.ShapeDtypeStruct((q.shape[0], 256), BF16),
        compiler_params=_cparams(("parallel", "parallel")),
        name="dec_window_attention",
    )(sink_l, q, k_d, v_d, ck, cv)


N_DR = 2 * NA_WIN_R - 2


def _bias_kernel(rpb_ref, o_ref):
    h = pl.program_id(0)
    d = pl.program_id(1)
    n_dc = 2 * NA_WIN_C - 1
    qc = lax.broadcasted_iota(jnp.int32, (GRID_W, LANES), 0)
    kk = lax.broadcasted_iota(jnp.int32, (GRID_W, LANES), 1)
    half = kk >= GRID_W
    kc = jnp.where(half, kk - GRID_W, kk)
    delta = jnp.clip(kc - qc, -(NA_WIN_C - 1), NA_WIN_C - 1) + (NA_WIN_C - 1)
    col_start = jnp.clip(qc - NA_WIN_C // 2, 0, GRID_W - NA_WIN_C)
    ok = (kc >= col_start) & (kc < col_start + NA_WIN_C)
    acc = jnp.zeros((GRID_W, LANES), F32)
    base = (h * (2 * NA_WIN_R - 1) + d) * n_dc
    for dc in range(n_dc):
        val = jnp.where(half, rpb_ref[base + n_dc + dc] * LOG2E, rpb_ref[base + dc] * LOG2E)
        acc = jnp.where(delta == dc, val, acc)
    o_ref[...] = jnp.where(ok, acc, NEG)


def _bias_call(rpb_l):
    return pl.pallas_call(
        _bias_kernel,
        grid_spec=pltpu.PrefetchScalarGridSpec(
            num_scalar_prefetch=1,
            grid=(HA, N_DR),
            in_specs=[],
            out_specs=pl.BlockSpec((None, None, GRID_W, LANES), lambda h, d, r: (h, d, 0, 0)),
        ),
        out_shape=jax.ShapeDtypeStruct((HA, N_DR, GRID_W, LANES), F32),
        compiler_params=_cparams(("arbitrary", "arbitrary")),
        name="na_bias_table",
    )(rpb_l.reshape(-1))


def _dec_a_kernel(q_ref, k_ref, v_ref, ck_ref, cv_ref, bias_ref, o_ref):
    rb = pl.program_id(2)
    rows = k_ref.shape[0] // GRID_W
    n_loc = NA_WIN_R * GRID_W
    _, m_lo, m_hi = _half_masks(BF16)
    lo = lax.broadcasted_iota(jnp.int32, (1, LANES), 1) < HEAD_DIM
    qs = [jnp.concatenate([q_ref[rr * GRID_W:(rr + 1) * GRID_W, :] * msk for msk in (m_lo, m_hi)], axis=0)
          for rr in range(ROWS_A)]
    s_ctx = _dot_nt(jnp.concatenate(qs, axis=0), ck_ref[...])
    s_loc, v_win = [], []
    for rr in range(ROWS_A):
        r = rb * ROWS_A + rr
        start = jnp.clip(r - NA_WIN_R // 2, 0, rows - NA_WIN_R)
        d0 = start - r + (NA_WIN_R - 1)
        off = pl.multiple_of(start * GRID_W, GRID_W)
        bias = jnp.concatenate(
            [jnp.concatenate([bias_ref[hh, d0 + 2 * i] for i in range(NA_WIN_R // 2)], axis=1) for hh in range(2)],
            axis=0)
        s_loc.append(_dot_nt(qs[rr], k_ref[pl.ds(off, n_loc), :]) + bias)
        v_win.append(v_ref[pl.ds(off, n_loc), :])
    m = jnp.maximum(jnp.concatenate([jnp.max(s, axis=-1, keepdims=True) for s in s_loc], axis=0),
                    jnp.max(s_ctx, axis=-1, keepdims=True))
    p_ctx = jnp.exp2(s_ctx - m)
    l_ctx = jnp.sum(p_ctx, axis=-1, keepdims=True)
    o_ctx = _dot(p_ctx.astype(BF16), cv_ref[...])
    for rr in range(ROWS_A):
        sl = slice(rr * 2 * GRID_W, (rr + 1) * 2 * GRID_W)
        p_loc = jnp.exp2(s_loc[rr] - m[sl])
        l = jnp.sum(p_loc, axis=-1, keepdims=True) + l_ctx[sl]
        o = (_dot(p_loc.astype(BF16), v_win[rr]) + o_ctx[sl]) / l
        o_c = jnp.where(lo, o[:GRID_W], o[GRID_W:])
        o_ref[rr * GRID_W:(rr + 1) * GRID_W, :] = o_c.astype(BF16)


def _dec_a_call(q, k_d, v_d, ck, cv, bias_tab):
    nb, seq, _ = k_d.shape
    n_ctx = ck.shape[1]
    tq = ROWS_A * GRID_W
    nq = seq // tq
    return pl.pallas_call(
        _dec_a_kernel,
        grid=(2, nb, nq),
        in_specs=[pl.BlockSpec((tq, LANES), lambda p, b, i: (b * nq + i, 4 + p)),
                  pl.BlockSpec((None, seq, LANES), lambda p, b, i: (b, 0, p)),
                  pl.BlockSpec((None, seq, LANES), lambda p, b, i: (b, 0, p)),
                  pl.BlockSpec((None, n_ctx, LANES), lambda p, b, i: (b, 0, p)),
                  pl.BlockSpec((None, n_ctx, LANES), lambda p, b, i: (b, 0, p)),
                  pl.BlockSpec((2, N_DR, GRID_W, LANES), lambda p, b, i: (p, 0, 0, 0))],
        out_specs=pl.BlockSpec((tq, LANES), lambda p, b, i: (b * nq + i, p)),
        out_shape=jax.ShapeDtypeStruct((q.shape[0], 2 * LANES), BF16),
        compiler_params=_cparams(("parallel", "parallel", "parallel")),
        name="dec_neighbourhood_attention",
    )(q, k_d, v_d, ck, cv, bias_tab)


def _merge_kernel(xc_ref, xd_ref, oc_ref, ob_ref, oa_ref, ow_ref, mod_ref, nm_ref, nf_ref, wg_ref, wbb_ref, wba_ref,
                  wbc_ref, wo_ref, wr_hi_ref, wr_lo_ref, xo_ref, h_ref, lg_ref, *, nt_c):
    d = xc_ref.shape[1]
    tm = xc_ref.shape[0]
    is_ctx = pl.program_id(0) < nt_c
    for c in range(ROW_CHAINS):
        rows = slice(c * tm // ROW_CHAINS, (c + 1) * tm // ROW_CHAINS)
        o_b = jnp.where(is_ctx, oc_ref[rows, :512], ob_ref[rows, :])
        o_a = jnp.where(is_ctx, oc_ref[rows, 512:768], oa_ref[rows, :])
        o_c = jnp.where(is_ctx, oc_ref[rows, 768:], ow_ref[rows, :])
        x = jnp.where(is_ctx, xc_ref[rows, :], xd_ref[rows, :])
        u = _modulated_norm(x, nm_ref[...], mod_ref[0, 1:2, :], mod_ref[0, 0:1, :])
        gates = _dot(u.astype(BF16), wg_ref[...])
        gates = 1.0 / (1.0 + jnp.exp(-gates))
        m = (gates[:, :d] * _dot(o_a, wba_ref[...])
             + gates[:, d:2 * d] * _dot(o_b, wbb_ref[...])
             + gates[:, 2 * d:] * _dot(o_c, wbc_ref[...]))
        x_new = x + mod_ref[0, 2:3, :] * _dot(m.astype(BF16), wo_ref[...])
        xo_ref[rows, :] = x_new
        h = _modulated_norm(x_new, nf_ref[...], mod_ref[0, 4:5, :], mod_ref[0, 3:4, :])
        h_hi, h_lo = _split_bf16(h)
        h_ref[rows, :] = _pack_bf16_pairs(h_hi)
        w_hi = wr_hi_ref[...]
        lg_ref[rows, :] = _dot(h_hi, w_hi) + _dot(h_lo, w_hi) + _dot(h_hi, wr_lo_ref[...])


def _merge_call(x_c, x_d, x_d_tile_off, o_ctx, o_b, o_a, o_c, mods, nm, nf, wg, wbb, wba, wbc, wo, wr_hi, wr_lo,
                mod_map):
    d = x_c.shape[1]
    nt_c = o_ctx.shape[0] // TM
    t = o_ctx.shape[0] + o_b.shape[0]
    const = lambda i: (0, 0)
    full = lambda a: pl.BlockSpec(a.shape, const)
    ctx_map = lambda i: (jnp.minimum(i, nt_c - 1), 0)
    dec_map = lambda i: (jnp.maximum(i - nt_c, 0), 0)
    return pl.pallas_call(
        functools.partial(_merge_kernel, nt_c=nt_c),
        grid=(t // TM,),
        in_specs=[pl.BlockSpec((TM, d), ctx_map),
                  pl.BlockSpec((TM, d), lambda i: (jnp.maximum(i - nt_c, 0) + x_d_tile_off, 0)),
                  pl.BlockSpec((TM, 1024), ctx_map),
                  pl.BlockSpec((TM, 512), dec_map),
                  pl.BlockSpec((TM, 256), dec_map),
                  pl.BlockSpec((TM, 256), dec_map),
                  pl.BlockSpec((1, 6, d), mod_map),
                  full(nm), full(nf), full(wg), full(wbb), full(wba), full(wbc), full(wo),
                  full(wr_hi), full(wr_lo)],
        out_specs=[pl.BlockSpec((TM, d), lambda i: (i, 0)),
                   pl.BlockSpec((TM, d // 2), lambda i: (i, 0)),
                   pl.BlockSpec((TM, LANES), lambda i: (i, 0))],
        out_shape=[jax.ShapeDtypeStruct((t, d), F32),
                   jax.ShapeDtypeStruct((t, d // 2), jnp.uint32),
                   jax.ShapeDtypeStruct((t, LANES), F32)],
        compiler_params=_cparams(("parallel",)),
        name="merge_branches",
    )(x_c, x_d, o_ctx, o_b, o_a, o_c, mods, nm, nf, wg, wbb, wba, wbc, wo, wr_hi, wr_lo)


def _route_kernel(lg_ref, ri_ref, rg_ref, tab_ref, blk_ref, cnt_sc, *, nb_max):
    i = pl.program_id(0)
    n = pl.num_programs(0)
    tm = lg_ref.shape[0]

    @pl.when(i == 0)
    def _():
        cnt_sc[...] = jnp.zeros(cnt_sc.shape, F32)

    lg = lg_ref[...]
    lane = lax.broadcasted_iota(jnp.int32, (tm, LANES), 1).astype(F32)
    big = float(LANES)
    neg_inf = -jnp.inf
    gl = jnp.where(lane < N_GROUPS, lg, neg_inf)
    gmax = jnp.max(gl, axis=-1, keepdims=True)
    g_sel = jnp.min(jnp.where(gl == gmax, lane, big), axis=-1, keepdims=True)
    g_w = 1.0 / jnp.sum(jnp.exp(gl - gmax), axis=-1, keepdims=True)
    e_lo = N_GROUPS + EXPERTS_PER_GROUP * g_sel
    el = jnp.where((lane >= e_lo) & (lane < e_lo + EXPERTS_PER_GROUP), lg, neg_inf)
    m1 = jnp.max(el, axis=-1, keepdims=True)
    i1 = jnp.min(jnp.where(el == m1, lane, big), axis=-1, keepdims=True)
    el2 = jnp.where(lane == i1, neg_inf, el)
    m2 = jnp.max(el2, axis=-1, keepdims=True)
    i2 = jnp.min(jnp.where(el2 == m2, lane, big), axis=-1, keepdims=True)
    e2x = jnp.exp(m2 - m1)
    den = 1.0 + e2x
    gate1 = g_w * (1.0 / den)
    gate2 = g_w * (e2x / den)
    hit1 = lane == i1
    hit2 = lane == i2
    onehot = jnp.where(hit1 | hit2, 1.0, 0.0).astype(BF16)
    rr = lax.broadcasted_iota(jnp.int32, (tm, tm), 0)
    cc = lax.broadcasted_iota(jnp.int32, (tm, tm), 1)
    tri = jnp.where(cc <= rr, 1.0, 0.0).astype(BF16)
    pref = _dot(tri, onehot) + cnt_sc[...]
    rank1 = jnp.sum(jnp.where(hit1, pref, 0.0), axis=-1, keepdims=True) - 1.0
    rank2 = jnp.sum(jnp.where(hit2, pref, 0.0), axis=-1, keepdims=True) - 1.0
    cnt_sc[...] = pref[tm - 1:tm, :]
    e1 = (i1 - N_GROUPS).astype(jnp.int32)
    e2 = (i2 - N_GROUPS).astype(jnp.int32)
    pk1 = (e1 << 16) | rank1.astype(jnp.int32)
    pk2 = (e2 << 16) | rank2.astype(jnp.int32)
    lane_i = lax.broadcasted_iota(jnp.int32, (tm, LANES), 1)
    ri_ref[...] = jnp.where(lane_i == 0, pk1, jnp.where(lane_i == 1, pk2, 0))
    rg_ref[...] = jnp.where(lane_i == 0, gate1, jnp.where(lane_i == 1, gate2, 0.0))

    @pl.when(i == n - 1)
    def _():
        cnt = cnt_sc[...]
        nblk = jnp.floor((cnt + (BM - 1.0)) * (1.0 / BM))
        l8 = lax.broadcasted_iota(jnp.int32, (8, LANES), 1)
        is_exp = (l8 >= N_GROUPS) & (l8 < N_GROUPS + N_EXPERTS)
        nblk8 = jnp.where(is_exp, jnp.broadcast_to(nblk, (8, LANES)), 0.0)
        r2 = lax.broadcasted_iota(jnp.int32, (LANES, LANES), 0)
        c2 = lax.broadcasted_iota(jnp.int32, (LANES, LANES), 1)
        upper = jnp.where(r2 <= c2, 1.0, 0.0).astype(BF16)
        hi, lo_ = _split_bf16(nblk8)
        blk_end = _dot(hi, upper) + _dot(lo_, upper)
        blk_start = blk_end - nblk8
        row8 = lax.broadcasted_iota(jnp.int32, (8, LANES), 0)
        tab = jnp.where(row8 == 0, blk_start * BM, jnp.where(row8 == 1, blk_end, cnt))
        tab_ref[...] = tab.astype(jnp.int32)
        bi = lax.broadcasted_iota(jnp.int32, (nb_max, LANES), 0).astype(F32)
        ends = blk_end[0:1, :]
        lb = lax.broadcasted_iota(jnp.int32, (nb_max, LANES), 1)
        done = jnp.where((lb >= N_GROUPS) & (lb < N_GROUPS + N_EXPERTS) & (ends <= bi), 1.0, 0.0)
        be = jnp.minimum(jnp.sum(done, axis=-1, keepdims=True), N_EXPERTS - 1.0)
        blk_ref[...] = jnp.broadcast_to(be, (nb_max, LANES)).astype(jnp.int32)


def _route_call(logits, nb_max):
    t = logits.shape[0]
    return pl.pallas_call(
        functools.partial(_route_kernel, nb_max=nb_max),
        grid=(t // TM,),
        in_specs=[pl.BlockSpec((TM, LANES), lambda i: (i, 0))],
        out_specs=[pl.BlockSpec((TM, LANES), lambda i: (i, 0)),
                   pl.BlockSpec((TM, LANES), lambda i: (i, 0)),
                   pl.BlockSpec((8, LANES), lambda i: (0, 0)),
                   pl.BlockSpec((nb_max, LANES), lambda i: (0, 0))],
        out_shape=[jax.ShapeDtypeStruct((t, LANES), jnp.int32),
                   jax.ShapeDtypeStruct((t, LANES), F32),
                   jax.ShapeDtypeStruct((8, LANES), jnp.int32),
                   jax.ShapeDtypeStruct((nb_max, LANES), jnp.int32)],
        scratch_shapes=[pltpu.VMEM((1, LANES), F32)],
        compiler_params=_cparams(("arbitrary",)),
        name="moe_route",
    )(logits)


def _slots_kernel(ri_ref, tab_ref, o_ref):
    tm = ri_ref.shape[0]
    lane = lax.broadcasted_iota(jnp.int32, (tm, LANES), 1)
    starts = tab_ref[0:1, :].astype(F32)
    pk = ri_ref[...]
    out = jnp.zeros((tm, LANES), jnp.int32)
    for k in range(2):
        pkk = pk[:, k:k + 1]
        first = jnp.sum(jnp.where(lane == (pkk >> 16) + N_GROUPS, starts, 0.0), axis=-1, keepdims=True)
        out = jnp.where(lane == k, first.astype(jnp.int32) + (pkk & 0xFFFF), out)
    o_ref[...] = out


def _slots_call(route_i, tab):
    t = route_i.shape[0]
    return pl.pallas_call(
        _slots_kernel,
        grid=(t // TM,),
        in_specs=[pl.BlockSpec((TM, LANES), lambda i: (i, 0)),
                  pl.BlockSpec((8, LANES), lambda i: (0, 0))],
        out_specs=pl.BlockSpec((TM, LANES), lambda i: (i, 0)),
        out_shape=jax.ShapeDtypeStruct((t, LANES), jnp.int32),
        compiler_params=_cparams(("parallel",)),
        name="moe_slots",
    )(route_i, tab)


N_DISPATCH_BUF = 3


def _dispatch_kernel(dest_ref, xs_in_ref, h_ref, xs_ref, buf, fsem, ssem):
    del xs_in_ref
    i = pl.program_id(0)
    n = pl.num_programs(0)
    tm = buf.shape[1]

    def fetch(tile, slot):
        return pltpu.make_async_copy(h_ref.at[pl.ds(tile * tm, tm), :], buf.at[slot], fsem.at[slot])

    def scatter_wait(slot):
        for _ in range(2):
            pltpu.make_async_copy(buf.at[slot], xs_ref.at[pl.ds(0, tm), :], ssem.at[slot]).wait()

    @pl.when(i == 0)
    def _():
        fetch(0, 0).start()

        @pl.when(n > 1)
        def _():
            fetch(1, 1).start()

    slot = i % N_DISPATCH_BUF
    fetch(i, slot).wait()
    base = 2 * i * tm
    for r in range(tm):
        for k in range(2):
            dst = dest_ref[base + (2 * r + k)]
            pltpu.make_async_copy(buf.at[slot, pl.ds(r, 1), :], xs_ref.at[pl.ds(dst, 1), :],
                                  ssem.at[slot]).start(priority=k)

    @pl.when(i >= 1)
    def _():
        scatter_wait((i - 1) % N_DISPATCH_BUF)

    @pl.when(i + 2 < n)
    def _():
        fetch(i + 2, (i + 2) % N_DISPATCH_BUF).start()

    @pl.when(i == n - 1)
    def _():
        scatter_wait(slot)


def _dispatch_call(dest, xs_init, h):
    t, dw = h.shape
    return pl.pallas_call(
        _dispatch_kernel,
        grid_spec=pltpu.PrefetchScalarGridSpec(
            num_scalar_prefetch=1,
            grid=(t // TM,),
            in_specs=[pl.BlockSpec(memory_space=pl.ANY),
                      pl.BlockSpec(memory_space=pl.ANY)],
            out_specs=pl.BlockSpec(memory_space=pl.ANY),
            scratch_shapes=[pltpu.VMEM((N_DISPATCH_BUF, TM, dw), h.dtype),
                            pltpu.SemaphoreType.DMA((N_DISPATCH_BUF,)),
                            pltpu.SemaphoreType.DMA((N_DISPATCH_BUF,))],
        ),
        out_shape=jax.ShapeDtypeStruct(xs_init.shape, xs_init.dtype),
        input_output_aliases={1: 0},
        compiler_params=_cparams(("arbitrary",)),
        name="moe_dispatch",
    )(dest, xs_init, h)


def _expert_kernel(blk_ref, nused_ref, xs_ref, wg_hbm, wu_hbm, wd_hbm, y_ref, wg_f32, wu_f32, wd_f32,
                   wg_sc, wu_sc, wd_sc, n_loaded, sem, *, layer):
    i = pl.program_id(0)
    n_used = nused_ref[0]
    last = blk_ref.shape[0] - 1
    e = blk_ref[i]

    def fetch(expert, slot):
        return [pltpu.make_async_copy(src.at[layer, expert], dst.at[slot], sem.at[slot])
                for src, dst in ((wg_hbm, wg_f32), (wu_hbm, wu_f32), (wd_hbm, wd_f32))]

    @pl.when((i == 0) & (n_used > 0))
    def _():
        n_loaded[0] = 0
        for cp in fetch(e, 0):
            cp.start()

    @pl.when((i < n_used) & ((i == 0) | (e != blk_ref[jnp.maximum(i - 1, 0)])))
    def _():
        slot = n_loaded[0] % 2
        for cp in fetch(e, slot):
            cp.wait()
        nxt = lax.while_loop(lambda j: (j < n_used) & (blk_ref[jnp.minimum(j, last)] == e), lambda j: j + 1, i + 1)

        @pl.when(nxt < n_used)
        def _():
            for cp in fetch(blk_ref[jnp.minimum(nxt, last)], 1 - slot):
                cp.start()

        wg_sc[...] = wg_f32[slot].astype(BF16)
        wu_sc[...] = wu_f32[slot].astype(BF16)
        wd_sc[...] = wd_f32[slot].astype(BF16)
        n_loaded[0] = n_loaded[0] + 1

    @pl.when(i < n_used)
    def _():
        x = _unpack_bf16_pairs(xs_ref[...])
        g = _dot(x, wg_sc[...])
        u = _dot(x, wu_sc[...])
        hdn = (g * (1.0 / (1.0 + jnp.exp(-g)))) * u
        y_ref[...] = _dot(hdn.astype(BF16), wd_sc[...])

    @pl.when(i >= nused_ref[0])
    def _():
        y_ref[...] = jnp.zeros(y_ref.shape, F32)


def _expert_call(blk_expert, n_used, xs, wg, wu, wd, layer):
    p, dw = xs.shape
    d, ff = wg.shape[-2:]
    return pl.pallas_call(
        functools.partial(_expert_kernel, layer=layer),
        grid_spec=pltpu.PrefetchScalarGridSpec(
            num_scalar_prefetch=2,
            grid=(p // BM,),
            in_specs=[pl.BlockSpec((BM, dw), lambda i, b, n: (i, 0)),
                      pl.BlockSpec(memory_space=pl.ANY),
                      pl.BlockSpec(memory_space=pl.ANY),
                      pl.BlockSpec(memory_space=pl.ANY)],
            out_specs=pl.BlockSpec((BM, d), lambda i, b, n: (i, 0)),
            scratch_shapes=[pltpu.VMEM((2, d, ff), F32), pltpu.VMEM((2, d, ff), F32), pltpu.VMEM((2, ff, d), F32),
                            pltpu.VMEM((d, ff), BF16), pltpu.VMEM((d, ff), BF16), pltpu.VMEM((ff, d), BF16),
                            pltpu.SMEM((1,), jnp.int32), pltpu.SemaphoreType.DMA((2,))],
        ),
        out_shape=jax.ShapeDtypeStruct((p, d), F32),
        compiler_params=_cparams(("arbitrary",)),
        name="moe_experts",
    )(blk_expert, n_used, xs, wg, wu, wd)


def _combine_kernel(dest_ref, x_ref, rg_ref, mod_ref, y_ref, *rest, nt_c):
    out_refs, (buf, sem) = rest[:-2], rest[-2:]
    j = pl.program_id(0)
    n_tiles = pl.num_programs(0) - 1
    tm = x_ref.shape[0]

    @pl.when(j < n_tiles)
    def _():
        slot = j % 2
        base = 2 * j * tm
        for r in range(tm):
            for k in range(2):
                src = dest_ref[base + (2 * r + k)]
                pltpu.make_async_copy(y_ref.at[pl.ds(src, 1), :], buf.at[slot, k, pl.ds(r, 1), :],
                                      sem.at[slot]).start(priority=k)

    @pl.when(j >= 1)
    def _():
        slot = (j - 1) % 2
        for k in range(2):
            pltpu.make_async_copy(y_ref.at[pl.ds(0, tm), :], buf.at[slot, k], sem.at[slot]).wait()
        rg = rg_ref[...]
        y = buf[slot, 0] * rg[:, 0:1] + buf[slot, 1] * rg[:, 1:2]
        res = x_ref[...] + mod_ref[0, 5:6, :] * y
        if len(out_refs) == 1:
            out_refs[0][...] = res
        else:
            @pl.when(j - 1 < nt_c)
            def _():
                out_refs[0][...] = res

            @pl.when(j - 1 >= nt_c)
            def _():
                out_refs[1][...] = res


def _combine_call(dest, x_new, route_g, mods, yb, mod_map, nt_c, split):
    t, d = x_new.shape
    prev = lambda j: jnp.maximum(j - 1, 0)
    if split:
        out_specs = [pl.BlockSpec((TM, d), lambda j, p: (jnp.minimum(prev(j), nt_c - 1), 0)),
                     pl.BlockSpec((TM, d), lambda j, p: (jnp.maximum(prev(j) - nt_c, 0), 0))]
        out_shape = [jax.ShapeDtypeStruct((nt_c * TM, d), F32), jax.ShapeDtypeStruct((t - nt_c * TM, d), F32)]
    else:
        out_specs = [pl.BlockSpec((TM, d), lambda j, p: (prev(j), 0))]
        out_shape = [jax.ShapeDtypeStruct((t, d), F32)]
    return pl.pallas_call(
        functools.partial(_combine_kernel, nt_c=nt_c),
        grid_spec=pltpu.PrefetchScalarGridSpec(
            num_scalar_prefetch=1,
            grid=(t // TM + 1,),
            in_specs=[pl.BlockSpec((TM, d), lambda j, p: (prev(j), 0)),
                      pl.BlockSpec((TM, LANES), lambda j, p: (prev(j), 0)),
                      pl.BlockSpec((1, 6, d), lambda j, p: mod_map(prev(j))),
                      pl.BlockSpec(memory_space=pl.ANY)],
            out_specs=out_specs,
            scratch_shapes=[pltpu.VMEM((2, 2, TM, d), F32), pltpu.SemaphoreType.DMA((2,))],
        ),
        out_shape=out_shape,
        compiler_params=_cparams(("arbitrary",)),
        name="moe_combine",
    )(dest, x_new, route_g, mods, yb)


def _head_cols(base, heads):
    return np.concatenate([base + h * HEAD_DIM + np.arange(HEAD_DIM) for h in heads])


def _layouts():
    wa, wb, wc = HA * HEAD_DIM, HB * HEAD_DIM, HC * HEAD_DIM
    qa0, ka0, va0 = 0, wa, 2 * wa
    qb0 = 3 * wa
    kb0 = qb0 + wb
    vb0 = kb0 + KVB * HEAD_DIM
    qc0 = vb0 + KVB * HEAD_DIM
    kc0 = qc0 + wc
    vc0 = kc0 + KVC * HEAD_DIM
    b_heads = [h for c in range(4) for h in (c, c + 4)]
    c_heads = [0, 2, 1, 3]
    q_cols = np.concatenate([_head_cols(qb0, b_heads), _head_cols(qa0, range(HA)), _head_cols(qc0, c_heads)])
    k_cols = np.concatenate([ka0 + np.arange(wa), kb0 + np.arange(KVB * HEAD_DIM), kc0 + np.arange(KVC * HEAD_DIM)])
    v_cols = np.concatenate([va0 + np.arange(wa), vb0 + np.arange(KVB * HEAD_DIM), vc0 + np.arange(KVC * HEAD_DIM)])
    in_cols = np.concatenate([q_cols, k_cols, v_cols])
    gain_idx = np.concatenate([np.full(HB, 2), np.full(HA, 0), np.full(HC, 4),
                               np.full(HA, 1), np.full(KVB, 3), np.full(KVC, 5)])
    gain_scale = np.concatenate([np.full(HB + HA + HC, HEAD_DIM ** -0.5 * LOG2E), np.ones(HA + KVB + KVC)])
    return in_cols, gain_idx, gain_scale.astype(np.float32), _head_cols(0, b_heads), _head_cols(0, c_heads)


def _rope_tables(seq):
    quarter = HEAD_DIM // 4
    t = jnp.arange(seq)
    rows, cols = t // GRID_W, t % GRID_W
    freqs = ROPE_THETA ** (-jnp.arange(quarter, dtype=F32) / quarter)
    ang_r = rows.astype(F32)[:, None] * freqs[None, :]
    ang_c = cols.astype(F32)[:, None] * freqs[None, :]
    cos = jnp.concatenate([jnp.cos(ang_r), jnp.cos(ang_r), jnp.cos(ang_c), jnp.cos(ang_c)], axis=1)
    sin = jnp.concatenate([-jnp.sin(ang_r), jnp.sin(ang_r), -jnp.sin(ang_c), jnp.sin(ang_c)], axis=1)
    return jnp.tile(cos, (1, 2)), jnp.tile(sin, (1, 2))


def kernel(x_prompt, x_sample, cache_ctx_k, cache_ctx_v, c, c_ctx, norm_mix, norm_ffn, w_ada, b_ada, w_in, qk_gain,
           rpb, sink, w_br_a, w_br_b, w_br_c, w_gate, w_out, w_route_group, w_route_expert, w_exp_gate, w_exp_up,
           w_exp_down):
    batch, seq_c, d = x_prompt.shape
    nb, seq_d, _ = x_sample.shape
    depth = w_in.shape[0]
    t_c, t_d = batch * seq_c, nb * seq_d
    t = t_c + t_d
    n_ctx = cache_ctx_k.shape[2]
    assert t_c % TM == 0 and seq_d % TM == 0 and nb + 1 <= 8 and 2 * t < (1 << 16)
    nt_c = t_c // TM
    tiles_per_seq = seq_d // TM
    nb_max = 2 * t // BM + N_EXPERTS
    nb_max = (nb_max + 7) // 8 * 8

    in_cols, gain_idx, gain_scale, ob_rows, oc_rows = _layouts()
    x_c, x_d, x_d_off = x_prompt.reshape(t_c, d), x_sample.reshape(t_d, d), 0
    ones_ctx = jnp.ones((nb, n_ctx, LANES), BF16)
    cvec = jnp.concatenate([c_ctx[None, :], c, jnp.zeros((8 - 1 - nb, d), F32)], axis=0)
    mods = _ada_call(cvec, w_ada, b_ada).reshape(depth, 8, 6, d)
    rope_tabs = _rope_tables(seq_d)

    def mod_map(i):
        return (jnp.where(i < nt_c, 0, 1 + (i - nt_c) // tiles_per_seq), 0, 0)

    new_k, new_v = [], []
    for l in range(depth):
        w_in_p = w_in[l][:, in_cols].astype(BF16)
        gain_row = (qk_gain[l][gain_idx] * gain_scale[:, None]).reshape(1, -1)
        gain_row = jnp.concatenate([gain_row, jnp.ones((1, 512), F32)], axis=1)
        nm = norm_mix[l].reshape(1, d)
        nf = norm_ffn[l].reshape(1, d)

        q_c, k_c, v_c, k32, v32 = _qkv_call(x_c, mods[l], nm, w_in_p, gain_row, tile_off=0, n_tok=t_c,
                                            seq_len=seq_c, row_off=0, per_seq_mod=False, rope_tabs=None, emit_f32=True)
        new_k.append(k32.reshape(batch, seq_c, 8, HEAD_DIM))
        new_v.append(v32.reshape(batch, seq_c, 8, HEAD_DIM))
        q_d, k_d, v_d = _qkv_call(x_d, mods[l], nm, w_in_p, gain_row, tile_off=x_d_off, n_tok=t_d,
                                  seq_len=seq_d, row_off=1, per_seq_mod=True, rope_tabs=rope_tabs, emit_f32=False)
        k_d = k_d.reshape(nb, seq_d, 512)
        v_d = v_d.reshape(nb, seq_d, V_EXT_COLS)

        o_ctx = _ctx_attn_call(sink[l], q_c, k_c, v_c, seq_c)
        ck = cache_ctx_k[:, l].reshape(nb, n_ctx, 512).astype(BF16)
        cv = cache_ctx_v[:, l].reshape(nb, n_ctx, 512).astype(BF16)
        cv = jnp.concatenate([cv[:, :, :384], ones_ctx, cv[:, :, 384:], ones_ctx], axis=2)
        o_b = _dec_b_call(q_d, jnp.concatenate([k_d[:, :, 256:384], ck[:, :, 256:384]], axis=1),
                          jnp.concatenate([v_d[:, :, 256:512], cv[:, :, 256:512]], axis=1))
        o_c = _dec_c_call(sink[l], q_d, k_d, v_d, ck, cv)
        o_a = _dec_a_call(q_d, k_d, v_d, ck, cv, _bias_call(rpb[l]))

        w_route = jnp.concatenate([w_route_group[l], w_route_expert[l],
                                   jnp.zeros((d, LANES - N_GROUPS - N_EXPERTS), F32)], axis=1)
        wr_hi = w_route.astype(BF16)
        wr_lo = (w_route - wr_hi.astype(F32)).astype(BF16)
        x_new, h, logits = _merge_call(
            x_c, x_d, x_d_off, o_ctx, o_b, o_a, o_c, mods[l], nm, nf, w_gate[l].astype(BF16), w_br_b[l][ob_rows].astype(BF16),
            w_br_a[l].astype(BF16), w_br_c[l][oc_rows].astype(BF16), w_out[l].astype(BF16), wr_hi, wr_lo, mod_map)

        route_i, route_g, tab, blk = _route_call(logits, nb_max)
        dest = _slots_call(route_i, tab)[:, :2].reshape(-1)
        xs = _dispatch_call(dest, jnp.zeros((nb_max * BM, d // 2), jnp.uint32), h)
        yb = _expert_call(blk[:, 0], tab[1, N_GROUPS + N_EXPERTS - 1:N_GROUPS + N_EXPERTS], xs,
                          w_exp_gate, w_exp_up, w_exp_down, l)
        outs = _combine_call(dest, x_new, route_g, mods[l], yb, mod_map, nt_c, split=(l == depth - 1))
        if l < depth - 1:
            x_c = x_d = outs[0]
            x_d_off = nt_c

    y_prompt = outs[0].reshape(batch, seq_c, d)
    y_sample = outs[1].reshape(nb, seq_d, d)
    return (y_prompt, y_sample, jnp.stack(new_k, axis=1), jnp.stack(new_v, axis=1))
```

```python
import functools

import jax
import jax.numpy as jnp
import numpy as np
from jax import lax
from jax.experimental import pallas as pl
from jax.experimental.pallas import tpu as pltpu

GRID_W = 64
HEAD_DIM = 64
HA, HB, KVB, HC, KVC = 4, 8, 2, 4, 2
NA_WIN_R, NA_WIN_C = 8, 16
WINDOW = 128
ROPE_THETA = 10000.0
N_GROUPS, EXPERTS_PER_GROUP, N_EXPERTS = 4, 8, 32
EPS = 1e-6
NEG = -1e30
LOG2E = 1.4426950408889634

LANES = 128
V7X_VMEM_LIMIT = 56 * 1024 * 1024

TM = 512
TQ_B = 128
TK_B = 256
TQ_C = 256
ROWS_A = 8
BM = 512
ROW_CHAINS = 2

F32 = jnp.float32
BF16 = jnp.bfloat16


def _cparams(sem, vmem=V7X_VMEM_LIMIT):
    return pltpu.CompilerParams(dimension_semantics=sem, vmem_limit_bytes=vmem)


def _split_bf16(x):
    hi = x.astype(BF16)
    lo = (x - hi.astype(F32)).astype(BF16)
    return hi, lo


def _pack_bf16_pairs(x_bf16):
    n = x_bf16.shape[1] // 2
    lo = lax.bitcast_convert_type(x_bf16[:, :n].astype(F32), jnp.uint32)
    hi = lax.bitcast_convert_type(x_bf16[:, n:].astype(F32), jnp.uint32)
    return (lo >> 16) | hi


def _unpack_bf16_pairs(w_u32):
    lo = lax.bitcast_convert_type(w_u32 << 16, F32)
    hi = lax.bitcast_convert_type(w_u32 & jnp.uint32(0xFFFF0000), F32)
    return jnp.concatenate([lo, hi], axis=1).astype(BF16)


def _dot(a, b):
    return jnp.dot(a, b, preferred_element_type=F32)


def _dot_nt(a, b):
    return lax.dot_general(a, b, (((1,), (1,)), ((), ())), preferred_element_type=F32)


def _half_masks(dtype):
    lane = lax.broadcasted_iota(jnp.int32, (1, LANES), 1)
    lo = (lane < HEAD_DIM)
    return lo, jnp.where(lo, 1.0, 0.0).astype(dtype), jnp.where(lo, 0.0, 1.0).astype(dtype)


def _ada_kernel(c_ref, w_ref, b_ref, o_ref):
    c = c_ref[...]
    s = c * (1.0 / (1.0 + jnp.exp(-c)))
    s_hi, s_lo = _split_bf16(s)
    w_hi, w_lo = _split_bf16(w_ref[...])
    o_ref[...] = _dot(s_hi, w_hi) + _dot(s_lo, w_hi) + _dot(s_hi, w_lo) + b_ref[...]


def _ada_call(cvec, w_ada, b_ada):
    depth, d, n6 = w_ada.shape
    tn = 512
    return pl.pallas_call(
        _ada_kernel,
        grid=(depth, n6 // tn),
        in_specs=[
            pl.BlockSpec((8, d), lambda l, j: (0, 0)),
            pl.BlockSpec((None, d, tn), lambda l, j: (l, 0, j)),
            pl.BlockSpec((None, 1, tn), lambda l, j: (l, 0, j)),
        ],
        out_specs=pl.BlockSpec((None, 8, tn), lambda l, j: (l, 0, j)),
        out_shape=jax.ShapeDtypeStruct((depth, 8, n6), F32),
        compiler_params=_cparams(("arbitrary", "arbitrary")),
        name="ada_modulation",
    )(cvec, w_ada, b_ada.reshape(depth, 1, n6))


N_QCHUNK = 8
N_KCHUNK = 4
ROPE_QCHUNKS = (0, 1, 2, 3, 6, 7)
ROPE_KCHUNKS = (2, 3)
V_EXT_COLS = 6 * LANES


def _modulated_norm(x, gain, scale, shift):
    ms = jnp.mean(x * x, axis=-1, keepdims=True)
    return (x * lax.rsqrt(ms + EPS) * gain) * (1.0 + scale) + shift


def _qkv_kernel(*refs, rope, emit_f32, n_prev):
    x_ref, mod_ref, nm_ref, w_ref, gain_ref = refs[:5]
    pos = 5
    if rope:
        cos_ref, sin_ref, bd_ref = refs[pos:pos + 3]
        pos += 3
    prev_refs = refs[pos:pos + 2 * n_prev]
    pos += 2 * n_prev
    q_ref, k_ref, v_ref = refs[pos:pos + 3]
    pos += 3
    if emit_f32:
        k32_ref, v32_ref = refs[pos:pos + 2]

    def put_f32(ref, cols, val):
        if n_prev:
            ref[:, n_prev, :, cols] = val.reshape(ref.shape[0], ref.shape[2], val.shape[-1])
        else:
            ref[:, cols] = val

    for l in range(n_prev):
        for src, dst in ((prev_refs[2 * l], k32_ref), (prev_refs[2 * l + 1], v32_ref)):
            dst[:, l] = src[...].reshape(dst.shape[0], dst.shape[2], dst.shape[3])

    lane = lax.broadcasted_iota(jnp.int32, (1, LANES), 1)
    lo = lane < HEAD_DIM
    first = (lane % 32) < 16
    tm = x_ref.shape[0]
    n_chains = ROW_CHAINS if rope else 1

    for c in range(n_chains):
        rows = slice(c * tm // n_chains, (c + 1) * tm // n_chains)
        u = _modulated_norm(x_ref[rows, :], nm_ref[...], mod_ref[0, 1:2, :], mod_ref[0, 0:1, :])
        y = _dot(u.astype(BF16), w_ref[...])
        if rope:
            cos_t = cos_ref[rows, :]
            sin_t = sin_ref[rows, :]
        pair_sumsq = {}

        def head_sumsq(j):
            yj = y[:, j * LANES:(j + 1) * LANES]
            if not rope:
                sq = yj * yj
                s_lo = jnp.sum(jnp.where(lo, sq, 0.0), axis=-1, keepdims=True)
                s_hi = jnp.sum(jnp.where(lo, 0.0, sq), axis=-1, keepdims=True)
                return jnp.where(lo, s_lo, s_hi)
            p = j // 2
            if p not in pair_sumsq:
                y2 = y[:, 2 * p * LANES:(2 * p + 2) * LANES]
                sq_hi, sq_lo = _split_bf16(y2 * y2)
                pair_sumsq[p] = _dot(sq_hi, bd_ref[...]) + _dot(sq_lo, bd_ref[...])
            return pair_sumsq[p][:, (j % 2) * LANES:(j % 2 + 1) * LANES]

        def norm_chunk(j, use_rope):
            yj = y[:, j * LANES:(j + 1) * LANES]
            inv = lax.rsqrt(head_sumsq(j) * (1.0 / HEAD_DIM) + EPS)
            yj = (yj * inv) * gain_ref[:, j * LANES:(j + 1) * LANES]
            if use_rope:
                partner = jnp.where(first, pltpu.roll(yj, LANES - 16, axis=1), pltpu.roll(yj, 16, axis=1))
                yj = yj * cos_t + partner * sin_t
            return yj

        for j in range(N_QCHUNK):
            yj = norm_chunk(j, rope and j in ROPE_QCHUNKS)
            q_ref[rows, j * LANES:(j + 1) * LANES] = yj.astype(BF16)
        for j in range(N_KCHUNK):
            yj = norm_chunk(N_QCHUNK + j, rope and j in ROPE_KCHUNKS)
            k_ref[rows, j * LANES:(j + 1) * LANES] = yj.astype(BF16)
            if emit_f32:
                put_f32(k32_ref, slice(j * LANES, (j + 1) * LANES), yj)
        v = y[:, (N_QCHUNK + N_KCHUNK) * LANES:]
        if rope:
            ones = jnp.ones((v.shape[0], LANES), BF16)
            v_ref[rows, :3 * LANES] = v[:, :3 * LANES].astype(BF16)
            v_ref[rows, 3 * LANES:4 * LANES] = ones
            v_ref[rows, 4 * LANES:5 * LANES] = v[:, 3 * LANES:].astype(BF16)
            v_ref[rows, 5 * LANES:] = ones
        else:
            v_ref[...] = v.astype(BF16)
        if emit_f32:
            put_f32(v32_ref, slice(None), v)


def _qkv_call(x_all, mods, norm_mix_l, w_in_p, gain_row, *, tile_off, n_tok, seq_len, row_off,
              per_seq_mod, rope_tabs, emit_f32, prev_kv=()):
    d = x_all.shape[1]
    nt = n_tok // TM
    tiles_per_seq = max(seq_len // TM, 1)
    rope = rope_tabs is not None
    n_prev = len(prev_kv) // 2

    def mod_map(i):
        return (row_off + i // tiles_per_seq, 0, 0) if per_seq_mod else (row_off, 0, 0)

    in_specs = [
        pl.BlockSpec((TM, d), lambda i: (i + tile_off, 0)),
        pl.BlockSpec((1, 6, d), mod_map),
        pl.BlockSpec((1, d), lambda i: (0, 0)),
        pl.BlockSpec(w_in_p.shape, lambda i: (0, 0)),
        pl.BlockSpec((1, 2048), lambda i: (0, 0)),
    ]
    args = [x_all, mods, norm_mix_l, w_in_p, gain_row]
    if rope:
        in_specs += [pl.BlockSpec((TM, LANES), lambda i: (i % tiles_per_seq, 0))] * 2
        in_specs += [pl.BlockSpec((2 * LANES, 2 * LANES), lambda i: (0, 0))]
        head_of_lane = np.arange(2 * LANES) // HEAD_DIM
        args += list(rope_tabs) + [jnp.asarray(head_of_lane[:, None] == head_of_lane[None, :], BF16)]
    v_cols = V_EXT_COLS if rope else 512
    out_shape = [jax.ShapeDtypeStruct((n_tok, 1024), BF16),
                 jax.ShapeDtypeStruct((n_tok, 512), BF16),
                 jax.ShapeDtypeStruct((n_tok, v_cols), BF16)]
    out_specs = [pl.BlockSpec((TM, 1024), lambda i: (i, 0)),
                 pl.BlockSpec((TM, 512), lambda i: (i, 0)),
                 pl.BlockSpec((TM, v_cols), lambda i: (i, 0))]
    in_specs += [pl.BlockSpec((TM, 512), lambda i: (i, 0))] * (2 * n_prev)
    args += list(prev_kv)
    if emit_f32 and n_prev:
        req = TM // seq_len
        out_shape += [jax.ShapeDtypeStruct((n_tok // seq_len, n_prev + 1, seq_len, 512), F32)] * 2
        out_specs += [pl.BlockSpec((req, n_prev + 1, seq_len, 512), lambda i: (i, 0, 0, 0))] * 2
    elif emit_f32:
        out_shape += [jax.ShapeDtypeStruct((n_tok, 512), F32)] * 2
        out_specs += [pl.BlockSpec((TM, 512), lambda i: (i, 0))] * 2
    return pl.pallas_call(
        functools.partial(_qkv_kernel, rope=rope, emit_f32=emit_f32, n_prev=n_prev),
        grid=(nt,),
        in_specs=in_specs,
        out_specs=out_specs,
        out_shape=out_shape,
        compiler_params=_cparams(("parallel",)),
        name="qkv_rope" if rope else "qkv_ctx",
    )(*args)


def _softmax_rows(s, sink_col=None):
    m = jnp.max(s, axis=-1, keepdims=True)
    if sink_col is not None:
        m = jnp.maximum(m, sink_col)
    p = jnp.exp2(s - m)
    l = jnp.sum(p, axis=-1, keepdims=True)
    if sink_col is not None:
        l = l + jnp.exp2(sink_col - m)
    return p, l


def _ctx_attn_kernel(sink_ref, q_ref, k_ref, v_ref, o_ref):
    s_len = q_ref.shape[0]
    _, m_lo, m_hi = _half_masks(BF16)
    lo = lax.broadcasted_iota(jnp.int32, (1, LANES), 1) < HEAD_DIM

    def group(q_chunks, kc, sinks=None):
        n = len(q_chunks)
        k = k_ref[:, kc * LANES:(kc + 1) * LANES]
        v = v_ref[:, kc * LANES:(kc + 1) * LANES]
        qs = jnp.concatenate([q_ref[:, c * LANES:(c + 1) * LANES] * m for m in (m_lo, m_hi) for c in q_chunks],
                             axis=0)
        s = _dot_nt(qs, k)
        sink_col = None
        if sinks is not None:
            row = lax.broadcasted_iota(jnp.int32, (2 * n * s_len, 1), 0)
            sink_col = jnp.zeros((2 * n * s_len, 1), F32)
            for idx, sv in enumerate(sinks):
                sink_col = jnp.where((row >= idx * s_len) & (row < (idx + 1) * s_len), sv, sink_col)
        p, l = _softmax_rows(s, sink_col)
        o = _dot(p.astype(BF16), v) / l
        for idx, c in enumerate(q_chunks):
            o_c = jnp.where(lo, o[idx * s_len:(idx + 1) * s_len], o[(n + idx) * s_len:(n + idx + 1) * s_len])
            o_ref[:, c * LANES:(c + 1) * LANES] = o_c.astype(BF16)

    group([0, 1, 2, 3], 2)
    group([4], 0)
    group([5], 1)
    group([6, 7], 3, sinks=[sink_ref[h] * LOG2E for h in range(HC)])


def _ctx_attn_call(sink_l, q, k, v, s_len):
    batch = q.shape[0] // s_len
    return pl.pallas_call(
        _ctx_attn_kernel,
        grid_spec=pltpu.PrefetchScalarGridSpec(
            num_scalar_prefetch=1,
            grid=(batch,),
            in_specs=[pl.BlockSpec((s_len, 1024), lambda b, s: (b, 0)),
                      pl.BlockSpec((s_len, 512), lambda b, s: (b, 0)),
                      pl.BlockSpec((s_len, 512), lambda b, s: (b, 0))],
            out_specs=pl.BlockSpec((s_len, 1024), lambda b, s: (b, 0)),
        ),
        out_shape=jax.ShapeDtypeStruct((q.shape[0], 1024), BF16),
        compiler_params=_cparams(("parallel",)),
        name="ctx_attention",
    )(sink_l, q, k, v)


def _dec_b_kernel(q_ref, k_ref, v_ref, o_ref):
    n_keys = k_ref.shape[0]
    n_chunks = n_keys // TK_B
    _, m_lo, m_hi = _half_masks(BF16)
    lo = lax.broadcasted_iota(jnp.int32, (1, LANES), 1) < HEAD_DIM

    def stacked_q(msk):
        return jnp.concatenate([q_ref[:, c * LANES:(c + 1) * LANES] * msk for c in range(4)], axis=0)

    def write(outs):
        for c in range(4):
            o_c = jnp.where(lo, outs[0][c * TQ_B:(c + 1) * TQ_B], outs[1][c * TQ_B:(c + 1) * TQ_B])
            o_ref[:, c * LANES:(c + 1) * LANES] = o_c.astype(BF16)

    outs, sums = [], []
    for msk in (m_lo, m_hi):
        qs = stacked_q(msk)
        s = _dot_nt(qs, k_ref[0:TK_B, :])
        m = jnp.max(s, axis=-1, keepdims=True)
        acc = _dot(jnp.exp2(s - m).astype(BF16), v_ref[0:TK_B, :])
        for t in range(1, n_chunks):
            s = _dot_nt(qs, k_ref[t * TK_B:(t + 1) * TK_B, :])
            acc = acc + _dot(jnp.exp2(s - m).astype(BF16), v_ref[t * TK_B:(t + 1) * TK_B, :])
        outs.append(acc[:, :LANES] / acc[:, LANES:])
        sums.append(jnp.sum(acc * 0.0))
    write(outs)

    @pl.when(jnp.logical_not(sums[0] + sums[1] == 0.0))
    def _():
        exact = []
        for msk in (m_lo, m_hi):
            s = _dot_nt(stacked_q(msk), k_ref[...])
            acc = _dot(jnp.exp2(s - jnp.max(s, axis=-1, keepdims=True)).astype(BF16), v_ref[...])
            exact.append(acc[:, :LANES] / acc[:, LANES:])
        write(exact)


def _dec_b_call(q, k_all, v_all):
    nb, n_keys, _ = k_all.shape
    seq = q.shape[0] // nb
    nq = seq // TQ_B
    return pl.pallas_call(
        _dec_b_kernel,
        grid=(nb, nq),
        in_specs=[pl.BlockSpec((TQ_B, 512), lambda b, i: (b * nq + i, 0)),
                  pl.BlockSpec((None, n_keys, LANES), lambda b, i: (b, 0, 0)),
                  pl.BlockSpec((None, n_keys, 2 * LANES), lambda b, i: (b, 0, 0))],
        out_specs=pl.BlockSpec((TQ_B, 512), lambda b, i: (b * nq + i, 0)),
        out_shape=jax.ShapeDtypeStruct((q.shape[0], 512), BF16),
        compiler_params=_cparams(("parallel", "parallel")),
        name="dec_dense_attention",
    )(q, k_all, v_all)


def _dec_c_kernel(sink_ref, q_ref, k_ref, v_ref, ck_ref, cv_ref, o_ref):
    qi = pl.program_id(1)
    seq = k_ref.shape[0]
    win = TQ_C + 2 * WINDOW
    _, m_lo, m_hi = _half_masks(BF16)
    lo = lax.broadcasted_iota(jnp.int32, (1, LANES), 1) < HEAD_DIM
    start = jnp.clip(qi * TQ_C - WINDOW, 0, seq - win)
    start = pl.multiple_of(start, WINDOW)
    k_win = k_ref[pl.ds(start, win), :]
    v_win = v_ref[pl.ds(start, win), :]
    k_ctx = ck_ref[...]
    v_ctx = cv_ref[...]
    row = lax.broadcasted_iota(jnp.int32, (2 * TQ_C, win), 0)
    col = lax.broadcasted_iota(jnp.int32, (2 * TQ_C, win), 1)
    qpos = qi * TQ_C + jnp.where(row >= TQ_C, row - TQ_C, row)
    ok = jnp.abs(qpos - (start + col)) <= WINDOW
    row1 = lax.broadcasted_iota(jnp.int32, (2 * TQ_C, 1), 0)
    def attend(exact_shift):
        outs, finite = [], []
        for kvh, msk in enumerate((m_lo, m_hi)):
            qs = jnp.concatenate([q_ref[:, c * LANES:(c + 1) * LANES] * msk for c in range(2)], axis=0)
            s_loc = jnp.where(ok, _dot_nt(qs, k_win), NEG)
            s_ctx = _dot_nt(qs, k_ctx)
            sink_col = jnp.where(row1 < TQ_C, sink_ref[2 * kvh] * LOG2E, sink_ref[2 * kvh + 1] * LOG2E)
            m = sink_col
            if exact_shift:
                m = jnp.maximum(jnp.maximum(jnp.max(s_loc, axis=-1, keepdims=True),
                                            jnp.max(s_ctx, axis=-1, keepdims=True)), sink_col)
            acc = (_dot(jnp.exp2(s_loc - m).astype(BF16), v_win)
                   + _dot(jnp.exp2(s_ctx - m).astype(BF16), v_ctx))
            outs.append(acc[:, :LANES] / (acc[:, LANES:] + jnp.exp2(sink_col - m)))
            finite.append(jnp.sum(acc * 0.0))
        for c in range(2):
            o_c = jnp.where(lo, outs[0][c * TQ_C:(c + 1) * TQ_C], outs[1][c * TQ_C:(c + 1) * TQ_C])
            o_ref[:, c * LANES:(c + 1) * LANES] = o_c.astype(BF16)
        return finite[0] + finite[1]

    check = attend(exact_shift=False)

    @pl.when(jnp.logical_not(check == 0.0))
    def _():
        attend(exact_shift=True)


def _dec_c_call(sink_l, q, k_d, v_d, ck, cv):
    nb, seq, _ = k_d.shape
    n_ctx = ck.shape[1]
    nq = seq // TQ_C
    return pl.pallas_call(
        _dec_c_kernel,
        grid_spec=pltpu.PrefetchScalarGridSpec(
            num_scalar_prefetch=1,
            grid=(nb, nq),
            in_specs=[pl.BlockSpec((TQ_C, 256), lambda b, i, s: (b * nq + i, 3)),
                      pl.BlockSpec((None, seq, LANES), lambda b, i, s: (b, 0, 3)),
                      pl.BlockSpec((None, seq, 2 * LANES), lambda b, i, s: (b, 0, 2)),
                      pl.BlockSpec((None, n_ctx, LANES), lambda b, i, s: (b, 0, 3)),
                      pl.BlockSpec((None, n_ctx, 2 * LANES), lambda b, i, s: (b, 0, 2))],
            out_specs=pl.BlockSpec((TQ_C, 256), lambda b, i, s: (b * nq + i, 0)),
        ),
        out_shape=jax.ShapeDtypeStruct((q.shape[0], 256), BF16),
        compiler_params=_cparams(("parallel", "parallel")),
        name="dec_window_attention",
    )(sink_l, q, k_d, v_d, ck, cv)


N_DR = 2 * NA_WIN_R - 2


def _bias_kernel(rpb_ref, o_ref):
    lh = pl.program_id(0)
    n_dc = 2 * NA_WIN_C - 1
    qc = lax.broadcasted_iota(jnp.int32, (GRID_W, LANES), 0)
    kk = lax.broadcasted_iota(jnp.int32, (GRID_W, LANES), 1)
    half = kk >= GRID_W
    kc = jnp.where(half, kk - GRID_W, kk)
    delta = jnp.clip(kc - qc, -(NA_WIN_C - 1), NA_WIN_C - 1) + (NA_WIN_C - 1)
    col_start = jnp.clip(qc - NA_WIN_C // 2, 0, GRID_W - NA_WIN_C)
    ok = (kc >= col_start) & (kc < col_start + NA_WIN_C)
    for d in range(N_DR):
        acc = jnp.zeros((GRID_W, LANES), F32)
        base = (lh * (2 * NA_WIN_R - 1) + d) * n_dc
        for dc in range(n_dc):
            val = jnp.where(half, rpb_ref[base + n_dc + dc] * LOG2E, rpb_ref[base + dc] * LOG2E)
            acc = jnp.where(delta == dc, val, acc)
        o_ref[d] = jnp.where(ok, acc, NEG)


def _bias_call(rpb):
    depth = rpb.shape[0]
    return pl.pallas_call(
        _bias_kernel,
        grid_spec=pltpu.PrefetchScalarGridSpec(
            num_scalar_prefetch=1,
            grid=(depth * HA,),
            in_specs=[],
            out_specs=pl.BlockSpec((None, N_DR, GRID_W, LANES), lambda lh, r: (lh, 0, 0, 0)),
        ),
        out_shape=jax.ShapeDtypeStruct((depth * HA, N_DR, GRID_W, LANES), F32),
        compiler_params=_cparams(("arbitrary",)),
        name="na_bias_table",
    )(rpb.reshape(-1)).reshape(depth, HA, N_DR, GRID_W, LANES)


def _dec_a_kernel(q_ref, k_ref, v_ref, ck_ref, cv_ref, bias_ref, o_ref):
    rb = pl.program_id(2)
    rows = k_ref.shape[0] // GRID_W
    n_loc = NA_WIN_R * GRID_W
    _, m_lo, m_hi = _half_masks(BF16)
    lo = lax.broadcasted_iota(jnp.int32, (1, LANES), 1) < HEAD_DIM
    qs = [jnp.concatenate([q_ref[rr * GRID_W:(rr + 1) * GRID_W, :] * msk for msk in (m_lo, m_hi)], axis=0)
          for rr in range(ROWS_A)]
    s_ctx = _dot_nt(jnp.concatenate(qs, axis=0), ck_ref[...])
    s_loc, v_win = [], []
    for rr in range(ROWS_A):
        r = rb * ROWS_A + rr
        start = jnp.clip(r - NA_WIN_R // 2, 0, rows - NA_WIN_R)
        d0 = start - r + (NA_WIN_R - 1)
        off = pl.multiple_of(start * GRID_W, GRID_W)
        bias = jnp.concatenate(
            [jnp.concatenate([bias_ref[hh, d0 + 2 * i] for i in range(NA_WIN_R // 2)], axis=1) for hh in range(2)],
            axis=0)
        s_loc.append(_dot_nt(qs[rr], k_ref[pl.ds(off, n_loc), :]) + bias)
        v_win.append(v_ref[pl.ds(off, n_loc), :])
    m = jnp.maximum(jnp.concatenate([jnp.max(s, axis=-1, keepdims=True) for s in s_loc], axis=0),
                    jnp.max(s_ctx, axis=-1, keepdims=True))
    p_ctx = jnp.exp2(s_ctx - m)
    l_ctx = jnp.sum(p_ctx, axis=-1, keepdims=True)
    o_ctx = _dot(p_ctx.astype(BF16), cv_ref[...])
    for rr in range(ROWS_A):
        sl = slice(rr * 2 * GRID_W, (rr + 1) * 2 * GRID_W)
        p_loc = jnp.exp2(s_loc[rr] - m[sl])
        l = jnp.sum(p_loc, axis=-1, keepdims=True) + l_ctx[sl]
        o = (_dot(p_loc.astype(BF16), v_win[rr]) + o_ctx[sl]) / l
        o_c = jnp.where(lo, o[:GRID_W], o[GRID_W:])
        o_ref[rr * GRID_W:(rr + 1) * GRID_W, :] = o_c.astype(BF16)


def _dec_a_call(q, k_d, v_d, ck, cv, bias_tab):
    nb, seq, _ = k_d.shape
    n_ctx = ck.shape[1]
    tq = ROWS_A * GRID_W
    nq = seq // tq
    return pl.pallas_call(
        _dec_a_kernel,
        grid=(2, nb, nq),
        in_specs=[pl.BlockSpec((tq, LANES), lambda p, b, i: (b * nq + i, 4 + p)),
                  pl.BlockSpec((None, seq, LANES), lambda p, b, i: (b, 0, p)),
                  pl.BlockSpec((None, seq, LANES), lambda p, b, i: (b, 0, p)),
                  pl.BlockSpec((None, n_ctx, LANES), lambda p, b, i: (b, 0, p)),
                  pl.BlockSpec((None, n_ctx, LANES), lambda p, b, i: (b, 0, p)),
                  pl.BlockSpec((2, N_DR, GRID_W, LANES), lambda p, b, i: (p, 0, 0, 0))],
        out_specs=pl.BlockSpec((tq, LANES), lambda p, b, i: (b * nq + i, p)),
        out_shape=jax.ShapeDtypeStruct((q.shape[0], 2 * LANES), BF16),
        compiler_params=_cparams(("parallel", "parallel", "parallel")),
        name="dec_neighbourhood_attention",
    )(q, k_d, v_d, ck, cv, bias_tab)


def _merge_kernel(xc_ref, xd_ref, oc_ref, ob_ref, oa_ref, ow_ref, mod_ref, nm_ref, nf_ref, wg_ref, wbb_ref, wba_ref,
                  wbc_ref, wo_ref, wr_ref, xo_ref, h_ref, lg_ref, *, nt_c):
    d = xc_ref.shape[1]
    tm = xc_ref.shape[0]
    is_ctx = pl.program_id(0) < nt_c
    for c in range(ROW_CHAINS):
        rows = slice(c * tm // ROW_CHAINS, (c + 1) * tm // ROW_CHAINS)
        o_b = jnp.where(is_ctx, oc_ref[rows, :512], ob_ref[rows, :])
        o_a = jnp.where(is_ctx, oc_ref[rows, 512:768], oa_ref[rows, :])
        o_c = jnp.where(is_ctx, oc_ref[rows, 768:], ow_ref[rows, :])
        x = jnp.where(is_ctx, xc_ref[rows, :], xd_ref[rows, :])
        u = _modulated_norm(x, nm_ref[...], mod_ref[0, 1:2, :], mod_ref[0, 0:1, :])
        gates = _dot(u.astype(BF16), wg_ref[...])
        gates = 1.0 / (1.0 + jnp.exp(-gates))
        m = (gates[:, :d] * _dot(o_a, wba_ref[...])
             + gates[:, d:2 * d] * _dot(o_b, wbb_ref[...])
             + gates[:, 2 * d:] * _dot(o_c, wbc_ref[...]))
        x_new = x + mod_ref[0, 2:3, :] * _dot(m.astype(BF16), wo_ref[...])
        xo_ref[rows, :] = x_new
        h = _modulated_norm(x_new, nf_ref[...], mod_ref[0, 4:5, :], mod_ref[0, 3:4, :])
        h_hi, h_lo = _split_bf16(h)
        h_ref[rows, :] = _pack_bf16_pairs(h_hi)
        both = _dot(h_hi, wr_ref[...])
        lg_ref[rows, :] = both[:, :LANES] + both[:, LANES:] + _dot(h_lo, wr_ref[:, :LANES])


def _merge_call(x_c, x_d, x_d_tile_off, o_ctx, o_b, o_a, o_c, mods, nm, nf, wg, wbb, wba, wbc, wo, wr, mod_map):
    d = x_c.shape[1]
    nt_c = o_ctx.shape[0] // TM
    t = o_ctx.shape[0] + o_b.shape[0]
    const = lambda i: (0, 0)
    full = lambda a: pl.BlockSpec(a.shape, const)
    ctx_map = lambda i: (jnp.minimum(i, nt_c - 1), 0)
    dec_map = lambda i: (jnp.maximum(i - nt_c, 0), 0)
    return pl.pallas_call(
        functools.partial(_merge_kernel, nt_c=nt_c),
        grid=(t // TM,),
        in_specs=[pl.BlockSpec((TM, d), ctx_map),
                  pl.BlockSpec((TM, d), lambda i: (jnp.maximum(i - nt_c, 0) + x_d_tile_off, 0)),
                  pl.BlockSpec((TM, 1024), ctx_map),
                  pl.BlockSpec((TM, 512), dec_map),
                  pl.BlockSpec((TM, 256), dec_map),
                  pl.BlockSpec((TM, 256), dec_map),
                  pl.BlockSpec((1, 6, d), mod_map),
                  full(nm), full(nf), full(wg), full(wbb), full(wba), full(wbc), full(wo), full(wr)],
        out_specs=[pl.BlockSpec((TM, d), lambda i: (i, 0)),
                   pl.BlockSpec((TM, d // 2), lambda i: (i, 0)),
                   pl.BlockSpec((TM, LANES), lambda i: (i, 0))],
        out_shape=[jax.ShapeDtypeStruct((t, d), F32),
                   jax.ShapeDtypeStruct((t, d // 2), jnp.uint32),
                   jax.ShapeDtypeStruct((t, LANES), F32)],
        compiler_params=_cparams(("parallel",)),
        name="merge_branches",
    )(x_c, x_d, o_ctx, o_b, o_a, o_c, mods, nm, nf, wg, wbb, wba, wbc, wo, wr)


def _route_kernel(lg_ref, ri_ref, rg_ref, tab_ref, blk_ref, cnt_sc, *, nb_max):
    i = pl.program_id(0)
    n = pl.num_programs(0)
    tm = lg_ref.shape[0]

    @pl.when(i == 0)
    def _():
        cnt_sc[...] = jnp.zeros(cnt_sc.shape, F32)

    lg = lg_ref[...]
    lane = lax.broadcasted_iota(jnp.int32, (tm, LANES), 1).astype(F32)
    big = float(LANES)
    neg_inf = -jnp.inf
    gl = jnp.where(lane < N_GROUPS, lg, neg_inf)
    gmax = jnp.max(gl, axis=-1, keepdims=True)
    g_sel = jnp.min(jnp.where(gl == gmax, lane, big), axis=-1, keepdims=True)
    g_w = 1.0 / jnp.sum(jnp.exp(gl - gmax), axis=-1, keepdims=True)
    e_lo = N_GROUPS + EXPERTS_PER_GROUP * g_sel
    el = jnp.where((lane >= e_lo) & (lane < e_lo + EXPERTS_PER_GROUP), lg, neg_inf)
    m1 = jnp.max(el, axis=-1, keepdims=True)
    i1 = jnp.min(jnp.where(el == m1, lane, big), axis=-1, keepdims=True)
    el2 = jnp.where(lane == i1, neg_inf, el)
    m2 = jnp.max(el2, axis=-1, keepdims=True)
    i2 = jnp.min(jnp.where(el2 == m2, lane, big), axis=-1, keepdims=True)
    e2x = jnp.exp(m2 - m1)
    den = 1.0 + e2x
    gate1 = g_w * (1.0 / den)
    gate2 = g_w * (e2x / den)
    hit1 = lane == i1
    hit2 = lane == i2
    onehot = jnp.where(hit1 | hit2, 1.0, 0.0).astype(BF16)
    rr = lax.broadcasted_iota(jnp.int32, (tm, tm), 0)
    cc = lax.broadcasted_iota(jnp.int32, (tm, tm), 1)
    tri = jnp.where(cc <= rr, 1.0, 0.0).astype(BF16)
    pref = _dot(tri, onehot) + cnt_sc[...]
    rank1 = jnp.sum(jnp.where(hit1, pref, 0.0), axis=-1, keepdims=True) - 1.0
    rank2 = jnp.sum(jnp.where(hit2, pref, 0.0), axis=-1, keepdims=True) - 1.0
    cnt_sc[...] = pref[tm - 1:tm, :]
    e1 = (i1 - N_GROUPS).astype(jnp.int32)
    e2 = (i2 - N_GROUPS).astype(jnp.int32)
    pk1 = (e1 << 16) | rank1.astype(jnp.int32)
    pk2 = (e2 << 16) | rank2.astype(jnp.int32)
    lane_i = lax.broadcasted_iota(jnp.int32, (tm, LANES), 1)
    ri_ref[...] = jnp.where(lane_i == 0, pk1, jnp.where(lane_i == 1, pk2, 0))
    rg_ref[...] = jnp.where(lane_i == 0, gate1, jnp.where(lane_i == 1, gate2, 0.0))

    @pl.when(i == n - 1)
    def _():
        cnt = cnt_sc[...]
        nblk = jnp.floor((cnt + (BM - 1.0)) * (1.0 / BM))
        l8 = lax.broadcasted_iota(jnp.int32, (8, LANES), 1)
        is_exp = (l8 >= N_GROUPS) & (l8 < N_GROUPS + N_EXPERTS)
        nblk8 = jnp.where(is_exp, jnp.broadcast_to(nblk, (8, LANES)), 0.0)
        r2 = lax.broadcasted_iota(jnp.int32, (LANES, LANES), 0)
        c2 = lax.broadcasted_iota(jnp.int32, (LANES, LANES), 1)
        upper = jnp.where(r2 <= c2, 1.0, 0.0).astype(BF16)
        hi, lo_ = _split_bf16(nblk8)
        blk_end = _dot(hi, upper) + _dot(lo_, upper)
        blk_start = blk_end - nblk8
        row8 = lax.broadcasted_iota(jnp.int32, (8, LANES), 0)
        tab = jnp.where(row8 == 0, blk_start * BM, jnp.where(row8 == 1, blk_end, cnt))
        tab_ref[...] = tab.astype(jnp.int32)
        bi = lax.broadcasted_iota(jnp.int32, (nb_max, LANES), 0).astype(F32)
        ends = blk_end[0:1, :]
        lb = lax.broadcasted_iota(jnp.int32, (nb_max, LANES), 1)
        done = jnp.where((lb >= N_GROUPS) & (lb < N_GROUPS + N_EXPERTS) & (ends <= bi), 1.0, 0.0)
        be = jnp.minimum(jnp.sum(done, axis=-1, keepdims=True), N_EXPERTS - 1.0)
        blk_ref[...] = jnp.broadcast_to(be, (nb_max, LANES)).astype(jnp.int32)


def _route_call(logits, nb_max):
    t = logits.shape[0]
    return pl.pallas_call(
        functools.partial(_route_kernel, nb_max=nb_max),
        grid=(t // TM,),
        in_specs=[pl.BlockSpec((TM, LANES), lambda i: (i, 0))],
        out_specs=[pl.BlockSpec((TM, LANES), lambda i: (i, 0)),
                   pl.BlockSpec((TM, LANES), lambda i: (i, 0)),
                   pl.BlockSpec((8, LANES), lambda i: (0, 0)),
                   pl.BlockSpec((nb_max, LANES), lambda i: (0, 0))],
        out_shape=[jax.ShapeDtypeStruct((t, LANES), jnp.int32),
                   jax.ShapeDtypeStruct((t, LANES), F32),
                   jax.ShapeDtypeStruct((8, LANES), jnp.int32),
                   jax.ShapeDtypeStruct((nb_max, LANES), jnp.int32)],
        scratch_shapes=[pltpu.VMEM((1, LANES), F32)],
        compiler_params=_cparams(("arbitrary",)),
        name="moe_route",
    )(logits)


def _slots_kernel(ri_ref, tab_ref, o_ref):
    tm = ri_ref.shape[0]
    lane = lax.broadcasted_iota(jnp.int32, (tm, LANES), 1)
    starts = tab_ref[0:1, :].astype(F32)
    pk = ri_ref[...]
    out = jnp.zeros((tm, LANES), jnp.int32)
    for k in range(2):
        pkk = pk[:, k:k + 1]
        first = jnp.sum(jnp.where(lane == (pkk >> 16) + N_GROUPS, starts, 0.0), axis=-1, keepdims=True)
        out = jnp.where(lane == k, first.astype(jnp.int32) + (pkk & 0xFFFF), out)
    o_ref[...] = out


def _slots_call(route_i, tab):
    t = route_i.shape[0]
    tile = next(c for c in (4 * TM, 2 * TM, TM) if t % c == 0)
    return pl.pallas_call(
        _slots_kernel,
        grid=(t // tile,),
        in_specs=[pl.BlockSpec((tile, LANES), lambda i: (i, 0)),
                  pl.BlockSpec((8, LANES), lambda i: (0, 0))],
        out_specs=pl.BlockSpec((tile, LANES), lambda i: (i, 0)),
        out_shape=jax.ShapeDtypeStruct((t, LANES), jnp.int32),
        compiler_params=_cparams(("parallel",)),
        name="moe_slots",
    )(route_i, tab)


N_DISPATCH_BUF = 3


def _dispatch_kernel(dest_ref, ends_ref, h_ref, xs_ref, buf, zbuf, fsem, ssem, zsem, *, min_used):
    i = pl.program_id(0)
    n = pl.num_programs(0)
    tm = buf.shape[1]

    def fetch(tile, slot):
        return pltpu.make_async_copy(h_ref.at[pl.ds(tile * tm, tm), :], buf.at[slot], fsem.at[slot])

    def scatter_wait(slot):
        for _ in range(2):
            pltpu.make_async_copy(buf.at[slot], xs_ref.at[pl.ds(0, tm), :], ssem.at[slot]).wait()

    @pl.when(i == 0)
    def _():
        fetch(0, 0).start()

        @pl.when(n > 1)
        def _():
            fetch(1, 1).start()

        zbuf[...] = jnp.zeros(zbuf.shape, zbuf.dtype)
        n_blocks = xs_ref.shape[0] // BM
        lane0 = N_GROUPS - 1
        n_used = ends_ref[lane0 + N_EXPERTS]
        targets = [(ends_ref[lane0 + e + 1] > ends_ref[lane0 + e], ends_ref[lane0 + e + 1] - 1)
                   for e in range(N_EXPERTS)]
        targets += [(n_used + j < n_blocks, n_used + j) for j in range(n_blocks - min_used)]
        for wait in (False, True):
            for cond, blk in targets:
                @pl.when(cond)
                def _():
                    cp = pltpu.make_async_copy(zbuf, xs_ref.at[pl.ds(blk * BM, BM), :], zsem)
                    cp.wait() if wait else cp.start()

    slot = i % N_DISPATCH_BUF
    fetch(i, slot).wait()
    base = 2 * i * tm
    for s in range(N_DISPATCH_BUF):
        @pl.when(slot == s)
        def _():
            for r in range(tm):
                for k in range(2):
                    dst = dest_ref[base + (2 * r + k)]
                    pltpu.make_async_copy(buf.at[s, pl.ds(r, 1), :], xs_ref.at[pl.ds(dst, 1), :],
                                          ssem.at[s]).start(priority=k)

    @pl.when(i >= 1)
    def _():
        scatter_wait((i - 1) % N_DISPATCH_BUF)

    @pl.when(i + 2 < n)
    def _():
        fetch(i + 2, (i + 2) % N_DISPATCH_BUF).start()

    @pl.when(i == n - 1)
    def _():
        scatter_wait(slot)


def _dispatch_call(dest, block_ends, h, n_blocks):
    t, dw = h.shape
    return pl.pallas_call(
        functools.partial(_dispatch_kernel, min_used=2 * t // BM),
        grid_spec=pltpu.PrefetchScalarGridSpec(
            num_scalar_prefetch=2,
            grid=(t // TM,),
            in_specs=[pl.BlockSpec(memory_space=pl.ANY)],
            out_specs=pl.BlockSpec(memory_space=pl.ANY),
            scratch_shapes=[pltpu.VMEM((N_DISPATCH_BUF, TM, dw), h.dtype),
                            pltpu.VMEM((BM, dw), h.dtype),
                            pltpu.SemaphoreType.DMA((N_DISPATCH_BUF,)),
                            pltpu.SemaphoreType.DMA((N_DISPATCH_BUF,)),
                            pltpu.SemaphoreType.DMA],
        ),
        out_shape=jax.ShapeDtypeStruct((n_blocks * BM, dw), h.dtype),
        compiler_params=_cparams(("arbitrary",)),
        name="moe_dispatch",
    )(dest, block_ends, h)


def _expert_kernel(blk_ref, nused_ref, xs_ref, wg_hbm, wu_hbm, wd_hbm, y_ref, wg_f32, wu_f32, wd_f32,
                   wg_sc, wu_sc, wd_sc, n_loaded, sem, *, layer):
    i = pl.program_id(0)
    n_used = nused_ref[0]
    last = blk_ref.shape[0] - 1
    e = blk_ref[i]

    def fetch(expert, slot):
        return [pltpu.make_async_copy(src.at[layer, expert], dst.at[slot], sem.at[slot])
                for src, dst in ((wg_hbm, wg_f32), (wu_hbm, wu_f32), (wd_hbm, wd_f32))]

    @pl.when((i == 0) & (n_used > 0))
    def _():
        n_loaded[0] = 0
        for cp in fetch(e, 0):
            cp.start()

    @pl.when((i < n_used) & ((i == 0) | (e != blk_ref[jnp.maximum(i - 1, 0)])))
    def _():
        slot = n_loaded[0] % 2
        for cp in fetch(e, slot):
            cp.wait()
        nxt = lax.while_loop(lambda j: (j < n_used) & (blk_ref[jnp.minimum(j, last)] == e), lambda j: j + 1, i + 1)

        @pl.when(nxt < n_used)
        def _():
            for cp in fetch(blk_ref[jnp.minimum(nxt, last)], 1 - slot):
                cp.start()

        wg_sc[...] = wg_f32[slot].astype(BF16)
        wu_sc[...] = wu_f32[slot].astype(BF16)
        wd_sc[...] = wd_f32[slot].astype(BF16)
        n_loaded[0] = n_loaded[0] + 1

    @pl.when(i < n_used)
    def _():
        x = _unpack_bf16_pairs(xs_ref[...])
        g = _dot(x, wg_sc[...])
        u = _dot(x, wu_sc[...])
        hdn = (g * (1.0 / (1.0 + jnp.exp(-g)))) * u
        y_ref[...] = _dot(hdn.astype(BF16), wd_sc[...])

    @pl.when(i >= nused_ref[0])
    def _():
        y_ref[...] = jnp.zeros(y_ref.shape, F32)


def _expert_call(blk_expert, n_used, xs, wg, wu, wd, layer):
    p, dw = xs.shape
    d, ff = wg.shape[-2:]
    return pl.pallas_call(
        functools.partial(_expert_kernel, layer=layer),
        grid_spec=pltpu.PrefetchScalarGridSpec(
            num_scalar_prefetch=2,
            grid=(p // BM,),
            in_specs=[pl.BlockSpec((BM, dw), lambda i, b, n: (i, 0)),
                      pl.BlockSpec(memory_space=pl.ANY),
                      pl.BlockSpec(memory_space=pl.ANY),
                      pl.BlockSpec(memory_space=pl.ANY)],
            out_specs=pl.BlockSpec((BM, d), lambda i, b, n: (i, 0)),
            scratch_shapes=[pltpu.VMEM((2, d, ff), F32), pltpu.VMEM((2, d, ff), F32), pltpu.VMEM((2, ff, d), F32),
                            pltpu.VMEM((d, ff), BF16), pltpu.VMEM((d, ff), BF16), pltpu.VMEM((ff, d), BF16),
                            pltpu.SMEM((1,), jnp.int32), pltpu.SemaphoreType.DMA((2,))],
        ),
        out_shape=jax.ShapeDtypeStruct((p, d), F32),
        compiler_params=_cparams(("arbitrary",)),
        name="moe_experts",
    )(blk_expert, n_used, xs, wg, wu, wd)


def _combine_kernel(dest_ref, x_ref, rg_ref, mod_ref, y_ref, *rest, nt_c):
    out_refs, (buf, sem) = rest[:-2], rest[-2:]
    j = pl.program_id(0)
    n_tiles = pl.num_programs(0) - 1
    tm = x_ref.shape[0]

    for s in range(2):
        @pl.when((j < n_tiles) & (j % 2 == s))
        def _():
            base = 2 * j * tm
            for r in range(tm):
                for k in range(2):
                    src = dest_ref[base + (2 * r + k)]
                    pltpu.make_async_copy(y_ref.at[pl.ds(src, 1), :], buf.at[s, k, pl.ds(r, 1), :],
                                          sem.at[s]).start(priority=k)

    @pl.when(j >= 1)
    def _():
        slot = (j - 1) % 2
        for k in range(2):
            pltpu.make_async_copy(y_ref.at[pl.ds(0, tm), :], buf.at[slot, k], sem.at[slot]).wait()
        rg = rg_ref[...]
        y = buf[slot, 0] * rg[:, 0:1] + buf[slot, 1] * rg[:, 1:2]
        res = x_ref[...] + mod_ref[0, 5:6, :] * y
        if len(out_refs) == 1:
            out_refs[0][...] = res
        else:
            @pl.when(j - 1 < nt_c)
            def _():
                out_refs[0][...] = res

            @pl.when(j - 1 >= nt_c)
            def _():
                out_refs[1][...] = res


def _combine_call(dest, x_new, route_g, mods, yb, mod_map, nt_c, split):
    t, d = x_new.shape
    prev = lambda j: jnp.maximum(j - 1, 0)
    if split:
        out_specs = [pl.BlockSpec((TM, d), lambda j, p: (jnp.minimum(prev(j), nt_c - 1), 0)),
                     pl.BlockSpec((TM, d), lambda j, p: (jnp.maximum(prev(j) - nt_c, 0), 0))]
        out_shape = [jax.ShapeDtypeStruct((nt_c * TM, d), F32), jax.ShapeDtypeStruct((t - nt_c * TM, d), F32)]
    else:
        out_specs = [pl.BlockSpec((TM, d), lambda j, p: (prev(j), 0))]
        out_shape = [jax.ShapeDtypeStruct((t, d), F32)]
    return pl.pallas_call(
        functools.partial(_combine_kernel, nt_c=nt_c),
        grid_spec=pltpu.PrefetchScalarGridSpec(
            num_scalar_prefetch=1,
            grid=(t // TM + 1,),
            in_specs=[pl.BlockSpec((TM, d), lambda j, p: (prev(j), 0)),
                      pl.BlockSpec((TM, LANES), lambda j, p: (prev(j), 0)),
                      pl.BlockSpec((1, 6, d), lambda j, p: mod_map(prev(j))),
                      pl.BlockSpec(memory_space=pl.ANY)],
            out_specs=out_specs,
            scratch_shapes=[pltpu.VMEM((2, 2, TM, d), F32), pltpu.SemaphoreType.DMA((2,))],
        ),
        out_shape=out_shape,
        compiler_params=_cparams(("arbitrary",)),
        name="moe_combine",
    )(dest, x_new, route_g, mods, yb)


def _head_cols(base, heads):
    return np.concatenate([base + h * HEAD_DIM + np.arange(HEAD_DIM) for h in heads])


def _layouts():
    wa, wb, wc = HA * HEAD_DIM, HB * HEAD_DIM, HC * HEAD_DIM
    qa0, ka0, va0 = 0, wa, 2 * wa
    qb0 = 3 * wa
    kb0 = qb0 + wb
    vb0 = kb0 + KVB * HEAD_DIM
    qc0 = vb0 + KVB * HEAD_DIM
    kc0 = qc0 + wc
    vc0 = kc0 + KVC * HEAD_DIM
    b_heads = [h for c in range(4) for h in (c, c + 4)]
    c_heads = [0, 2, 1, 3]
    q_cols = np.concatenate([_head_cols(qb0, b_heads), _head_cols(qa0, range(HA)), _head_cols(qc0, c_heads)])
    k_cols = np.concatenate([ka0 + np.arange(wa), kb0 + np.arange(KVB * HEAD_DIM), kc0 + np.arange(KVC * HEAD_DIM)])
    v_cols = np.concatenate([va0 + np.arange(wa), vb0 + np.arange(KVB * HEAD_DIM), vc0 + np.arange(KVC * HEAD_DIM)])
    in_cols = np.concatenate([q_cols, k_cols, v_cols])
    gain_idx = np.concatenate([np.full(HB, 2), np.full(HA, 0), np.full(HC, 4),
                               np.full(HA, 1), np.full(KVB, 3), np.full(KVC, 5)])
    gain_scale = np.concatenate([np.full(HB + HA + HC, HEAD_DIM ** -0.5 * LOG2E), np.ones(HA + KVB + KVC)])
    return in_cols, gain_idx, gain_scale.astype(np.float32), _head_cols(0, b_heads), _head_cols(0, c_heads)


def _rope_tables(seq):
    quarter = HEAD_DIM // 4
    t = jnp.arange(seq)
    rows, cols = t // GRID_W, t % GRID_W
    freqs = ROPE_THETA ** (-jnp.arange(quarter, dtype=F32) / quarter)
    ang_r = rows.astype(F32)[:, None] * freqs[None, :]
    ang_c = cols.astype(F32)[:, None] * freqs[None, :]
    cos = jnp.concatenate([jnp.cos(ang_r), jnp.cos(ang_r), jnp.cos(ang_c), jnp.cos(ang_c)], axis=1)
    sin = jnp.concatenate([-jnp.sin(ang_r), jnp.sin(ang_r), -jnp.sin(ang_c), jnp.sin(ang_c)], axis=1)
    return jnp.tile(cos, (1, 2)), jnp.tile(sin, (1, 2))


def kernel(x_prompt, x_sample, cache_ctx_k, cache_ctx_v, c, c_ctx, norm_mix, norm_ffn, w_ada, b_ada, w_in, qk_gain,
           rpb, sink, w_br_a, w_br_b, w_br_c, w_gate, w_out, w_route_group, w_route_expert, w_exp_gate, w_exp_up,
           w_exp_down):
    batch, seq_c, d = x_prompt.shape
    nb, seq_d, _ = x_sample.shape
    depth = w_in.shape[0]
    t_c, t_d = batch * seq_c, nb * seq_d
    t = t_c + t_d
    n_ctx = cache_ctx_k.shape[2]
    assert t_c % TM == 0 and seq_d % TM == 0 and TM % seq_c == 0 and nb + 1 <= 8 and 2 * t < (1 << 16)
    nt_c = t_c // TM
    tiles_per_seq = seq_d // TM
    nb_max = 2 * t // BM + N_EXPERTS
    nb_max = (nb_max + 7) // 8 * 8

    in_cols, gain_idx, gain_scale, ob_rows, oc_rows = _layouts()
    x_c, x_d, x_d_off = x_prompt.reshape(t_c, d), x_sample.reshape(t_d, d), 0
    ones_ctx = jnp.ones((nb, n_ctx, LANES), BF16)
    cvec = jnp.concatenate([c_ctx[None, :], c, jnp.zeros((8 - 1 - nb, d), F32)], axis=0)
    mods = _ada_call(cvec, w_ada, b_ada).reshape(depth, 8, 6, d)
    rope_tabs = _rope_tables(seq_d)
    bias_tabs = _bias_call(rpb)

    def mod_map(i):
        return (jnp.where(i < nt_c, 0, 1 + (i - nt_c) // tiles_per_seq), 0, 0)

    prev_kv = []
    for l in range(depth):
        w_in_p = w_in[l][:, in_cols].astype(BF16)
        gain_row = (qk_gain[l][gain_idx] * gain_scale[:, None]).reshape(1, -1)
        gain_row = jnp.concatenate([gain_row, jnp.ones((1, 512), F32)], axis=1)
        nm = norm_mix[l].reshape(1, d)
        nf = norm_ffn[l].reshape(1, d)

        q_c, k_c, v_c, k32, v32 = _qkv_call(x_c, mods[l], nm, w_in_p, gain_row, tile_off=0, n_tok=t_c,
                                            seq_len=seq_c, row_off=0, per_seq_mod=False, rope_tabs=None, emit_f32=True,
                                            prev_kv=tuple(prev_kv) if l == depth - 1 else ())
        prev_kv += [k32, v32]
        q_d, k_d, v_d = _qkv_call(x_d, mods[l], nm, w_in_p, gain_row, tile_off=x_d_off, n_tok=t_d,
                                  seq_len=seq_d, row_off=1, per_seq_mod=True, rope_tabs=rope_tabs, emit_f32=False)
        k_d = k_d.reshape(nb, seq_d, 512)
        v_d = v_d.reshape(nb, seq_d, V_EXT_COLS)

        o_ctx = _ctx_attn_call(sink[l], q_c, k_c, v_c, seq_c)
        ck = cache_ctx_k[:, l].reshape(nb, n_ctx, 512).astype(BF16)
        cv = cache_ctx_v[:, l].reshape(nb, n_ctx, 512).astype(BF16)
        cv = jnp.concatenate([cv[:, :, :384], ones_ctx, cv[:, :, 384:], ones_ctx], axis=2)
        o_b = _dec_b_call(q_d, jnp.concatenate([k_d[:, :, 256:384], ck[:, :, 256:384]], axis=1),
                          jnp.concatenate([v_d[:, :, 256:512], cv[:, :, 256:512]], axis=1))
        o_c = _dec_c_call(sink[l], q_d, k_d, v_d, ck, cv)
        o_a = _dec_a_call(q_d, k_d, v_d, ck, cv, bias_tabs[l])

        w_route = jnp.concatenate([w_route_group[l], w_route_expert[l],
                                   jnp.zeros((d, LANES - N_GROUPS - N_EXPERTS), F32)], axis=1)
        wr_hi = w_route.astype(BF16)
        wr = jnp.concatenate([wr_hi, (w_route - wr_hi.astype(F32)).astype(BF16)], axis=1)
        x_new, h, logits = _merge_call(
            x_c, x_d, x_d_off, o_ctx, o_b, o_a, o_c, mods[l], nm, nf, w_gate[l].astype(BF16), w_br_b[l][ob_rows].astype(BF16),
            w_br_a[l].astype(BF16), w_br_c[l][oc_rows].astype(BF16), w_out[l].astype(BF16), wr, mod_map)

        route_i, route_g, tab, blk = _route_call(logits, nb_max)
        dest = _slots_call(route_i, tab)[:, :2].reshape(-1)
        xs = _dispatch_call(dest, tab[1], h, nb_max)
        yb = _expert_call(blk[:, 0], tab[1, N_GROUPS + N_EXPERTS - 1:N_GROUPS + N_EXPERTS], xs,
                          w_exp_gate, w_exp_up, w_exp_down, l)
        outs = _combine_call(dest, x_new, route_g, mods[l], yb, mod_map, nt_c, split=(l == depth - 1))
        if l < depth - 1:
            x_c = x_d = outs[0]
            x_d_off = nt_c

    y_prompt = outs[0].reshape(batch, seq_c, d)
    y_sample = outs[1].reshape(nb, seq_d, d)
    new_k, new_v = (a.reshape(batch, depth, seq_c, HA + KVB + KVC, HEAD_DIM) for a in prev_kv[-2:])
    return (y_prompt, y_sample, new_k, new_v)
```

```python
import functools

import jax
import jax.numpy as jnp
import numpy as np
from jax import lax
from jax.experimental import pallas as pl
from jax.experimental.pallas import tpu as pltpu

GRID_W = 64
HEAD_DIM = 64
HA, HB, KVB, HC, KVC = 4, 8, 2, 4, 2
NA_WIN_R, NA_WIN_C = 8, 16
WINDOW = 128
ROPE_THETA = 10000.0
N_GROUPS, EXPERTS_PER_GROUP, N_EXPERTS = 4, 8, 32
QB_COLS, QA_COLS, QC_COLS = HB * HEAD_DIM, HA * HEAD_DIM, HC * HEAD_DIM
Q_COLS = QB_COLS + QA_COLS + QC_COLS
KV_COLS = (HA + KVB + KVC) * HEAD_DIM
EPS = 1e-6
NEG = -1e30
LOG2E = 1.4426950408889634

LANES = 128
V7X_VMEM_LIMIT = 56 * 1024 * 1024

TM = 512
TQ_B = 128
TK_B = 256
TQ_C = 256
ROWS_A = 8
BM = 512
ROW_CHAINS = 2

F32 = jnp.float32
BF16 = jnp.bfloat16


def _cparams(sem, vmem=V7X_VMEM_LIMIT):
    return pltpu.CompilerParams(dimension_semantics=sem, vmem_limit_bytes=vmem)


def _split_bf16(x):
    hi = x.astype(BF16)
    lo = (x - hi.astype(F32)).astype(BF16)
    return hi, lo


def _pack_bf16_pairs(x_bf16):
    n = x_bf16.shape[1] // 2
    lo = lax.bitcast_convert_type(x_bf16[:, :n].astype(F32), jnp.uint32)
    hi = lax.bitcast_convert_type(x_bf16[:, n:].astype(F32), jnp.uint32)
    return (lo >> 16) | hi


def _unpack_bf16_pairs(w_u32):
    lo = lax.bitcast_convert_type(w_u32 << 16, F32)
    hi = lax.bitcast_convert_type(w_u32 & jnp.uint32(0xFFFF0000), F32)
    return jnp.concatenate([lo, hi], axis=1).astype(BF16)


def _dot(a, b):
    return jnp.dot(a, b, preferred_element_type=F32)


def _dot_nt(a, b):
    return lax.dot_general(a, b, (((1,), (1,)), ((), ())), preferred_element_type=F32)


def _half_masks(dtype):
    lane = lax.broadcasted_iota(jnp.int32, (1, LANES), 1)
    lo = (lane < HEAD_DIM)
    return lo, jnp.where(lo, 1.0, 0.0).astype(dtype), jnp.where(lo, 0.0, 1.0).astype(dtype)


def _ada_kernel(c_ref, w_ref, b_ref, o_ref):
    c = c_ref[...]
    s = c * (1.0 / (1.0 + jnp.exp(-c)))
    s_hi, s_lo = _split_bf16(s)
    w_hi, w_lo = _split_bf16(w_ref[...])
    o_ref[...] = _dot(s_hi, w_hi) + _dot(s_lo, w_hi) + _dot(s_hi, w_lo) + b_ref[...]


def _ada_call(cvec, w_ada, b_ada):
    depth, d, n6 = w_ada.shape
    tn = 512
    return pl.pallas_call(
        _ada_kernel,
        grid=(depth, n6 // tn),
        in_specs=[
            pl.BlockSpec((8, d), lambda l, j: (0, 0)),
            pl.BlockSpec((None, d, tn), lambda l, j: (l, 0, j)),
            pl.BlockSpec((None, 1, tn), lambda l, j: (l, 0, j)),
        ],
        out_specs=pl.BlockSpec((None, 8, tn), lambda l, j: (l, 0, j)),
        out_shape=jax.ShapeDtypeStruct((depth, 8, n6), F32),
        compiler_params=_cparams(("arbitrary", "arbitrary")),
        name="ada_modulation",
    )(cvec, w_ada, b_ada.reshape(depth, 1, n6))


N_QCHUNK = 8
N_KCHUNK = 4
ROPE_QCHUNKS = (0, 1, 2, 3, 6, 7)
ROPE_KCHUNKS = (2, 3)
V_EXT_COLS = 6 * LANES


def _modulated_norm(x, gain, scale, shift):
    ms = jnp.mean(x * x, axis=-1, keepdims=True)
    return (x * lax.rsqrt(ms + EPS) * gain) * (1.0 + scale) + shift


def _qkv_kernel(*refs, rope, emit_f32, n_prev):
    x_ref, mod_ref, nm_ref, w_ref, gain_ref = refs[:5]
    pos = 5
    if rope:
        cos_ref, sin_ref, bd_ref = refs[pos:pos + 3]
        pos += 3
    prev_refs = refs[pos:pos + 2 * n_prev]
    pos += 2 * n_prev
    q_ref, k_ref, v_ref = refs[pos:pos + 3]
    pos += 3
    if emit_f32:
        k32_ref, v32_ref = refs[pos:pos + 2]

    def put_f32(ref, cols, val):
        if n_prev:
            ref[:, n_prev, :, cols] = val.reshape(ref.shape[0], ref.shape[2], val.shape[-1])
        else:
            ref[:, cols] = val

    for l in range(n_prev):
        for src, dst in ((prev_refs[2 * l], k32_ref), (prev_refs[2 * l + 1], v32_ref)):
            dst[:, l] = src[...].reshape(dst.shape[0], dst.shape[2], dst.shape[3])

    lane = lax.broadcasted_iota(jnp.int32, (1, LANES), 1)
    lo = lane < HEAD_DIM
    first = (lane % 32) < 16
    tm = x_ref.shape[0]
    n_chains = ROW_CHAINS if rope else 1

    for c in range(n_chains):
        rows = slice(c * tm // n_chains, (c + 1) * tm // n_chains)
        u = _modulated_norm(x_ref[rows, :], nm_ref[...], mod_ref[0, 1:2, :], mod_ref[0, 0:1, :])
        y = _dot(u.astype(BF16), w_ref[...])
        if rope:
            cos_t = cos_ref[rows, :]
            sin_t = sin_ref[rows, :]
        pair_sumsq = {}

        def head_sumsq(j):
            yj = y[:, j * LANES:(j + 1) * LANES]
            if not rope:
                sq = yj * yj
                s_lo = jnp.sum(jnp.where(lo, sq, 0.0), axis=-1, keepdims=True)
                s_hi = jnp.sum(jnp.where(lo, 0.0, sq), axis=-1, keepdims=True)
                return jnp.where(lo, s_lo, s_hi)
            p = j // 2
            if p not in pair_sumsq:
                y2 = y[:, 2 * p * LANES:(2 * p + 2) * LANES]
                sq_hi, sq_lo = _split_bf16(y2 * y2)
                pair_sumsq[p] = _dot(sq_hi, bd_ref[...]) + _dot(sq_lo, bd_ref[...])
            return pair_sumsq[p][:, (j % 2) * LANES:(j % 2 + 1) * LANES]

        def norm_chunk(j, use_rope):
            yj = y[:, j * LANES:(j + 1) * LANES]
            inv = lax.rsqrt(head_sumsq(j) * (1.0 / HEAD_DIM) + EPS)
            yj = (yj * inv) * gain_ref[:, j * LANES:(j + 1) * LANES]
            if use_rope:
                partner = jnp.where(first, pltpu.roll(yj, LANES - 16, axis=1), pltpu.roll(yj, 16, axis=1))
                yj = yj * cos_t + partner * sin_t
            return yj

        for j in range(N_QCHUNK):
            yj = norm_chunk(j, rope and j in ROPE_QCHUNKS)
            q_ref[rows, j * LANES:(j + 1) * LANES] = yj.astype(BF16)
        for j in range(N_KCHUNK):
            yj = norm_chunk(N_QCHUNK + j, rope and j in ROPE_KCHUNKS)
            k_ref[rows, j * LANES:(j + 1) * LANES] = yj.astype(BF16)
            if emit_f32:
                put_f32(k32_ref, slice(j * LANES, (j + 1) * LANES), yj)
        v = y[:, (N_QCHUNK + N_KCHUNK) * LANES:]
        if rope:
            ones = jnp.ones((v.shape[0], LANES), BF16)
            v_ref[rows, :3 * LANES] = v[:, :3 * LANES].astype(BF16)
            v_ref[rows, 3 * LANES:4 * LANES] = ones
            v_ref[rows, 4 * LANES:5 * LANES] = v[:, 3 * LANES:].astype(BF16)
            v_ref[rows, 5 * LANES:] = ones
        else:
            v_ref[...] = v.astype(BF16)
        if emit_f32:
            put_f32(v32_ref, slice(None), v)


def _qkv_call(x_all, mods, norm_mix_l, w_in_p, gain_row, *, tile_off, n_tok, seq_len, row_off,
              per_seq_mod, rope_tabs, emit_f32, prev_kv=()):
    d = x_all.shape[1]
    nt = n_tok // TM
    tiles_per_seq = max(seq_len // TM, 1)
    rope = rope_tabs is not None
    n_prev = len(prev_kv) // 2

    def mod_map(i):
        return (row_off + i // tiles_per_seq, 0, 0) if per_seq_mod else (row_off, 0, 0)

    in_specs = [
        pl.BlockSpec((TM, d), lambda i: (i + tile_off, 0)),
        pl.BlockSpec((1, 6, d), mod_map),
        pl.BlockSpec((1, d), lambda i: (0, 0)),
        pl.BlockSpec(w_in_p.shape, lambda i: (0, 0)),
        pl.BlockSpec((1, Q_COLS + 2 * KV_COLS), lambda i: (0, 0)),
    ]
    args = [x_all, mods, norm_mix_l, w_in_p, gain_row]
    if rope:
        in_specs += [pl.BlockSpec((TM, LANES), lambda i: (i % tiles_per_seq, 0))] * 2
        in_specs += [pl.BlockSpec((2 * LANES, 2 * LANES), lambda i: (0, 0))]
        head_of_lane = np.arange(2 * LANES) // HEAD_DIM
        args += list(rope_tabs) + [jnp.asarray(head_of_lane[:, None] == head_of_lane[None, :], BF16)]
    v_cols = V_EXT_COLS if rope else KV_COLS
    out_shape = [jax.ShapeDtypeStruct((n_tok, Q_COLS), BF16),
                 jax.ShapeDtypeStruct((n_tok, KV_COLS), BF16),
                 jax.ShapeDtypeStruct((n_tok, v_cols), BF16)]
    out_specs = [pl.BlockSpec((TM, Q_COLS), lambda i: (i, 0)),
                 pl.BlockSpec((TM, KV_COLS), lambda i: (i, 0)),
                 pl.BlockSpec((TM, v_cols), lambda i: (i, 0))]
    in_specs += [pl.BlockSpec((TM, KV_COLS), lambda i: (i, 0))] * (2 * n_prev)
    args += list(prev_kv)
    if emit_f32 and n_prev:
        req = TM // seq_len
        out_shape += [jax.ShapeDtypeStruct((n_tok // seq_len, n_prev + 1, seq_len, KV_COLS), F32)] * 2
        out_specs += [pl.BlockSpec((req, n_prev + 1, seq_len, KV_COLS), lambda i: (i, 0, 0, 0))] * 2
    elif emit_f32:
        out_shape += [jax.ShapeDtypeStruct((n_tok, KV_COLS), F32)] * 2
        out_specs += [pl.BlockSpec((TM, KV_COLS), lambda i: (i, 0))] * 2
    return pl.pallas_call(
        functools.partial(_qkv_kernel, rope=rope, emit_f32=emit_f32, n_prev=n_prev),
        grid=(nt,),
        in_specs=in_specs,
        out_specs=out_specs,
        out_shape=out_shape,
        compiler_params=_cparams(("parallel",)),
        name="qkv_rope" if rope else "qkv_ctx",
    )(*args)


def _softmax_rows(s, sink_col=None):
    m = jnp.max(s, axis=-1, keepdims=True)
    if sink_col is not None:
        m = jnp.maximum(m, sink_col)
    p = jnp.exp2(s - m)
    l = jnp.sum(p, axis=-1, keepdims=True)
    if sink_col is not None:
        l = l + jnp.exp2(sink_col - m)
    return p, l


def _ctx_attn_kernel(sink_ref, q_ref, k_ref, v_ref, o_ref):
    s_len = q_ref.shape[0]
    _, m_lo, m_hi = _half_masks(BF16)
    lo = lax.broadcasted_iota(jnp.int32, (1, LANES), 1) < HEAD_DIM

    def group(q_chunks, kc, sinks=None):
        n = len(q_chunks)
        k = k_ref[:, kc * LANES:(kc + 1) * LANES]
        v = v_ref[:, kc * LANES:(kc + 1) * LANES]
        qs = jnp.concatenate([q_ref[:, c * LANES:(c + 1) * LANES] * m for m in (m_lo, m_hi) for c in q_chunks],
                             axis=0)
        s = _dot_nt(qs, k)
        sink_col = None
        if sinks is not None:
            row = lax.broadcasted_iota(jnp.int32, (2 * n * s_len, 1), 0)
            sink_col = jnp.zeros((2 * n * s_len, 1), F32)
            for idx, sv in enumerate(sinks):
                sink_col = jnp.where((row >= idx * s_len) & (row < (idx + 1) * s_len), sv, sink_col)
        p, l = _softmax_rows(s, sink_col)
        o = _dot(p.astype(BF16), v) / l
        for idx, c in enumerate(q_chunks):
            o_c = jnp.where(lo, o[idx * s_len:(idx + 1) * s_len], o[(n + idx) * s_len:(n + idx + 1) * s_len])
            o_ref[:, c * LANES:(c + 1) * LANES] = o_c.astype(BF16)

    group([0, 1, 2, 3], 2)
    group([4], 0)
    group([5], 1)
    group([6, 7], 3, sinks=[sink_ref[h] * LOG2E for h in range(HC)])


def _ctx_attn_call(sink_l, q, k, v, s_len):
    batch = q.shape[0] // s_len
    return pl.pallas_call(
        _ctx_attn_kernel,
        grid_spec=pltpu.PrefetchScalarGridSpec(
            num_scalar_prefetch=1,
            grid=(batch,),
            in_specs=[pl.BlockSpec((s_len, Q_COLS), lambda b, s: (b, 0)),
                      pl.BlockSpec((s_len, KV_COLS), lambda b, s: (b, 0)),
                      pl.BlockSpec((s_len, KV_COLS), lambda b, s: (b, 0))],
            out_specs=pl.BlockSpec((s_len, Q_COLS), lambda b, s: (b, 0)),
        ),
        out_shape=jax.ShapeDtypeStruct((q.shape[0], Q_COLS), BF16),
        compiler_params=_cparams(("parallel",)),
        name="ctx_attention",
    )(sink_l, q, k, v)


def _dec_b_kernel(q_ref, k_ref, v_ref, o_ref):
    n_keys = k_ref.shape[0]
    n_chunks = n_keys // TK_B
    _, m_lo, m_hi = _half_masks(BF16)
    lo = lax.broadcasted_iota(jnp.int32, (1, LANES), 1) < HEAD_DIM

    def stacked_q(msk):
        return jnp.concatenate([q_ref[:, c * LANES:(c + 1) * LANES] * msk for c in range(4)], axis=0)

    def write(outs):
        for c in range(4):
            o_c = jnp.where(lo, outs[0][c * TQ_B:(c + 1) * TQ_B], outs[1][c * TQ_B:(c + 1) * TQ_B])
            o_ref[:, c * LANES:(c + 1) * LANES] = o_c.astype(BF16)

    outs, sums = [], []
    for msk in (m_lo, m_hi):
        qs = stacked_q(msk)
        s = _dot_nt(qs, k_ref[0:TK_B, :])
        m = jnp.max(s, axis=-1, keepdims=True)
        acc = _dot(jnp.exp2(s - m).astype(BF16), v_ref[0:TK_B, :])
        for t in range(1, n_chunks):
            s = _dot_nt(qs, k_ref[t * TK_B:(t + 1) * TK_B, :])
            acc = acc + _dot(jnp.exp2(s - m).astype(BF16), v_ref[t * TK_B:(t + 1) * TK_B, :])
        outs.append(acc[:, :LANES] / acc[:, LANES:])
        sums.append(jnp.sum(acc * 0.0))
    write(outs)

    @pl.when(jnp.logical_not(sums[0] + sums[1] == 0.0))
    def _():
        exact = []
        for msk in (m_lo, m_hi):
            s = _dot_nt(stacked_q(msk), k_ref[...])
            acc = _dot(jnp.exp2(s - jnp.max(s, axis=-1, keepdims=True)).astype(BF16), v_ref[...])
            exact.append(acc[:, :LANES] / acc[:, LANES:])
        write(exact)


def _dec_b_call(q, k_all, v_all):
    nb, n_keys, _ = k_all.shape
    seq = q.shape[0] // nb
    nq = seq // TQ_B
    return pl.pallas_call(
        _dec_b_kernel,
        grid=(nb, nq),
        in_specs=[pl.BlockSpec((TQ_B, QB_COLS), lambda b, i: (b * nq + i, 0)),
                  pl.BlockSpec((None, n_keys, LANES), lambda b, i: (b, 0, 0)),
                  pl.BlockSpec((None, n_keys, 2 * LANES), lambda b, i: (b, 0, 0))],
        out_specs=pl.BlockSpec((TQ_B, QB_COLS), lambda b, i: (b * nq + i, 0)),
        out_shape=jax.ShapeDtypeStruct((q.shape[0], QB_COLS), BF16),
        compiler_params=_cparams(("parallel", "parallel")),
        name="dec_dense_attention",
    )(q, k_all, v_all)


def _dec_c_kernel(sink_ref, q_ref, k_ref, v_ref, ck_ref, cv_ref, o_ref):
    qi = pl.program_id(1)
    seq = k_ref.shape[0]
    win = TQ_C + 2 * WINDOW
    _, m_lo, m_hi = _half_masks(BF16)
    lo = lax.broadcasted_iota(jnp.int32, (1, LANES), 1) < HEAD_DIM
    start = jnp.clip(qi * TQ_C - WINDOW, 0, seq - win)
    start = pl.multiple_of(start, WINDOW)
    k_win = k_ref[pl.ds(start, win), :]
    v_win = v_ref[pl.ds(start, win), :]
    k_ctx = ck_ref[...]
    v_ctx = cv_ref[...]
    row = lax.broadcasted_iota(jnp.int32, (2 * TQ_C, win), 0)
    col = lax.broadcasted_iota(jnp.int32, (2 * TQ_C, win), 1)
    qpos = qi * TQ_C + jnp.where(row >= TQ_C, row - TQ_C, row)
    ok = jnp.abs(qpos - (start + col)) <= WINDOW
    row1 = lax.broadcasted_iota(jnp.int32, (2 * TQ_C, 1), 0)
    def attend(exact_shift):
        outs, finite = [], []
        for kvh, msk in enumerate((m_lo, m_hi)):
            qs = jnp.concatenate([q_ref[:, c * LANES:(c + 1) * LANES] * msk for c in range(2)], axis=0)
            s_loc = jnp.where(ok, _dot_nt(qs, k_win), NEG)
            s_ctx = _dot_nt(qs, k_ctx)
            sink_col = jnp.where(row1 < TQ_C, sink_ref[2 * kvh] * LOG2E, sink_ref[2 * kvh + 1] * LOG2E)
            m = sink_col
            if exact_shift:
                m = jnp.maximum(jnp.maximum(jnp.max(s_loc, axis=-1, keepdims=True),
                                            jnp.max(s_ctx, axis=-1, keepdims=True)), sink_col)
            acc = (_dot(jnp.exp2(s_loc - m).astype(BF16), v_win)
                   + _dot(jnp.exp2(s_ctx - m).astype(BF16), v_ctx))
            outs.append(acc[:, :LANES] / (acc[:, LANES:] + jnp.exp2(sink_col - m)))
            finite.append(jnp.sum(acc * 0.0))
        for c in range(2):
            o_c = jnp.where(lo, outs[0][c * TQ_C:(c + 1) * TQ_C], outs[1][c * TQ_C:(c + 1) * TQ_C])
            o_ref[:, c * LANES:(c + 1) * LANES] = o_c.astype(BF16)
        return finite[0] + finite[1]

    check = attend(exact_shift=False)

    @pl.when(jnp.logical_not(check == 0.0))
    def _():
        attend(exact_shift=True)


def _dec_c_call(sink_l, q, k_d, v_d, ck, cv):
    nb, seq, _ = k_d.shape
    n_ctx = ck.shape[1]
    nq = seq // TQ_C
    return pl.pallas_call(
        _dec_c_kernel,
        grid_spec=pltpu.PrefetchScalarGridSpec(
            num_scalar_prefetch=1,
            grid=(nb, nq),
            in_specs=[pl.BlockSpec((TQ_C, QC_COLS), lambda b, i, s: (b * nq + i, (QB_COLS + QA_COLS) // QC_COLS)),
                      pl.BlockSpec((None, seq, LANES), lambda b, i, s: (b, 0, 3)),
                      pl.BlockSpec((None, seq, 2 * LANES), lambda b, i, s: (b, 0, 2)),
                      pl.BlockSpec((None, n_ctx, LANES), lambda b, i, s: (b, 0, 3)),
                      pl.BlockSpec((None, n_ctx, 2 * LANES), lambda b, i, s: (b, 0, 2))],
            out_specs=pl.BlockSpec((TQ_C, QC_COLS), lambda b, i, s: (b * nq + i, 0)),
        ),
        out_shape=jax.ShapeDtypeStruct((q.shape[0], QC_COLS), BF16),
        compiler_params=_cparams(("parallel", "parallel")),
        name="dec_window_attention",
    )(sink_l, q, k_d, v_d, ck, cv)


N_DR = 2 * NA_WIN_R - 2


def _bias_kernel(rpb_ref, o_ref):
    lh = pl.program_id(0)
    n_dc = 2 * NA_WIN_C - 1
    qc = lax.broadcasted_iota(jnp.int32, (GRID_W, LANES), 0)
    kk = lax.broadcasted_iota(jnp.int32, (GRID_W, LANES), 1)
    half = kk >= GRID_W
    kc = jnp.where(half, kk - GRID_W, kk)
    delta = jnp.clip(kc - qc, -(NA_WIN_C - 1), NA_WIN_C - 1) + (NA_WIN_C - 1)
    col_start = jnp.clip(qc - NA_WIN_C // 2, 0, GRID_W - NA_WIN_C)
    ok = (kc >= col_start) & (kc < col_start + NA_WIN_C)
    for d in range(N_DR):
        acc = jnp.zeros((GRID_W, LANES), F32)
        base = (lh * (2 * NA_WIN_R - 1) + d) * n_dc
        for dc in range(n_dc):
            val = jnp.where(half, rpb_ref[base + n_dc + dc] * LOG2E, rpb_ref[base + dc] * LOG2E)
            acc = jnp.where(delta == dc, val, acc)
        o_ref[d] = jnp.where(ok, acc, NEG)


def _bias_call(rpb):
    depth = rpb.shape[0]
    return pl.pallas_call(
        _bias_kernel,
        grid_spec=pltpu.PrefetchScalarGridSpec(
            num_scalar_prefetch=1,
            grid=(depth * HA,),
            in_specs=[],
            out_specs=pl.BlockSpec((None, N_DR, GRID_W, LANES), lambda lh, r: (lh, 0, 0, 0)),
        ),
        out_shape=jax.ShapeDtypeStruct((depth * HA, N_DR, GRID_W, LANES), F32),
        compiler_params=_cparams(("arbitrary",)),
        name="na_bias_table",
    )(rpb.reshape(-1)).reshape(depth, HA, N_DR, GRID_W, LANES)


def _dec_a_kernel(q_ref, k_ref, v_ref, ck_ref, cv_ref, bias_ref, o_ref):
    rb = pl.program_id(2)
    rows = k_ref.shape[0] // GRID_W
    n_loc = NA_WIN_R * GRID_W
    _, m_lo, m_hi = _half_masks(BF16)
    lo = lax.broadcasted_iota(jnp.int32, (1, LANES), 1) < HEAD_DIM
    qs = [jnp.concatenate([q_ref[rr * GRID_W:(rr + 1) * GRID_W, :] * msk for msk in (m_lo, m_hi)], axis=0)
          for rr in range(ROWS_A)]
    s_ctx = _dot_nt(jnp.concatenate(qs, axis=0), ck_ref[...])
    s_loc, v_win = [], []
    for rr in range(ROWS_A):
        r = rb * ROWS_A + rr
        start = jnp.clip(r - NA_WIN_R // 2, 0, rows - NA_WIN_R)
        d0 = start - r + (NA_WIN_R - 1)
        off = pl.multiple_of(start * GRID_W, GRID_W)
        bias = jnp.concatenate(
            [jnp.concatenate([bias_ref[hh, d0 + 2 * i] for i in range(NA_WIN_R // 2)], axis=1) for hh in range(2)],
            axis=0)
        s_loc.append(_dot_nt(qs[rr], k_ref[pl.ds(off, n_loc), :]) + bias)
        v_win.append(v_ref[pl.ds(off, n_loc), :])
    m = jnp.maximum(jnp.concatenate([jnp.max(s, axis=-1, keepdims=True) for s in s_loc], axis=0),
                    jnp.max(s_ctx, axis=-1, keepdims=True))
    p_ctx = jnp.exp2(s_ctx - m)
    l_ctx = jnp.sum(p_ctx, axis=-1, keepdims=True)
    o_ctx = _dot(p_ctx.astype(BF16), cv_ref[...])
    for rr in range(ROWS_A):
        sl = slice(rr * 2 * GRID_W, (rr + 1) * 2 * GRID_W)
        p_loc = jnp.exp2(s_loc[rr] - m[sl])
        l = jnp.sum(p_loc, axis=-1, keepdims=True) + l_ctx[sl]
        o = (_dot(p_loc.astype(BF16), v_win[rr]) + o_ctx[sl]) / l
        o_c = jnp.where(lo, o[:GRID_W], o[GRID_W:])
        o_ref[rr * GRID_W:(rr + 1) * GRID_W, :] = o_c.astype(BF16)


def _dec_a_call(q, k_d, v_d, ck, cv, bias_tab):
    nb, seq, _ = k_d.shape
    n_ctx = ck.shape[1]
    tq = ROWS_A * GRID_W
    nq = seq // tq
    return pl.pallas_call(
        _dec_a_kernel,
        grid=(2, nb, nq),
        in_specs=[pl.BlockSpec((tq, LANES), lambda p, b, i: (b * nq + i, 4 + p)),
                  pl.BlockSpec((None, seq, LANES), lambda p, b, i: (b, 0, p)),
                  pl.BlockSpec((None, seq, LANES), lambda p, b, i: (b, 0, p)),
                  pl.BlockSpec((None, n_ctx, LANES), lambda p, b, i: (b, 0, p)),
                  pl.BlockSpec((None, n_ctx, LANES), lambda p, b, i: (b, 0, p)),
                  pl.BlockSpec((2, N_DR, GRID_W, LANES), lambda p, b, i: (p, 0, 0, 0))],
        out_specs=pl.BlockSpec((tq, LANES), lambda p, b, i: (b * nq + i, p)),
        out_shape=jax.ShapeDtypeStruct((q.shape[0], 2 * LANES), BF16),
        compiler_params=_cparams(("parallel", "parallel", "parallel")),
        name="dec_neighbourhood_attention",
    )(q, k_d, v_d, ck, cv, bias_tab)


def _merge_kernel(xc_ref, xd_ref, oc_ref, ob_ref, oa_ref, ow_ref, mod_ref, nm_ref, nf_ref, wg_ref, wbb_ref, wba_ref,
                  wbc_ref, wo_ref, wr_ref, xo_ref, h_ref, lg_ref, *, nt_c):
    d = xc_ref.shape[1]
    tm = xc_ref.shape[0]
    is_ctx = pl.program_id(0) < nt_c
    for c in range(ROW_CHAINS):
        rows = slice(c * tm // ROW_CHAINS, (c + 1) * tm // ROW_CHAINS)
        o_b = jnp.where(is_ctx, oc_ref[rows, :QB_COLS], ob_ref[rows, :])
        o_a = jnp.where(is_ctx, oc_ref[rows, QB_COLS:QB_COLS + QA_COLS], oa_ref[rows, :])
        o_c = jnp.where(is_ctx, oc_ref[rows, QB_COLS + QA_COLS:], ow_ref[rows, :])
        x = jnp.where(is_ctx, xc_ref[rows, :], xd_ref[rows, :])
        u = _modulated_norm(x, nm_ref[...], mod_ref[0, 1:2, :], mod_ref[0, 0:1, :])
        gates = _dot(u.astype(BF16), wg_ref[...])
        gates = 1.0 / (1.0 + jnp.exp(-gates))
        m = (gates[:, :d] * _dot(o_a, wba_ref[...])
             + gates[:, d:2 * d] * _dot(o_b, wbb_ref[...])
             + gates[:, 2 * d:] * _dot(o_c, wbc_ref[...]))
        x_new = x + mod_ref[0, 2:3, :] * _dot(m.astype(BF16), wo_ref[...])
        xo_ref[rows, :] = x_new
        h = _modulated_norm(x_new, nf_ref[...], mod_ref[0, 4:5, :], mod_ref[0, 3:4, :])
        h_hi, h_lo = _split_bf16(h)
        h_ref[rows, :] = _pack_bf16_pairs(h_hi)
        both = _dot(h_hi, wr_ref[...])
        lg_ref[rows, :] = both[:, :LANES] + both[:, LANES:] + _dot(h_lo, wr_ref[:, :LANES])


def _merge_call(x_c, x_d, x_d_tile_off, o_ctx, o_b, o_a, o_c, mods, nm, nf, wg, wbb, wba, wbc, wo, wr, mod_map):
    d = x_c.shape[1]
    nt_c = o_ctx.shape[0] // TM
    t = o_ctx.shape[0] + o_b.shape[0]
    const = lambda i: (0, 0)
    full = lambda a: pl.BlockSpec(a.shape, const)
    ctx_map = lambda i: (jnp.minimum(i, nt_c - 1), 0)
    dec_map = lambda i: (jnp.maximum(i - nt_c, 0), 0)
    return pl.pallas_call(
        functools.partial(_merge_kernel, nt_c=nt_c),
        grid=(t // TM,),
        in_specs=[pl.BlockSpec((TM, d), ctx_map),
                  pl.BlockSpec((TM, d), lambda i: (jnp.maximum(i - nt_c, 0) + x_d_tile_off, 0)),
                  pl.BlockSpec((TM, Q_COLS), ctx_map),
                  pl.BlockSpec((TM, QB_COLS), dec_map),
                  pl.BlockSpec((TM, QA_COLS), dec_map),
                  pl.BlockSpec((TM, QC_COLS), dec_map),
                  pl.BlockSpec((1, 6, d), mod_map),
                  full(nm), full(nf), full(wg), full(wbb), full(wba), full(wbc), full(wo), full(wr)],
        out_specs=[pl.BlockSpec((TM, d), lambda i: (i, 0)),
                   pl.BlockSpec((TM, d // 2), lambda i: (i, 0)),
                   pl.BlockSpec((TM, LANES), lambda i: (i, 0))],
        out_shape=[jax.ShapeDtypeStruct((t, d), F32),
                   jax.ShapeDtypeStruct((t, d // 2), jnp.uint32),
                   jax.ShapeDtypeStruct((t, LANES), F32)],
        compiler_params=_cparams(("parallel",)),
        name="merge_branches",
    )(x_c, x_d, o_ctx, o_b, o_a, o_c, mods, nm, nf, wg, wbb, wba, wbc, wo, wr)


def _route_kernel(lg_ref, ri_ref, rg_ref, tab_ref, blk_ref, cnt_sc, *, nb_max):
    i = pl.program_id(0)
    n = pl.num_programs(0)
    tm = lg_ref.shape[0]

    @pl.when(i == 0)
    def _():
        cnt_sc[...] = jnp.zeros(cnt_sc.shape, F32)

    lg = lg_ref[...]
    lane = lax.broadcasted_iota(jnp.int32, (tm, LANES), 1).astype(F32)
    big = float(LANES)
    neg_inf = -jnp.inf
    gl = jnp.where(lane < N_GROUPS, lg, neg_inf)
    gmax = jnp.max(gl, axis=-1, keepdims=True)
    g_sel = jnp.min(jnp.where(gl == gmax, lane, big), axis=-1, keepdims=True)
    g_w = 1.0 / jnp.sum(jnp.exp(gl - gmax), axis=-1, keepdims=True)
    e_lo = N_GROUPS + EXPERTS_PER_GROUP * g_sel
    el = jnp.where((lane >= e_lo) & (lane < e_lo + EXPERTS_PER_GROUP), lg, neg_inf)
    m1 = jnp.max(el, axis=-1, keepdims=True)
    i1 = jnp.min(jnp.where(el == m1, lane, big), axis=-1, keepdims=True)
    el2 = jnp.where(lane == i1, neg_inf, el)
    m2 = jnp.max(el2, axis=-1, keepdims=True)
    i2 = jnp.min(jnp.where(el2 == m2, lane, big), axis=-1, keepdims=True)
    e2x = jnp.exp(m2 - m1)
    den = 1.0 + e2x
    gate1 = g_w * (1.0 / den)
    gate2 = g_w * (e2x / den)
    hit1 = lane == i1
    hit2 = lane == i2
    onehot = jnp.where(hit1 | hit2, 1.0, 0.0).astype(BF16)
    rr = lax.broadcasted_iota(jnp.int32, (tm, tm), 0)
    cc = lax.broadcasted_iota(jnp.int32, (tm, tm), 1)
    tri = jnp.where(cc <= rr, 1.0, 0.0).astype(BF16)
    pref = _dot(tri, onehot) + cnt_sc[...]
    rank1 = jnp.sum(jnp.where(hit1, pref, 0.0), axis=-1, keepdims=True) - 1.0
    rank2 = jnp.sum(jnp.where(hit2, pref, 0.0), axis=-1, keepdims=True) - 1.0
    cnt_sc[...] = pref[tm - 1:tm, :]
    e1 = (i1 - N_GROUPS).astype(jnp.int32)
    e2 = (i2 - N_GROUPS).astype(jnp.int32)
    pk1 = (e1 << 16) | rank1.astype(jnp.int32)
    pk2 = (e2 << 16) | rank2.astype(jnp.int32)
    lane_i = lax.broadcasted_iota(jnp.int32, (tm, LANES), 1)
    ri_ref[...] = jnp.where(lane_i == 0, pk1, jnp.where(lane_i == 1, pk2, 0))
    rg_ref[...] = jnp.where(lane_i == 0, gate1, jnp.where(lane_i == 1, gate2, 0.0))

    @pl.when(i == n - 1)
    def _():
        cnt = cnt_sc[...]
        nblk = jnp.floor((cnt + (BM - 1.0)) * (1.0 / BM))
        l8 = lax.broadcasted_iota(jnp.int32, (8, LANES), 1)
        is_exp = (l8 >= N_GROUPS) & (l8 < N_GROUPS + N_EXPERTS)
        nblk8 = jnp.where(is_exp, jnp.broadcast_to(nblk, (8, LANES)), 0.0)
        r2 = lax.broadcasted_iota(jnp.int32, (LANES, LANES), 0)
        c2 = lax.broadcasted_iota(jnp.int32, (LANES, LANES), 1)
        upper = jnp.where(r2 <= c2, 1.0, 0.0).astype(BF16)
        hi, lo_ = _split_bf16(nblk8)
        blk_end = _dot(hi, upper) + _dot(lo_, upper)
        blk_start = blk_end - nblk8
        row8 = lax.broadcasted_iota(jnp.int32, (8, LANES), 0)
        tab = jnp.where(row8 == 0, blk_start * BM, jnp.where(row8 == 1, blk_end, cnt))
        tab_ref[...] = tab.astype(jnp.int32)
        bi = lax.broadcasted_iota(jnp.int32, (nb_max, LANES), 0).astype(F32)
        ends = blk_end[0:1, :]
        lb = lax.broadcasted_iota(jnp.int32, (nb_max, LANES), 1)
        done = jnp.where((lb >= N_GROUPS) & (lb < N_GROUPS + N_EXPERTS) & (ends <= bi), 1.0, 0.0)
        be = jnp.minimum(jnp.sum(done, axis=-1, keepdims=True), N_EXPERTS - 1.0)
        blk_ref[...] = jnp.broadcast_to(be, (nb_max, LANES)).astype(jnp.int32)


def _route_call(logits, nb_max):
    t = logits.shape[0]
    return pl.pallas_call(
        functools.partial(_route_kernel, nb_max=nb_max),
        grid=(t // TM,),
        in_specs=[pl.BlockSpec((TM, LANES), lambda i: (i, 0))],
        out_specs=[pl.BlockSpec((TM, LANES), lambda i: (i, 0)),
                   pl.BlockSpec((TM, LANES), lambda i: (i, 0)),
                   pl.BlockSpec((8, LANES), lambda i: (0, 0)),
                   pl.BlockSpec((nb_max, LANES), lambda i: (0, 0))],
        out_shape=[jax.ShapeDtypeStruct((t, LANES), jnp.int32),
                   jax.ShapeDtypeStruct((t, LANES), F32),
                   jax.ShapeDtypeStruct((8, LANES), jnp.int32),
                   jax.ShapeDtypeStruct((nb_max, LANES), jnp.int32)],
        scratch_shapes=[pltpu.VMEM((1, LANES), F32)],
        compiler_params=_cparams(("arbitrary",)),
        name="moe_route",
    )(logits)


def _slots_kernel(ri_ref, tab_ref, o_ref):
    tm = ri_ref.shape[0]
    lane = lax.broadcasted_iota(jnp.int32, (tm, LANES), 1)
    starts = tab_ref[0:1, :].astype(F32)
    pk = ri_ref[...]
    out = jnp.zeros((tm, LANES), jnp.int32)
    for k in range(2):
        pkk = pk[:, k:k + 1]
        first = jnp.sum(jnp.where(lane == (pkk >> 16) + N_GROUPS, starts, 0.0), axis=-1, keepdims=True)
        out = jnp.where(lane == k, first.astype(jnp.int32) + (pkk & 0xFFFF), out)
    o_ref[...] = out


def _slots_call(route_i, tab):
    t = route_i.shape[0]
    tile = next(c for c in (4 * TM, 2 * TM, TM) if t % c == 0)
    return pl.pallas_call(
        _slots_kernel,
        grid=(t // tile,),
        in_specs=[pl.BlockSpec((tile, LANES), lambda i: (i, 0)),
                  pl.BlockSpec((8, LANES), lambda i: (0, 0))],
        out_specs=pl.BlockSpec((tile, LANES), lambda i: (i, 0)),
        out_shape=jax.ShapeDtypeStruct((t, LANES), jnp.int32),
        compiler_params=_cparams(("parallel",)),
        name="moe_slots",
    )(route_i, tab)


N_DISPATCH_BUF = 3


def _dispatch_kernel(dest_ref, ends_ref, h_ref, xs_ref, buf, zbuf, fsem, ssem, zsem, *, min_used):
    i = pl.program_id(0)
    n = pl.num_programs(0)
    tm = buf.shape[1]

    def fetch(tile, slot):
        return pltpu.make_async_copy(h_ref.at[pl.ds(tile * tm, tm), :], buf.at[slot], fsem.at[slot])

    def scatter_wait(slot):
        for _ in range(2):
            pltpu.make_async_copy(buf.at[slot], xs_ref.at[pl.ds(0, tm), :], ssem.at[slot]).wait()

    @pl.when(i == 0)
    def _():
        fetch(0, 0).start()

        @pl.when(n > 1)
        def _():
            fetch(1, 1).start()

        zbuf[...] = jnp.zeros(zbuf.shape, zbuf.dtype)
        n_blocks = xs_ref.shape[0] // BM
        lane0 = N_GROUPS - 1
        n_used = ends_ref[lane0 + N_EXPERTS]
        targets = [(ends_ref[lane0 + e + 1] > ends_ref[lane0 + e], ends_ref[lane0 + e + 1] - 1)
                   for e in range(N_EXPERTS)]
        targets += [(n_used + j < n_blocks, n_used + j) for j in range(n_blocks - min_used)]
        for wait in (False, True):
            for cond, blk in targets:
                @pl.when(cond)
                def _():
                    cp = pltpu.make_async_copy(zbuf, xs_ref.at[pl.ds(blk * BM, BM), :], zsem)
                    cp.wait() if wait else cp.start()

    slot = i % N_DISPATCH_BUF
    fetch(i, slot).wait()
    base = 2 * i * tm
    for s in range(N_DISPATCH_BUF):
        @pl.when(slot == s)
        def _():
            for r in range(tm):
                for k in range(2):
                    dst = dest_ref[base + (2 * r + k)]
                    pltpu.make_async_copy(buf.at[s, pl.ds(r, 1), :], xs_ref.at[pl.ds(dst, 1), :],
                                          ssem.at[s]).start(priority=k)

    @pl.when(i >= 1)
    def _():
        scatter_wait((i - 1) % N_DISPATCH_BUF)

    @pl.when(i + 2 < n)
    def _():
        fetch(i + 2, (i + 2) % N_DISPATCH_BUF).start()

    @pl.when(i == n - 1)
    def _():
        scatter_wait(slot)


def _dispatch_call(dest, block_ends, h, n_blocks):
    t, dw = h.shape
    return pl.pallas_call(
        functools.partial(_dispatch_kernel, min_used=2 * t // BM),
        grid_spec=pltpu.PrefetchScalarGridSpec(
            num_scalar_prefetch=2,
            grid=(t // TM,),
            in_specs=[pl.BlockSpec(memory_space=pl.ANY)],
            out_specs=pl.BlockSpec(memory_space=pl.ANY),
            scratch_shapes=[pltpu.VMEM((N_DISPATCH_BUF, TM, dw), h.dtype),
                            pltpu.VMEM((BM, dw), h.dtype),
                            pltpu.SemaphoreType.DMA((N_DISPATCH_BUF,)),
                            pltpu.SemaphoreType.DMA((N_DISPATCH_BUF,)),
                            pltpu.SemaphoreType.DMA],
        ),
        out_shape=jax.ShapeDtypeStruct((n_blocks * BM, dw), h.dtype),
        compiler_params=_cparams(("arbitrary",)),
        name="moe_dispatch",
    )(dest, block_ends, h)


def _expert_kernel(blk_ref, tab_ref, xs_ref, wg_hbm, wu_hbm, wd_hbm, y_ref, wg_f32, wu_f32, wd_f32,
                   wg_sc, wu_sc, wd_sc, n_loaded, sem, *, layer):
    i = pl.program_id(0)
    n_used = tab_ref[LANES + N_GROUPS + N_EXPERTS - 1]
    last = blk_ref.shape[0] - 1
    e = blk_ref[i]

    def fetch(expert, slot):
        return [pltpu.make_async_copy(src.at[layer, expert], dst.at[slot], sem.at[slot])
                for src, dst in ((wg_hbm, wg_f32), (wu_hbm, wu_f32), (wd_hbm, wd_f32))]

    @pl.when((i == 0) & (n_used > 0))
    def _():
        n_loaded[0] = 0
        for cp in fetch(e, 0):
            cp.start()

    @pl.when((i < n_used) & ((i == 0) | (e != blk_ref[jnp.maximum(i - 1, 0)])))
    def _():
        slot = n_loaded[0] % 2
        for cp in fetch(e, slot):
            cp.wait()
        nxt = lax.while_loop(lambda j: (j < n_used) & (blk_ref[jnp.minimum(j, last)] == e), lambda j: j + 1, i + 1)

        @pl.when(nxt < n_used)
        def _():
            for cp in fetch(blk_ref[jnp.minimum(nxt, last)], 1 - slot):
                cp.start()

        wg_sc[...] = wg_f32[slot].astype(BF16)
        wu_sc[...] = wu_f32[slot].astype(BF16)
        wd_sc[...] = wd_f32[slot].astype(BF16)
        n_loaded[0] = n_loaded[0] + 1

    bm = xs_ref.shape[0]
    quarter = bm // 4
    before = tab_ref[LANES + N_GROUPS - 1 + e]
    valid = tab_ref[2 * LANES + N_GROUPS + e] - (i - before) * bm
    n_quarters = jnp.clip((valid + quarter - 1) // quarter, 1, 4)
    for nq in range(1, 5):
        @pl.when((i < n_used) & (n_quarters == nq))
        def _():
            rows = nq * quarter
            x = _unpack_bf16_pairs(xs_ref[0:rows, :])
            g = _dot(x, wg_sc[...])
            u = _dot(x, wu_sc[...])
            hdn = (g * (1.0 / (1.0 + jnp.exp(-g)))) * u
            y_ref[0:rows, :] = _dot(hdn.astype(BF16), wd_sc[...])
            if rows < bm:
                y_ref[rows:, :] = jnp.zeros((bm - rows, y_ref.shape[1]), F32)

    @pl.when(i >= n_used)
    def _():
        y_ref[...] = jnp.zeros(y_ref.shape, F32)


def _expert_call(blk_expert, tab_flat, xs, wg, wu, wd, layer):
    p, dw = xs.shape
    d, ff = wg.shape[-2:]
    return pl.pallas_call(
        functools.partial(_expert_kernel, layer=layer),
        grid_spec=pltpu.PrefetchScalarGridSpec(
            num_scalar_prefetch=2,
            grid=(p // BM,),
            in_specs=[pl.BlockSpec((BM, dw), lambda i, b, n: (i, 0)),
                      pl.BlockSpec(memory_space=pl.ANY),
                      pl.BlockSpec(memory_space=pl.ANY),
                      pl.BlockSpec(memory_space=pl.ANY)],
            out_specs=pl.BlockSpec((BM, d), lambda i, b, n: (i, 0)),
            scratch_shapes=[pltpu.VMEM((2, d, ff), F32), pltpu.VMEM((2, d, ff), F32), pltpu.VMEM((2, ff, d), F32),
                            pltpu.VMEM((d, ff), BF16), pltpu.VMEM((d, ff), BF16), pltpu.VMEM((ff, d), BF16),
                            pltpu.SMEM((1,), jnp.int32), pltpu.SemaphoreType.DMA((2,))],
        ),
        out_shape=jax.ShapeDtypeStruct((p, d), F32),
        compiler_params=_cparams(("arbitrary",)),
        name="moe_experts",
    )(blk_expert, tab_flat, xs, wg, wu, wd)


def _combine_kernel(dest_ref, x_ref, rg_ref, mod_ref, y_ref, *rest, nt_c):
    out_refs, (buf, sem) = rest[:-2], rest[-2:]
    j = pl.program_id(0)
    n_tiles = pl.num_programs(0) - 1
    tm = x_ref.shape[0]

    for s in range(2):
        @pl.when((j < n_tiles) & (j % 2 == s))
        def _():
            base = 2 * j * tm
            for r in range(tm):
                for k in range(2):
                    src = dest_ref[base + (2 * r + k)]
                    pltpu.make_async_copy(y_ref.at[pl.ds(src, 1), :], buf.at[s, k, pl.ds(r, 1), :],
                                          sem.at[s]).start(priority=k)

    @pl.when(j >= 1)
    def _():
        slot = (j - 1) % 2
        for k in range(2):
            pltpu.make_async_copy(y_ref.at[pl.ds(0, tm), :], buf.at[slot, k], sem.at[slot]).wait()
        rg = rg_ref[...]
        y = buf[slot, 0] * rg[:, 0:1] + buf[slot, 1] * rg[:, 1:2]
        res = x_ref[...] + mod_ref[0, 5:6, :] * y
        if len(out_refs) == 1:
            out_refs[0][...] = res
        else:
            @pl.when(j - 1 < nt_c)
            def _():
                out_refs[0][...] = res

            @pl.when(j - 1 >= nt_c)
            def _():
                out_refs[1][...] = res


def _combine_call(dest, x_new, route_g, mods, yb, mod_map, nt_c, split):
    t, d = x_new.shape
    prev = lambda j: jnp.maximum(j - 1, 0)
    if split:
        out_specs = [pl.BlockSpec((TM, d), lambda j, p: (jnp.minimum(prev(j), nt_c - 1), 0)),
                     pl.BlockSpec((TM, d), lambda j, p: (jnp.maximum(prev(j) - nt_c, 0), 0))]
        out_shape = [jax.ShapeDtypeStruct((nt_c * TM, d), F32), jax.ShapeDtypeStruct((t - nt_c * TM, d), F32)]
    else:
        out_specs = [pl.BlockSpec((TM, d), lambda j, p: (prev(j), 0))]
        out_shape = [jax.ShapeDtypeStruct((t, d), F32)]
    return pl.pallas_call(
        functools.partial(_combine_kernel, nt_c=nt_c),
        grid_spec=pltpu.PrefetchScalarGridSpec(
            num_scalar_prefetch=1,
            grid=(t // TM + 1,),
            in_specs=[pl.BlockSpec((TM, d), lambda j, p: (prev(j), 0)),
                      pl.BlockSpec((TM, LANES), lambda j, p: (prev(j), 0)),
                      pl.BlockSpec((1, 6, d), lambda j, p: mod_map(prev(j))),
                      pl.BlockSpec(memory_space=pl.ANY)],
            out_specs=out_specs,
            scratch_shapes=[pltpu.VMEM((2, 2, TM, d), F32), pltpu.SemaphoreType.DMA((2,))],
        ),
        out_shape=out_shape,
        compiler_params=_cparams(("arbitrary",)),
        name="moe_combine",
    )(dest, x_new, route_g, mods, yb)


def _head_cols(base, heads):
    return np.concatenate([base + h * HEAD_DIM + np.arange(HEAD_DIM) for h in heads])


def _layouts():
    wa, wb, wc = HA * HEAD_DIM, HB * HEAD_DIM, HC * HEAD_DIM
    qa0, ka0, va0 = 0, wa, 2 * wa
    qb0 = 3 * wa
    kb0 = qb0 + wb
    vb0 = kb0 + KVB * HEAD_DIM
    qc0 = vb0 + KVB * HEAD_DIM
    kc0 = qc0 + wc
    vc0 = kc0 + KVC * HEAD_DIM
    b_heads = [h for c in range(4) for h in (c, c + 4)]
    c_heads = [0, 2, 1, 3]
    q_cols = np.concatenate([_head_cols(qb0, b_heads), _head_cols(qa0, range(HA)), _head_cols(qc0, c_heads)])
    k_cols = np.concatenate([ka0 + np.arange(wa), kb0 + np.arange(KVB * HEAD_DIM), kc0 + np.arange(KVC * HEAD_DIM)])
    v_cols = np.concatenate([va0 + np.arange(wa), vb0 + np.arange(KVB * HEAD_DIM), vc0 + np.arange(KVC * HEAD_DIM)])
    in_cols = np.concatenate([q_cols, k_cols, v_cols])
    gain_idx = np.concatenate([np.full(HB, 2), np.full(HA, 0), np.full(HC, 4),
                               np.full(HA, 1), np.full(KVB, 3), np.full(KVC, 5)])
    gain_scale = np.concatenate([np.full(HB + HA + HC, HEAD_DIM ** -0.5 * LOG2E), np.ones(HA + KVB + KVC)])
    return in_cols, gain_idx, gain_scale.astype(np.float32), _head_cols(0, b_heads), _head_cols(0, c_heads)


def _rope_tables(seq):
    quarter = HEAD_DIM // 4
    t = jnp.arange(seq)
    rows, cols = t // GRID_W, t % GRID_W
    freqs = ROPE_THETA ** (-jnp.arange(quarter, dtype=F32) / quarter)
    ang_r = rows.astype(F32)[:, None] * freqs[None, :]
    ang_c = cols.astype(F32)[:, None] * freqs[None, :]
    cos = jnp.concatenate([jnp.cos(ang_r), jnp.cos(ang_r), jnp.cos(ang_c), jnp.cos(ang_c)], axis=1)
    sin = jnp.concatenate([-jnp.sin(ang_r), jnp.sin(ang_r), -jnp.sin(ang_c), jnp.sin(ang_c)], axis=1)
    return jnp.tile(cos, (1, 2)), jnp.tile(sin, (1, 2))


def kernel(x_prompt, x_sample, cache_ctx_k, cache_ctx_v, c, c_ctx, norm_mix, norm_ffn, w_ada, b_ada, w_in, qk_gain,
           rpb, sink, w_br_a, w_br_b, w_br_c, w_gate, w_out, w_route_group, w_route_expert, w_exp_gate, w_exp_up,
           w_exp_down):
    batch, seq_c, d = x_prompt.shape
    nb, seq_d, _ = x_sample.shape
    depth = w_in.shape[0]
    t_c, t_d = batch * seq_c, nb * seq_d
    t = t_c + t_d
    n_ctx = cache_ctx_k.shape[2]
    assert t_c % TM == 0 and seq_d % TM == 0 and TM % seq_c == 0 and nb + 1 <= 8 and 2 * t < (1 << 16)
    nt_c = t_c // TM
    tiles_per_seq = seq_d // TM
    nb_max = 2 * t // BM + N_EXPERTS
    nb_max = (nb_max + 7) // 8 * 8

    in_cols, gain_idx, gain_scale, ob_rows, oc_rows = _layouts()
    x_c, x_d, x_d_off = x_prompt.reshape(t_c, d), x_sample.reshape(t_d, d), 0
    ones_ctx = jnp.ones((nb, n_ctx, LANES), BF16)
    cvec = jnp.concatenate([c_ctx[None, :], c, jnp.zeros((8 - 1 - nb, d), F32)], axis=0)
    mods = _ada_call(cvec, w_ada, b_ada).reshape(depth, 8, 6, d)
    rope_tabs = _rope_tables(seq_d)
    bias_tabs = _bias_call(rpb)

    def mod_map(i):
        return (jnp.where(i < nt_c, 0, 1 + (i - nt_c) // tiles_per_seq), 0, 0)

    prev_kv = []
    for l in range(depth):
        w_in_p = w_in[l][:, in_cols].astype(BF16)
        gain_row = (qk_gain[l][gain_idx] * gain_scale[:, None]).reshape(1, -1)
        gain_row = jnp.concatenate([gain_row, jnp.ones((1, KV_COLS), F32)], axis=1)
        nm = norm_mix[l].reshape(1, d)
        nf = norm_ffn[l].reshape(1, d)

        q_c, k_c, v_c, k32, v32 = _qkv_call(x_c, mods[l], nm, w_in_p, gain_row, tile_off=0, n_tok=t_c,
                                            seq_len=seq_c, row_off=0, per_seq_mod=False, rope_tabs=None, emit_f32=True,
                                            prev_kv=tuple(prev_kv) if l == depth - 1 else ())
        prev_kv += [k32, v32]
        q_d, k_d, v_d = _qkv_call(x_d, mods[l], nm, w_in_p, gain_row, tile_off=x_d_off, n_tok=t_d,
                                  seq_len=seq_d, row_off=1, per_seq_mod=True, rope_tabs=rope_tabs, emit_f32=False)
        k_d = k_d.reshape(nb, seq_d, KV_COLS)
        v_d = v_d.reshape(nb, seq_d, V_EXT_COLS)

        o_ctx = _ctx_attn_call(sink[l], q_c, k_c, v_c, seq_c)
        ck = cache_ctx_k[:, l].reshape(nb, n_ctx, KV_COLS).astype(BF16)
        cv = cache_ctx_v[:, l].reshape(nb, n_ctx, KV_COLS).astype(BF16)
        b0, b1 = HA * HEAD_DIM, (HA + KVB) * HEAD_DIM
        cv = jnp.concatenate([cv[:, :, :b1], ones_ctx, cv[:, :, b1:], ones_ctx], axis=2)
        o_b = _dec_b_call(q_d, jnp.concatenate([k_d[:, :, b0:b1], ck[:, :, b0:b1]], axis=1),
                          jnp.concatenate([v_d[:, :, b0:b1 + LANES], cv[:, :, b0:b1 + LANES]], axis=1))
        o_c = _dec_c_call(sink[l], q_d, k_d, v_d, ck, cv)
        o_a = _dec_a_call(q_d, k_d, v_d, ck, cv, bias_tabs[l])

        w_route = jnp.concatenate([w_route_group[l], w_route_expert[l],
                                   jnp.zeros((d, LANES - N_GROUPS - N_EXPERTS), F32)], axis=1)
        wr_hi = w_route.astype(BF16)
        wr = jnp.concatenate([wr_hi, (w_route - wr_hi.astype(F32)).astype(BF16)], axis=1)
        x_new, h, logits = _merge_call(
            x_c, x_d, x_d_off, o_ctx, o_b, o_a, o_c, mods[l], nm, nf, w_gate[l].astype(BF16), w_br_b[l][ob_rows].astype(BF16),
            w_br_a[l].astype(BF16), w_br_c[l][oc_rows].astype(BF16), w_out[l].astype(BF16), wr, mod_map)

        route_i, route_g, tab, blk = _route_call(logits, nb_max)
        dest = _slots_call(route_i, tab)[:, :2].reshape(-1)
        xs = _dispatch_call(dest, tab[1], h, nb_max)
        yb = _expert_call(blk[:, 0], tab.reshape(-1), xs, w_exp_gate, w_exp_up, w_exp_down, l)
        outs = _combine_call(dest, x_new, route_g, mods[l], yb, mod_map, nt_c, split=(l == depth - 1))
        if l < depth - 1:
            x_c = x_d = outs[0]
            x_d_off = nt_c

    y_prompt = outs[0].reshape(batch, seq_c, d)
    y_sample = outs[1].reshape(nb, seq_d, d)
    new_k, new_v = (a.reshape(batch, depth, seq_c, HA + KVB + KVC, HEAD_DIM) for a in prev_kv[-2:])
    return (y_prompt, y_sample, new_k, new_v)
```

```python
import functools

import jax
import jax.numpy as jnp
import numpy as np
from jax import lax
from jax.experimental import pallas as pl
from jax.experimental.pallas import tpu as pltpu

GRID_W = 64
HEAD_DIM = 64
HA, HB, KVB, HC, KVC = 4, 8, 2, 4, 2
NA_WIN_R, NA_WIN_C = 8, 16
WINDOW = 128
ROPE_THETA = 10000.0
N_GROUPS, EXPERTS_PER_GROUP, N_EXPERTS = 4, 8, 32
QB_COLS, QA_COLS, QC_COLS = HB * HEAD_DIM, HA * HEAD_DIM, HC * HEAD_DIM
Q_COLS = QB_COLS + QA_COLS + QC_COLS
KV_COLS = (HA + KVB + KVC) * HEAD_DIM
EPS = 1e-6
NEG = -1e30
LOG2E = 1.4426950408889634

LANES = 128
V7X_VMEM_LIMIT = 56 * 1024 * 1024

TM = 512
TQ_B = 256
TK_B = 256
TQ_C = 256
ROWS_A = 8
BM = 512
ROW_CHAINS = 2

F32 = jnp.float32
BF16 = jnp.bfloat16


def _cparams(sem, vmem=V7X_VMEM_LIMIT):
    return pltpu.CompilerParams(dimension_semantics=sem, vmem_limit_bytes=vmem)


def _split_bf16(x):
    hi = x.astype(BF16)
    lo = (x - hi.astype(F32)).astype(BF16)
    return hi, lo


def _pack_bf16_pairs(x_bf16):
    n = x_bf16.shape[1] // 2
    lo = lax.bitcast_convert_type(x_bf16[:, :n].astype(F32), jnp.uint32)
    hi = lax.bitcast_convert_type(x_bf16[:, n:].astype(F32), jnp.uint32)
    return (lo >> 16) | hi


def _unpack_bf16_pairs(w_u32):
    lo = lax.bitcast_convert_type(w_u32 << 16, F32)
    hi = lax.bitcast_convert_type(w_u32 & jnp.uint32(0xFFFF0000), F32)
    return jnp.concatenate([lo, hi], axis=1).astype(BF16)


def _dot(a, b):
    return jnp.dot(a, b, preferred_element_type=F32)


def _dot_nt(a, b):
    return lax.dot_general(a, b, (((1,), (1,)), ((), ())), preferred_element_type=F32)


def _half_masks(dtype):
    lane = lax.broadcasted_iota(jnp.int32, (1, LANES), 1)
    lo = (lane < HEAD_DIM)
    return lo, jnp.where(lo, 1.0, 0.0).astype(dtype), jnp.where(lo, 0.0, 1.0).astype(dtype)


def _ada_kernel(c_ref, w_ref, b_ref, o_ref):
    c = c_ref[...]
    s = c * (1.0 / (1.0 + jnp.exp(-c)))
    s_hi, s_lo = _split_bf16(s)
    w_hi, w_lo = _split_bf16(w_ref[...])
    o_ref[...] = _dot(s_hi, w_hi) + _dot(s_lo, w_hi) + _dot(s_hi, w_lo) + b_ref[...]


def _ada_call(cvec, w_ada, b_ada):
    depth, d, n6 = w_ada.shape
    tn = 512
    return pl.pallas_call(
        _ada_kernel,
        grid=(depth, n6 // tn),
        in_specs=[
            pl.BlockSpec((8, d), lambda l, j: (0, 0)),
            pl.BlockSpec((None, d, tn), lambda l, j: (l, 0, j)),
            pl.BlockSpec((None, 1, tn), lambda l, j: (l, 0, j)),
        ],
        out_specs=pl.BlockSpec((None, 8, tn), lambda l, j: (l, 0, j)),
        out_shape=jax.ShapeDtypeStruct((depth, 8, n6), F32),
        compiler_params=_cparams(("arbitrary", "arbitrary")),
        name="ada_modulation",
    )(cvec, w_ada, b_ada.reshape(depth, 1, n6))


N_QCHUNK = 8
N_KCHUNK = 4
ROPE_QCHUNKS = (0, 1, 2, 3, 6, 7)
ROPE_KCHUNKS = (2, 3)
V_EXT_COLS = 6 * LANES


def _modulated_norm(x, gain, scale, shift):
    ms = jnp.mean(x * x, axis=-1, keepdims=True)
    return (x * lax.rsqrt(ms + EPS) * gain) * (1.0 + scale) + shift


def _qkv_kernel(*refs, rope, emit_f32, n_prev):
    x_ref, mod_ref, nm_ref, w_ref, gain_ref = refs[:5]
    pos = 5
    if rope:
        cos_ref, sin_ref, bd_ref = refs[pos:pos + 3]
        pos += 3
    prev_refs = refs[pos:pos + 2 * n_prev]
    pos += 2 * n_prev
    q_ref, k_ref, v_ref = refs[pos:pos + 3]
    pos += 3
    if emit_f32:
        k32_ref, v32_ref = refs[pos:pos + 2]

    def put_f32(ref, cols, val):
        if n_prev:
            ref[:, n_prev, :, cols] = val.reshape(ref.shape[0], ref.shape[2], val.shape[-1])
        else:
            ref[:, cols] = val

    for l in range(n_prev):
        for src, dst in ((prev_refs[2 * l], k32_ref), (prev_refs[2 * l + 1], v32_ref)):
            dst[:, l] = src[...].reshape(dst.shape[0], dst.shape[2], dst.shape[3])

    lane = lax.broadcasted_iota(jnp.int32, (1, LANES), 1)
    lo = lane < HEAD_DIM
    first = (lane % 32) < 16
    tm = x_ref.shape[0]
    n_chains = ROW_CHAINS if rope else 1

    for c in range(n_chains):
        rows = slice(c * tm // n_chains, (c + 1) * tm // n_chains)
        u = _modulated_norm(x_ref[rows, :], nm_ref[...], mod_ref[0, 1:2, :], mod_ref[0, 0:1, :])
        y = _dot(u.astype(BF16), w_ref[...])
        if rope:
            cos_t = cos_ref[rows, :]
            sin_t = sin_ref[rows, :]
        pair_sumsq = {}

        def head_sumsq(j):
            yj = y[:, j * LANES:(j + 1) * LANES]
            if not rope:
                sq = yj * yj
                s_lo = jnp.sum(jnp.where(lo, sq, 0.0), axis=-1, keepdims=True)
                s_hi = jnp.sum(jnp.where(lo, 0.0, sq), axis=-1, keepdims=True)
                return jnp.where(lo, s_lo, s_hi)
            p = j // 2
            if p not in pair_sumsq:
                y2 = y[:, 2 * p * LANES:(2 * p + 2) * LANES]
                sq_hi, sq_lo = _split_bf16(y2 * y2)
                pair_sumsq[p] = _dot(sq_hi, bd_ref[...]) + _dot(sq_lo, bd_ref[...])
            return pair_sumsq[p][:, (j % 2) * LANES:(j % 2 + 1) * LANES]

        def norm_chunk(j, use_rope):
            yj = y[:, j * LANES:(j + 1) * LANES]
            inv = lax.rsqrt(head_sumsq(j) * (1.0 / HEAD_DIM) + EPS)
            yj = (yj * inv) * gain_ref[:, j * LANES:(j + 1) * LANES]
            if use_rope:
                partner = jnp.where(first, pltpu.roll(yj, LANES - 16, axis=1), pltpu.roll(yj, 16, axis=1))
                yj = yj * cos_t + partner * sin_t
            return yj

        for j in range(N_QCHUNK):
            yj = norm_chunk(j, rope and j in ROPE_QCHUNKS)
            q_ref[rows, j * LANES:(j + 1) * LANES] = yj.astype(BF16)
        for j in range(N_KCHUNK):
            yj = norm_chunk(N_QCHUNK + j, rope and j in ROPE_KCHUNKS)
            k_ref[rows, j * LANES:(j + 1) * LANES] = yj.astype(BF16)
            if emit_f32:
                put_f32(k32_ref, slice(j * LANES, (j + 1) * LANES), yj)
        v = y[:, (N_QCHUNK + N_KCHUNK) * LANES:]
        if rope:
            ones = jnp.ones((v.shape[0], LANES), BF16)
            v_ref[rows, :3 * LANES] = v[:, :3 * LANES].astype(BF16)
            v_ref[rows, 3 * LANES:4 * LANES] = ones
            v_ref[rows, 4 * LANES:5 * LANES] = v[:, 3 * LANES:].astype(BF16)
            v_ref[rows, 5 * LANES:] = ones
        else:
            v_ref[...] = v.astype(BF16)
        if emit_f32:
            put_f32(v32_ref, slice(None), v)


def _qkv_call(x_all, mods, norm_mix_l, w_in_p, gain_row, *, tile_off, n_tok, seq_len, row_off,
              per_seq_mod, rope_tabs, emit_f32, prev_kv=()):
    d = x_all.shape[1]
    nt = n_tok // TM
    tiles_per_seq = max(seq_len // TM, 1)
    rope = rope_tabs is not None
    n_prev = len(prev_kv) // 2

    def mod_map(i):
        return (row_off + i // tiles_per_seq, 0, 0) if per_seq_mod else (row_off, 0, 0)

    in_specs = [
        pl.BlockSpec((TM, d), lambda i: (i + tile_off, 0)),
        pl.BlockSpec((1, 6, d), mod_map),
        pl.BlockSpec((1, d), lambda i: (0, 0)),
        pl.BlockSpec(w_in_p.shape, lambda i: (0, 0)),
        pl.BlockSpec((1, Q_COLS + 2 * KV_COLS), lambda i: (0, 0)),
    ]
    args = [x_all, mods, norm_mix_l, w_in_p, gain_row]
    if rope:
        in_specs += [pl.BlockSpec((TM, LANES), lambda i: (i % tiles_per_seq, 0))] * 2
        in_specs += [pl.BlockSpec((2 * LANES, 2 * LANES), lambda i: (0, 0))]
        head_of_lane = np.arange(2 * LANES) // HEAD_DIM
        args += list(rope_tabs) + [jnp.asarray(head_of_lane[:, None] == head_of_lane[None, :], BF16)]
    v_cols = V_EXT_COLS if rope else KV_COLS
    out_shape = [jax.ShapeDtypeStruct((n_tok, Q_COLS), BF16),
                 jax.ShapeDtypeStruct((n_tok, KV_COLS), BF16),
                 jax.ShapeDtypeStruct((n_tok, v_cols), BF16)]
    out_specs = [pl.BlockSpec((TM, Q_COLS), lambda i: (i, 0)),
                 pl.BlockSpec((TM, KV_COLS), lambda i: (i, 0)),
                 pl.BlockSpec((TM, v_cols), lambda i: (i, 0))]
    in_specs += [pl.BlockSpec((TM, KV_COLS), lambda i: (i, 0))] * (2 * n_prev)
    args += list(prev_kv)
    if emit_f32 and n_prev:
        req = TM // seq_len
        out_shape += [jax.ShapeDtypeStruct((n_tok // seq_len, n_prev + 1, seq_len, KV_COLS), F32)] * 2
        out_specs += [pl.BlockSpec((req, n_prev + 1, seq_len, KV_COLS), lambda i: (i, 0, 0, 0))] * 2
    elif emit_f32:
        out_shape += [jax.ShapeDtypeStruct((n_tok, KV_COLS), F32)] * 2
        out_specs += [pl.BlockSpec((TM, KV_COLS), lambda i: (i, 0))] * 2
    return pl.pallas_call(
        functools.partial(_qkv_kernel, rope=rope, emit_f32=emit_f32, n_prev=n_prev),
        grid=(nt,),
        in_specs=in_specs,
        out_specs=out_specs,
        out_shape=out_shape,
        compiler_params=_cparams(("parallel",)),
        name="qkv_rope" if rope else "qkv_ctx",
    )(*args)


def _softmax_rows(s, sink_col=None):
    m = jnp.max(s, axis=-1, keepdims=True)
    if sink_col is not None:
        m = jnp.maximum(m, sink_col)
    p = jnp.exp2(s - m)
    l = jnp.sum(p, axis=-1, keepdims=True)
    if sink_col is not None:
        l = l + jnp.exp2(sink_col - m)
    return p, l


def _ctx_attn_kernel(sink_ref, q_ref, k_ref, v_ref, o_ref):
    s_len = q_ref.shape[0]
    _, m_lo, m_hi = _half_masks(BF16)
    lo = lax.broadcasted_iota(jnp.int32, (1, LANES), 1) < HEAD_DIM

    def group(q_chunks, kc, sinks=None):
        n = len(q_chunks)
        k = k_ref[:, kc * LANES:(kc + 1) * LANES]
        v = v_ref[:, kc * LANES:(kc + 1) * LANES]
        qs = jnp.concatenate([q_ref[:, c * LANES:(c + 1) * LANES] * m for m in (m_lo, m_hi) for c in q_chunks],
                             axis=0)
        s = _dot_nt(qs, k)
        sink_col = None
        if sinks is not None:
            row = lax.broadcasted_iota(jnp.int32, (2 * n * s_len, 1), 0)
            sink_col = jnp.zeros((2 * n * s_len, 1), F32)
            for idx, sv in enumerate(sinks):
                sink_col = jnp.where((row >= idx * s_len) & (row < (idx + 1) * s_len), sv, sink_col)
        p, l = _softmax_rows(s, sink_col)
        o = _dot(p.astype(BF16), v) / l
        for idx, c in enumerate(q_chunks):
            o_c = jnp.where(lo, o[idx * s_len:(idx + 1) * s_len], o[(n + idx) * s_len:(n + idx + 1) * s_len])
            o_ref[:, c * LANES:(c + 1) * LANES] = o_c.astype(BF16)

    group([0, 1, 2, 3], 2)
    group([4], 0)
    group([5], 1)
    group([6, 7], 3, sinks=[sink_ref[h] * LOG2E for h in range(HC)])


def _ctx_attn_call(sink_l, q, k, v, s_len):
    batch = q.shape[0] // s_len
    return pl.pallas_call(
        _ctx_attn_kernel,
        grid_spec=pltpu.PrefetchScalarGridSpec(
            num_scalar_prefetch=1,
            grid=(batch,),
            in_specs=[pl.BlockSpec((s_len, Q_COLS), lambda b, s: (b, 0)),
                      pl.BlockSpec((s_len, KV_COLS), lambda b, s: (b, 0)),
                      pl.BlockSpec((s_len, KV_COLS), lambda b, s: (b, 0))],
            out_specs=pl.BlockSpec((s_len, Q_COLS), lambda b, s: (b, 0)),
        ),
        out_shape=jax.ShapeDtypeStruct((q.shape[0], Q_COLS), BF16),
        compiler_params=_cparams(("parallel",)),
        name="ctx_attention",
    )(sink_l, q, k, v)


def _dec_b_kernel(q_ref, k_ref, v_ref, o_ref):
    n_keys = k_ref.shape[0]
    n_chunks = n_keys // TK_B
    _, m_lo, m_hi = _half_masks(BF16)
    lo = lax.broadcasted_iota(jnp.int32, (1, LANES), 1) < HEAD_DIM

    def stacked_q(msk):
        return jnp.concatenate([q_ref[:, c * LANES:(c + 1) * LANES] * msk for c in range(4)], axis=0)

    def write(outs):
        for c in range(4):
            o_c = jnp.where(lo, outs[0][c * TQ_B:(c + 1) * TQ_B], outs[1][c * TQ_B:(c + 1) * TQ_B])
            o_ref[:, c * LANES:(c + 1) * LANES] = o_c.astype(BF16)

    outs, sums = [], []
    for msk in (m_lo, m_hi):
        qs = stacked_q(msk)
        s = _dot_nt(qs, k_ref[0:TK_B, :])
        m = jnp.max(s, axis=-1, keepdims=True)
        acc = _dot(jnp.exp2(s - m).astype(BF16), v_ref[0:TK_B, :])
        for t in range(1, n_chunks):
            s = _dot_nt(qs, k_ref[t * TK_B:(t + 1) * TK_B, :])
            acc = acc + _dot(jnp.exp2(s - m).astype(BF16), v_ref[t * TK_B:(t + 1) * TK_B, :])
        outs.append(acc[:, :LANES] / acc[:, LANES:])
        sums.append(jnp.sum(acc * 0.0))
    write(outs)

    @pl.when(jnp.logical_not(sums[0] + sums[1] == 0.0))
    def _():
        exact = []
        for msk in (m_lo, m_hi):
            s = _dot_nt(stacked_q(msk), k_ref[...])
            acc = _dot(jnp.exp2(s - jnp.max(s, axis=-1, keepdims=True)).astype(BF16), v_ref[...])
            exact.append(acc[:, :LANES] / acc[:, LANES:])
        write(exact)


def _dec_b_call(q, k_all, v_all):
    nb, n_keys, _ = k_all.shape
    seq = q.shape[0] // nb
    nq = seq // TQ_B
    return pl.pallas_call(
        _dec_b_kernel,
        grid=(nb, nq),
        in_specs=[pl.BlockSpec((TQ_B, QB_COLS), lambda b, i: (b * nq + i, 0)),
                  pl.BlockSpec((None, n_keys, LANES), lambda b, i: (b, 0, 0)),
                  pl.BlockSpec((None, n_keys, 2 * LANES), lambda b, i: (b, 0, 0))],
        out_specs=pl.BlockSpec((TQ_B, QB_COLS), lambda b, i: (b * nq + i, 0)),
        out_shape=jax.ShapeDtypeStruct((q.shape[0], QB_COLS), BF16),
        compiler_params=_cparams(("parallel", "parallel")),
        name="dec_dense_attention",
    )(q, k_all, v_all)


def _dec_c_kernel(sink_ref, q_ref, k_ref, v_ref, ck_ref, cv_ref, o_ref):
    qi = pl.program_id(1)
    seq = k_ref.shape[0]
    win = TQ_C + 2 * WINDOW
    _, m_lo, m_hi = _half_masks(BF16)
    lo = lax.broadcasted_iota(jnp.int32, (1, LANES), 1) < HEAD_DIM
    start = jnp.clip(qi * TQ_C - WINDOW, 0, seq - win)
    start = pl.multiple_of(start, WINDOW)
    k_win = k_ref[pl.ds(start, win), :]
    v_win = v_ref[pl.ds(start, win), :]
    k_ctx = ck_ref[...]
    v_ctx = cv_ref[...]
    row = lax.broadcasted_iota(jnp.int32, (2 * TQ_C, win), 0)
    col = lax.broadcasted_iota(jnp.int32, (2 * TQ_C, win), 1)
    qpos = qi * TQ_C + jnp.where(row >= TQ_C, row - TQ_C, row)
    ok = jnp.abs(qpos - (start + col)) <= WINDOW
    row1 = lax.broadcasted_iota(jnp.int32, (2 * TQ_C, 1), 0)
    def attend(exact_shift):
        outs, finite = [], []
        for kvh, msk in enumerate((m_lo, m_hi)):
            qs = jnp.concatenate([q_ref[:, c * LANES:(c + 1) * LANES] * msk for c in range(2)], axis=0)
            s_loc = jnp.where(ok, _dot_nt(qs, k_win), NEG)
            s_ctx = _dot_nt(qs, k_ctx)
            sink_col = jnp.where(row1 < TQ_C, sink_ref[2 * kvh] * LOG2E, sink_ref[2 * kvh + 1] * LOG2E)
            m = sink_col
            if exact_shift:
                m = jnp.maximum(jnp.maximum(jnp.max(s_loc, axis=-1, keepdims=True),
                                            jnp.max(s_ctx, axis=-1, keepdims=True)), sink_col)
            acc = (_dot(jnp.exp2(s_loc - m).astype(BF16), v_win)
                   + _dot(jnp.exp2(s_ctx - m).astype(BF16), v_ctx))
            outs.append(acc[:, :LANES] / (acc[:, LANES:] + jnp.exp2(sink_col - m)))
            finite.append(jnp.sum(acc * 0.0))
        for c in range(2):
            o_c = jnp.where(lo, outs[0][c * TQ_C:(c + 1) * TQ_C], outs[1][c * TQ_C:(c + 1) * TQ_C])
            o_ref[:, c * LANES:(c + 1) * LANES] = o_c.astype(BF16)
        return finite[0] + finite[1]

    check = attend(exact_shift=False)

    @pl.when(jnp.logical_not(check == 0.0))
    def _():
        attend(exact_shift=True)


def _dec_c_call(sink_l, q, k_d, v_d, ck, cv):
    nb, seq, _ = k_d.shape
    n_ctx = ck.shape[1]
    nq = seq // TQ_C
    return pl.pallas_call(
        _dec_c_kernel,
        grid_spec=pltpu.PrefetchScalarGridSpec(
            num_scalar_prefetch=1,
            grid=(nb, nq),
            in_specs=[pl.BlockSpec((TQ_C, QC_COLS), lambda b, i, s: (b * nq + i, (QB_COLS + QA_COLS) // QC_COLS)),
                      pl.BlockSpec((None, seq, LANES), lambda b, i, s: (b, 0, 3)),
                      pl.BlockSpec((None, seq, 2 * LANES), lambda b, i, s: (b, 0, 2)),
                      pl.BlockSpec((None, n_ctx, LANES), lambda b, i, s: (b, 0, 3)),
                      pl.BlockSpec((None, n_ctx, 2 * LANES), lambda b, i, s: (b, 0, 2))],
            out_specs=pl.BlockSpec((TQ_C, QC_COLS), lambda b, i, s: (b * nq + i, 0)),
        ),
        out_shape=jax.ShapeDtypeStruct((q.shape[0], QC_COLS), BF16),
        compiler_params=_cparams(("parallel", "parallel")),
        name="dec_window_attention",
    )(sink_l, q, k_d, v_d, ck, cv)


N_DR = 2 * NA_WIN_R - 2


def _bias_kernel(rpb_ref, o_ref):
    lh = pl.program_id(0)
    n_dc = 2 * NA_WIN_C - 1
    qc = lax.broadcasted_iota(jnp.int32, (GRID_W, LANES), 0)
    kk = lax.broadcasted_iota(jnp.int32, (GRID_W, LANES), 1)
    half = kk >= GRID_W
    kc = jnp.where(half, kk - GRID_W, kk)
    delta = jnp.clip(kc - qc, -(NA_WIN_C - 1), NA_WIN_C - 1) + (NA_WIN_C - 1)
    col_start = jnp.clip(qc - NA_WIN_C // 2, 0, GRID_W - NA_WIN_C)
    ok = (kc >= col_start) & (kc < col_start + NA_WIN_C)
    for d in range(N_DR):
        acc = jnp.zeros((GRID_W, LANES), F32)
        base = (lh * (2 * NA_WIN_R - 1) + d) * n_dc
        for dc in range(n_dc):
            val = jnp.where(half, rpb_ref[base + n_dc + dc] * LOG2E, rpb_ref[base + dc] * LOG2E)
            acc = jnp.where(delta == dc, val, acc)
        o_ref[d] = jnp.where(ok, acc, NEG)


def _bias_call(rpb):
    depth = rpb.shape[0]
    return pl.pallas_call(
        _bias_kernel,
        grid_spec=pltpu.PrefetchScalarGridSpec(
            num_scalar_prefetch=1,
            grid=(depth * HA,),
            in_specs=[],
            out_specs=pl.BlockSpec((None, N_DR, GRID_W, LANES), lambda lh, r: (lh, 0, 0, 0)),
        ),
        out_shape=jax.ShapeDtypeStruct((depth * HA, N_DR, GRID_W, LANES), F32),
        compiler_params=_cparams(("arbitrary",)),
        name="na_bias_table",
    )(rpb.reshape(-1)).reshape(depth, HA, N_DR, GRID_W, LANES)


def _dec_a_kernel(q_ref, k_ref, v_ref, ck_ref, cv_ref, bias_ref, o_ref):
    rb = pl.program_id(2)
    rows = k_ref.shape[0] // GRID_W
    n_loc = NA_WIN_R * GRID_W
    _, m_lo, m_hi = _half_masks(BF16)
    lo = lax.broadcasted_iota(jnp.int32, (1, LANES), 1) < HEAD_DIM
    qs = [jnp.concatenate([q_ref[rr * GRID_W:(rr + 1) * GRID_W, :] * msk for msk in (m_lo, m_hi)], axis=0)
          for rr in range(ROWS_A)]
    s_ctx = _dot_nt(jnp.concatenate(qs, axis=0), ck_ref[...])
    s_loc, v_win = [], []
    for rr in range(ROWS_A):
        r = rb * ROWS_A + rr
        start = jnp.clip(r - NA_WIN_R // 2, 0, rows - NA_WIN_R)
        d0 = start - r + (NA_WIN_R - 1)
        off = pl.multiple_of(start * GRID_W, GRID_W)
        bias = jnp.concatenate(
            [jnp.concatenate([bias_ref[hh, d0 + 2 * i] for i in range(NA_WIN_R // 2)], axis=1) for hh in range(2)],
            axis=0)
        s_loc.append(_dot_nt(qs[rr], k_ref[pl.ds(off, n_loc), :]) + bias)
        v_win.append(v_ref[pl.ds(off, n_loc), :])
    m = jnp.maximum(jnp.concatenate([jnp.max(s, axis=-1, keepdims=True) for s in s_loc], axis=0),
                    jnp.max(s_ctx, axis=-1, keepdims=True))
    p_ctx = jnp.exp2(s_ctx - m)
    l_ctx = jnp.sum(p_ctx, axis=-1, keepdims=True)
    o_ctx = _dot(p_ctx.astype(BF16), cv_ref[...])
    for rr in range(ROWS_A):
        sl = slice(rr * 2 * GRID_W, (rr + 1) * 2 * GRID_W)
        p_loc = jnp.exp2(s_loc[rr] - m[sl])
        l = jnp.sum(p_loc, axis=-1, keepdims=True) + l_ctx[sl]
        o = (_dot(p_loc.astype(BF16), v_win[rr]) + o_ctx[sl]) / l
        o_c = jnp.where(lo, o[:GRID_W], o[GRID_W:])
        o_ref[rr * GRID_W:(rr + 1) * GRID_W, :] = o_c.astype(BF16)


def _dec_a_call(q, k_d, v_d, ck, cv, bias_tab):
    nb, seq, _ = k_d.shape
    n_ctx = ck.shape[1]
    tq = ROWS_A * GRID_W
    nq = seq // tq
    return pl.pallas_call(
        _dec_a_kernel,
        grid=(2, nb, nq),
        in_specs=[pl.BlockSpec((tq, LANES), lambda p, b, i: (b * nq + i, 4 + p)),
                  pl.BlockSpec((None, seq, LANES), lambda p, b, i: (b, 0, p)),
                  pl.BlockSpec((None, seq, LANES), lambda p, b, i: (b, 0, p)),
                  pl.BlockSpec((None, n_ctx, LANES), lambda p, b, i: (b, 0, p)),
                  pl.BlockSpec((None, n_ctx, LANES), lambda p, b, i: (b, 0, p)),
                  pl.BlockSpec((2, N_DR, GRID_W, LANES), lambda p, b, i: (p, 0, 0, 0))],
        out_specs=pl.BlockSpec((tq, LANES), lambda p, b, i: (b * nq + i, p)),
        out_shape=jax.ShapeDtypeStruct((q.shape[0], 2 * LANES), BF16),
        compiler_params=_cparams(("parallel", "parallel", "parallel")),
        name="dec_neighbourhood_attention",
    )(q, k_d, v_d, ck, cv, bias_tab)


def _merge_kernel(xc_ref, xd_ref, oc_ref, ob_ref, oa_ref, ow_ref, mod_ref, nm_ref, nf_ref, wg_ref, wbb_ref, wba_ref,
                  wbc_ref, wo_ref, wr_ref, xo_ref, h_ref, lg_ref, *, nt_c):
    d = xc_ref.shape[1]
    tm = xc_ref.shape[0]
    is_ctx = pl.program_id(0) < nt_c
    for c in range(ROW_CHAINS):
        rows = slice(c * tm // ROW_CHAINS, (c + 1) * tm // ROW_CHAINS)
        o_b = jnp.where(is_ctx, oc_ref[rows, :QB_COLS], ob_ref[rows, :])
        o_a = jnp.where(is_ctx, oc_ref[rows, QB_COLS:QB_COLS + QA_COLS], oa_ref[rows, :])
        o_c = jnp.where(is_ctx, oc_ref[rows, QB_COLS + QA_COLS:], ow_ref[rows, :])
        x = jnp.where(is_ctx, xc_ref[rows, :], xd_ref[rows, :])
        u = _modulated_norm(x, nm_ref[...], mod_ref[0, 1:2, :], mod_ref[0, 0:1, :])
        gates = _dot(u.astype(BF16), wg_ref[...])
        gates = 1.0 / (1.0 + jnp.exp(-gates))
        m = (gates[:, :d] * _dot(o_a, wba_ref[...])
             + gates[:, d:2 * d] * _dot(o_b, wbb_ref[...])
             + gates[:, 2 * d:] * _dot(o_c, wbc_ref[...]))
        x_new = x + mod_ref[0, 2:3, :] * _dot(m.astype(BF16), wo_ref[...])
        xo_ref[rows, :] = x_new
        h = _modulated_norm(x_new, nf_ref[...], mod_ref[0, 4:5, :], mod_ref[0, 3:4, :])
        h_hi, h_lo = _split_bf16(h)
        h_ref[rows, :] = _pack_bf16_pairs(h_hi)
        both = _dot(h_hi, wr_ref[...])
        lg_ref[rows, :] = both[:, :LANES] + both[:, LANES:] + _dot(h_lo, wr_ref[:, :LANES])


def _merge_call(x_c, x_d, x_d_tile_off, o_ctx, o_b, o_a, o_c, mods, nm, nf, wg, wbb, wba, wbc, wo, wr, mod_map):
    d = x_c.shape[1]
    nt_c = o_ctx.shape[0] // TM
    t = o_ctx.shape[0] + o_b.shape[0]
    const = lambda i: (0, 0)
    full = lambda a: pl.BlockSpec(a.shape, const)
    ctx_map = lambda i: (jnp.minimum(i, nt_c - 1), 0)
    dec_map = lambda i: (jnp.maximum(i - nt_c, 0), 0)
    return pl.pallas_call(
        functools.partial(_merge_kernel, nt_c=nt_c),
        grid=(t // TM,),
        in_specs=[pl.BlockSpec((TM, d), ctx_map),
                  pl.BlockSpec((TM, d), lambda i: (jnp.maximum(i - nt_c, 0) + x_d_tile_off, 0)),
                  pl.BlockSpec((TM, Q_COLS), ctx_map),
                  pl.BlockSpec((TM, QB_COLS), dec_map),
                  pl.BlockSpec((TM, QA_COLS), dec_map),
                  pl.BlockSpec((TM, QC_COLS), dec_map),
                  pl.BlockSpec((1, 6, d), mod_map),
                  full(nm), full(nf), full(wg), full(wbb), full(wba), full(wbc), full(wo), full(wr)],
        out_specs=[pl.BlockSpec((TM, d), lambda i: (i, 0)),
                   pl.BlockSpec((TM, d // 2), lambda i: (i, 0)),
                   pl.BlockSpec((TM, LANES), lambda i: (i, 0))],
        out_shape=[jax.ShapeDtypeStruct((t, d), F32),
                   jax.ShapeDtypeStruct((t, d // 2), jnp.uint32),
                   jax.ShapeDtypeStruct((t, LANES), F32)],
        compiler_params=_cparams(("parallel",)),
        name="merge_branches",
    )(x_c, x_d, o_ctx, o_b, o_a, o_c, mods, nm, nf, wg, wbb, wba, wbc, wo, wr)


def _route_kernel(lg_ref, ri_ref, rg_ref, tab_ref, blk_ref, cnt_sc, *, nb_max):
    i = pl.program_id(0)
    n = pl.num_programs(0)
    tm = lg_ref.shape[0]

    @pl.when(i == 0)
    def _():
        cnt_sc[...] = jnp.zeros(cnt_sc.shape, F32)

    lg = lg_ref[...]
    lane = lax.broadcasted_iota(jnp.int32, (tm, LANES), 1).astype(F32)
    big = float(LANES)
    neg_inf = -jnp.inf
    gl = jnp.where(lane < N_GROUPS, lg, neg_inf)
    gmax = jnp.max(gl, axis=-1, keepdims=True)
    g_sel = jnp.min(jnp.where(gl == gmax, lane, big), axis=-1, keepdims=True)
    g_w = 1.0 / jnp.sum(jnp.exp(gl - gmax), axis=-1, keepdims=True)
    e_lo = N_GROUPS + EXPERTS_PER_GROUP * g_sel
    el = jnp.where((lane >= e_lo) & (lane < e_lo + EXPERTS_PER_GROUP), lg, neg_inf)
    m1 = jnp.max(el, axis=-1, keepdims=True)
    i1 = jnp.min(jnp.where(el == m1, lane, big), axis=-1, keepdims=True)
    el2 = jnp.where(lane == i1, neg_inf, el)
    m2 = jnp.max(el2, axis=-1, keepdims=True)
    i2 = jnp.min(jnp.where(el2 == m2, lane, big), axis=-1, keepdims=True)
    e2x = jnp.exp(m2 - m1)
    den = 1.0 + e2x
    gate1 = g_w * (1.0 / den)
    gate2 = g_w * (e2x / den)
    hit1 = lane == i1
    hit2 = lane == i2
    onehot = jnp.where(hit1 | hit2, 1.0, 0.0).astype(BF16)
    rr = lax.broadcasted_iota(jnp.int32, (tm, tm), 0)
    cc = lax.broadcasted_iota(jnp.int32, (tm, tm), 1)
    tri = jnp.where(cc <= rr, 1.0, 0.0).astype(BF16)
    pref = _dot(tri, onehot) + cnt_sc[...]
    rank1 = jnp.sum(jnp.where(hit1, pref, 0.0), axis=-1, keepdims=True) - 1.0
    rank2 = jnp.sum(jnp.where(hit2, pref, 0.0), axis=-1, keepdims=True) - 1.0
    cnt_sc[...] = pref[tm - 1:tm, :]
    e1 = (i1 - N_GROUPS).astype(jnp.int32)
    e2 = (i2 - N_GROUPS).astype(jnp.int32)
    pk1 = (e1 << 16) | rank1.astype(jnp.int32)
    pk2 = (e2 << 16) | rank2.astype(jnp.int32)
    lane_i = lax.broadcasted_iota(jnp.int32, (tm, LANES), 1)
    ri_ref[...] = jnp.where(lane_i == 0, pk1, jnp.where(lane_i == 1, pk2, 0))
    rg_ref[...] = jnp.where(lane_i == 0, gate1, jnp.where(lane_i == 1, gate2, 0.0))

    @pl.when(i == n - 1)
    def _():
        cnt = cnt_sc[...]
        nblk = jnp.floor((cnt + (BM - 1.0)) * (1.0 / BM))
        l8 = lax.broadcasted_iota(jnp.int32, (8, LANES), 1)
        is_exp = (l8 >= N_GROUPS) & (l8 < N_GROUPS + N_EXPERTS)
        nblk8 = jnp.where(is_exp, jnp.broadcast_to(nblk, (8, LANES)), 0.0)
        r2 = lax.broadcasted_iota(jnp.int32, (LANES, LANES), 0)
        c2 = lax.broadcasted_iota(jnp.int32, (LANES, LANES), 1)
        upper = jnp.where(r2 <= c2, 1.0, 0.0).astype(BF16)
        hi, lo_ = _split_bf16(nblk8)
        blk_end = _dot(hi, upper) + _dot(lo_, upper)
        blk_start = blk_end - nblk8
        row8 = lax.broadcasted_iota(jnp.int32, (8, LANES), 0)
        tab = jnp.where(row8 == 0, blk_start * BM, jnp.where(row8 == 1, blk_end, cnt))
        tab_ref[...] = tab.astype(jnp.int32)
        bi = lax.broadcasted_iota(jnp.int32, (nb_max, LANES), 0).astype(F32)
        ends = blk_end[0:1, :]
        lb = lax.broadcasted_iota(jnp.int32, (nb_max, LANES), 1)
        done = jnp.where((lb >= N_GROUPS) & (lb < N_GROUPS + N_EXPERTS) & (ends <= bi), 1.0, 0.0)
        be = jnp.minimum(jnp.sum(done, axis=-1, keepdims=True), N_EXPERTS - 1.0)
        blk_ref[...] = jnp.broadcast_to(be, (nb_max, LANES)).astype(jnp.int32)


def _route_call(logits, nb_max):
    t = logits.shape[0]
    return pl.pallas_call(
        functools.partial(_route_kernel, nb_max=nb_max),
        grid=(t // TM,),
        in_specs=[pl.BlockSpec((TM, LANES), lambda i: (i, 0))],
        out_specs=[pl.BlockSpec((TM, LANES), lambda i: (i, 0)),
                   pl.BlockSpec((TM, LANES), lambda i: (i, 0)),
                   pl.BlockSpec((8, LANES), lambda i: (0, 0)),
                   pl.BlockSpec((nb_max, LANES), lambda i: (0, 0))],
        out_shape=[jax.ShapeDtypeStruct((t, LANES), jnp.int32),
                   jax.ShapeDtypeStruct((t, LANES), F32),
                   jax.ShapeDtypeStruct((8, LANES), jnp.int32),
                   jax.ShapeDtypeStruct((nb_max, LANES), jnp.int32)],
        scratch_shapes=[pltpu.VMEM((1, LANES), F32)],
        compiler_params=_cparams(("arbitrary",)),
        name="moe_route",
    )(logits)


def _slots_kernel(ri_ref, tab_ref, o_ref):
    tm = ri_ref.shape[0]
    lane = lax.broadcasted_iota(jnp.int32, (tm, LANES), 1)
    starts = tab_ref[0:1, :].astype(F32)
    pk = ri_ref[...]
    out = jnp.zeros((tm, LANES), jnp.int32)
    for k in range(2):
        pkk = pk[:, k:k + 1]
        first = jnp.sum(jnp.where(lane == (pkk >> 16) + N_GROUPS, starts, 0.0), axis=-1, keepdims=True)
        out = jnp.where(lane == k, first.astype(jnp.int32) + (pkk & 0xFFFF), out)
    o_ref[...] = out


def _slots_call(route_i, tab):
    t = route_i.shape[0]
    tile = next(c for c in (4 * TM, 2 * TM, TM) if t % c == 0)
    return pl.pallas_call(
        _slots_kernel,
        grid=(t // tile,),
        in_specs=[pl.BlockSpec((tile, LANES), lambda i: (i, 0)),
                  pl.BlockSpec((8, LANES), lambda i: (0, 0))],
        out_specs=pl.BlockSpec((tile, LANES), lambda i: (i, 0)),
        out_shape=jax.ShapeDtypeStruct((t, LANES), jnp.int32),
        compiler_params=_cparams(("parallel",)),
        name="moe_slots",
    )(route_i, tab)


N_DISPATCH_BUF = 3


def _dispatch_kernel(dest_ref, ends_ref, h_ref, xs_ref, buf, zbuf, fsem, ssem, zsem, *, min_used):
    i = pl.program_id(0)
    n = pl.num_programs(0)
    tm = buf.shape[1]

    def fetch(tile, slot):
        return pltpu.make_async_copy(h_ref.at[pl.ds(tile * tm, tm), :], buf.at[slot], fsem.at[slot])

    def scatter_wait(slot):
        for _ in range(2):
            pltpu.make_async_copy(buf.at[slot], xs_ref.at[pl.ds(0, tm), :], ssem.at[slot]).wait()

    @pl.when(i == 0)
    def _():
        fetch(0, 0).start()

        @pl.when(n > 1)
        def _():
            fetch(1, 1).start()

        zbuf[...] = jnp.zeros(zbuf.shape, zbuf.dtype)
        n_blocks = xs_ref.shape[0] // BM
        lane0 = N_GROUPS - 1
        n_used = ends_ref[lane0 + N_EXPERTS]
        targets = [(ends_ref[lane0 + e + 1] > ends_ref[lane0 + e], ends_ref[lane0 + e + 1] - 1)
                   for e in range(N_EXPERTS)]
        targets += [(n_used + j < n_blocks, n_used + j) for j in range(n_blocks - min_used)]
        for wait in (False, True):
            for cond, blk in targets:
                @pl.when(cond)
                def _():
                    cp = pltpu.make_async_copy(zbuf, xs_ref.at[pl.ds(blk * BM, BM), :], zsem)
                    cp.wait() if wait else cp.start()

    slot = i % N_DISPATCH_BUF
    fetch(i, slot).wait()
    base = 2 * i * tm
    for s in range(N_DISPATCH_BUF):
        @pl.when(slot == s)
        def _():
            for r in range(tm):
                for k in range(2):
                    dst = dest_ref[base + (2 * r + k)]
                    pltpu.make_async_copy(buf.at[s, pl.ds(r, 1), :], xs_ref.at[pl.ds(dst, 1), :],
                                          ssem.at[s]).start(priority=k)

    @pl.when(i >= 1)
    def _():
        scatter_wait((i - 1) % N_DISPATCH_BUF)

    @pl.when(i + 2 < n)
    def _():
        fetch(i + 2, (i + 2) % N_DISPATCH_BUF).start()

    @pl.when(i == n - 1)
    def _():
        scatter_wait(slot)


def _dispatch_call(dest, block_ends, h, n_blocks):
    t, dw = h.shape
    return pl.pallas_call(
        functools.partial(_dispatch_kernel, min_used=2 * t // BM),
        grid_spec=pltpu.PrefetchScalarGridSpec(
            num_scalar_prefetch=2,
            grid=(t // TM,),
            in_specs=[pl.BlockSpec(memory_space=pl.ANY)],
            out_specs=pl.BlockSpec(memory_space=pl.ANY),
            scratch_shapes=[pltpu.VMEM((N_DISPATCH_BUF, TM, dw), h.dtype),
                            pltpu.VMEM((BM, dw), h.dtype),
                            pltpu.SemaphoreType.DMA((N_DISPATCH_BUF,)),
                            pltpu.SemaphoreType.DMA((N_DISPATCH_BUF,)),
                            pltpu.SemaphoreType.DMA],
        ),
        out_shape=jax.ShapeDtypeStruct((n_blocks * BM, dw), h.dtype),
        compiler_params=_cparams(("arbitrary",)),
        name="moe_dispatch",
    )(dest, block_ends, h)


def _expert_kernel(blk_ref, tab_ref, xs_ref, wg_hbm, wu_hbm, wd_hbm, y_ref, wg_f32, wu_f32, wd_f32,
                   wg_sc, wu_sc, wd_sc, n_loaded, sem, *, layer):
    i = pl.program_id(0)
    n_used = tab_ref[LANES + N_GROUPS + N_EXPERTS - 1]
    last = blk_ref.shape[0] - 1
    e = blk_ref[i]

    def fetch(expert, slot):
        return [pltpu.make_async_copy(src.at[layer, expert], dst.at[slot], sem.at[slot])
                for src, dst in ((wg_hbm, wg_f32), (wu_hbm, wu_f32), (wd_hbm, wd_f32))]

    @pl.when((i == 0) & (n_used > 0))
    def _():
        n_loaded[0] = 0
        for cp in fetch(e, 0):
            cp.start()

    @pl.when((i < n_used) & ((i == 0) | (e != blk_ref[jnp.maximum(i - 1, 0)])))
    def _():
        slot = n_loaded[0] % 2
        for cp in fetch(e, slot):
            cp.wait()
        nxt = lax.while_loop(lambda j: (j < n_used) & (blk_ref[jnp.minimum(j, last)] == e), lambda j: j + 1, i + 1)

        @pl.when(nxt < n_used)
        def _():
            for cp in fetch(blk_ref[jnp.minimum(nxt, last)], 1 - slot):
                cp.start()

        wg_sc[...] = wg_f32[slot].astype(BF16)
        wu_sc[...] = wu_f32[slot].astype(BF16)
        wd_sc[...] = wd_f32[slot].astype(BF16)
        n_loaded[0] = n_loaded[0] + 1

    bm = xs_ref.shape[0]
    quarter = bm // 4
    before = tab_ref[LANES + N_GROUPS - 1 + e]
    valid = tab_ref[2 * LANES + N_GROUPS + e] - (i - before) * bm
    n_quarters = jnp.clip((valid + quarter - 1) // quarter, 1, 4)
    for nq in range(1, 5):
        @pl.when((i < n_used) & (n_quarters == nq))
        def _():
            rows = nq * quarter
            x = _unpack_bf16_pairs(xs_ref[0:rows, :])
            g = _dot(x, wg_sc[...])
            u = _dot(x, wu_sc[...])
            hdn = (g * (1.0 / (1.0 + jnp.exp(-g)))) * u
            y_ref[0:rows, :] = _dot(hdn.astype(BF16), wd_sc[...])
            if rows < bm:
                y_ref[rows:, :] = jnp.zeros((bm - rows, y_ref.shape[1]), F32)

    @pl.when(i >= n_used)
    def _():
        y_ref[...] = jnp.zeros(y_ref.shape, F32)


def _expert_call(blk_expert, tab_flat, xs, wg, wu, wd, layer):
    p, dw = xs.shape
    d, ff = wg.shape[-2:]
    return pl.pallas_call(
        functools.partial(_expert_kernel, layer=layer),
        grid_spec=pltpu.PrefetchScalarGridSpec(
            num_scalar_prefetch=2,
            grid=(p // BM,),
            in_specs=[pl.BlockSpec((BM, dw), lambda i, b, n: (i, 0)),
                      pl.BlockSpec(memory_space=pl.ANY),
                      pl.BlockSpec(memory_space=pl.ANY),
                      pl.BlockSpec(memory_space=pl.ANY)],
            out_specs=pl.BlockSpec((BM, d), lambda i, b, n: (i, 0)),
            scratch_shapes=[pltpu.VMEM((2, d, ff), F32), pltpu.VMEM((2, d, ff), F32), pltpu.VMEM((2, ff, d), F32),
                            pltpu.VMEM((d, ff), BF16), pltpu.VMEM((d, ff), BF16), pltpu.VMEM((ff, d), BF16),
                            pltpu.SMEM((1,), jnp.int32), pltpu.SemaphoreType.DMA((2,))],
        ),
        out_shape=jax.ShapeDtypeStruct((p, d), F32),
        compiler_params=_cparams(("arbitrary",)),
        name="moe_experts",
    )(blk_expert, tab_flat, xs, wg, wu, wd)


def _combine_kernel(dest_ref, x_ref, rg_ref, mod_ref, y_ref, *rest, nt_c):
    out_refs, (buf, sem) = rest[:-2], rest[-2:]
    j = pl.program_id(0)
    n_tiles = pl.num_programs(0) - 1
    tm = x_ref.shape[0]

    for s in range(2):
        @pl.when((j < n_tiles) & (j % 2 == s))
        def _():
            base = 2 * j * tm
            for r in range(tm):
                for k in range(2):
                    src = dest_ref[base + (2 * r + k)]
                    pltpu.make_async_copy(y_ref.at[pl.ds(src, 1), :], buf.at[s, k, pl.ds(r, 1), :],
                                          sem.at[s]).start(priority=k)

    @pl.when(j >= 1)
    def _():
        slot = (j - 1) % 2
        for k in range(2):
            pltpu.make_async_copy(y_ref.at[pl.ds(0, tm), :], buf.at[slot, k], sem.at[slot]).wait()
        rg = rg_ref[...]
        y = buf[slot, 0] * rg[:, 0:1] + buf[slot, 1] * rg[:, 1:2]
        res = x_ref[...] + mod_ref[0, 5:6, :] * y
        if len(out_refs) == 1:
            out_refs[0][...] = res
        else:
            @pl.when(j - 1 < nt_c)
            def _():
                out_refs[0][...] = res

            @pl.when(j - 1 >= nt_c)
            def _():
                out_refs[1][...] = res


def _combine_call(dest, x_new, route_g, mods, yb, mod_map, nt_c, split):
    t, d = x_new.shape
    prev = lambda j: jnp.maximum(j - 1, 0)
    if split:
        out_specs = [pl.BlockSpec((TM, d), lambda j, p: (jnp.minimum(prev(j), nt_c - 1), 0)),
                     pl.BlockSpec((TM, d), lambda j, p: (jnp.maximum(prev(j) - nt_c, 0), 0))]
        out_shape = [jax.ShapeDtypeStruct((nt_c * TM, d), F32), jax.ShapeDtypeStruct((t - nt_c * TM, d), F32)]
    else:
        out_specs = [pl.BlockSpec((TM, d), lambda j, p: (prev(j), 0))]
        out_shape = [jax.ShapeDtypeStruct((t, d), F32)]
    return pl.pallas_call(
        functools.partial(_combine_kernel, nt_c=nt_c),
        grid_spec=pltpu.PrefetchScalarGridSpec(
            num_scalar_prefetch=1,
            grid=(t // TM + 1,),
            in_specs=[pl.BlockSpec((TM, d), lambda j, p: (prev(j), 0)),
                      pl.BlockSpec((TM, LANES), lambda j, p: (prev(j), 0)),
                      pl.BlockSpec((1, 6, d), lambda j, p: mod_map(prev(j))),
                      pl.BlockSpec(memory_space=pl.ANY)],
            out_specs=out_specs,
            scratch_shapes=[pltpu.VMEM((2, 2, TM, d), F32), pltpu.SemaphoreType.DMA((2,))],
        ),
        out_shape=out_shape,
        compiler_params=_cparams(("arbitrary",)),
        name="moe_combine",
    )(dest, x_new, route_g, mods, yb)


def _head_cols(base, heads):
    return np.concatenate([base + h * HEAD_DIM + np.arange(HEAD_DIM) for h in heads])


def _layouts():
    wa, wb, wc = HA * HEAD_DIM, HB * HEAD_DIM, HC * HEAD_DIM
    qa0, ka0, va0 = 0, wa, 2 * wa
    qb0 = 3 * wa
    kb0 = qb0 + wb
    vb0 = kb0 + KVB * HEAD_DIM
    qc0 = vb0 + KVB * HEAD_DIM
    kc0 = qc0 + wc
    vc0 = kc0 + KVC * HEAD_DIM
    b_heads = [h for c in range(4) for h in (c, c + 4)]
    c_heads = [0, 2, 1, 3]
    q_cols = np.concatenate([_head_cols(qb0, b_heads), _head_cols(qa0, range(HA)), _head_cols(qc0, c_heads)])
    k_cols = np.concatenate([ka0 + np.arange(wa), kb0 + np.arange(KVB * HEAD_DIM), kc0 + np.arange(KVC * HEAD_DIM)])
    v_cols = np.concatenate([va0 + np.arange(wa), vb0 + np.arange(KVB * HEAD_DIM), vc0 + np.arange(KVC * HEAD_DIM)])
    in_cols = np.concatenate([q_cols, k_cols, v_cols])
    gain_idx = np.concatenate([np.full(HB, 2), np.full(HA, 0), np.full(HC, 4),
                               np.full(HA, 1), np.full(KVB, 3), np.full(KVC, 5)])
    gain_scale = np.concatenate([np.full(HB + HA + HC, HEAD_DIM ** -0.5 * LOG2E), np.ones(HA + KVB + KVC)])
    return in_cols, gain_idx, gain_scale.astype(np.float32), _head_cols(0, b_heads), _head_cols(0, c_heads)


def _rope_tables(seq):
    quarter = HEAD_DIM // 4
    t = jnp.arange(seq)
    rows, cols = t // GRID_W, t % GRID_W
    freqs = ROPE_THETA ** (-jnp.arange(quarter, dtype=F32) / quarter)
    ang_r = rows.astype(F32)[:, None] * freqs[None, :]
    ang_c = cols.astype(F32)[:, None] * freqs[None, :]
    cos = jnp.concatenate([jnp.cos(ang_r), jnp.cos(ang_r), jnp.cos(ang_c), jnp.cos(ang_c)], axis=1)
    sin = jnp.concatenate([-jnp.sin(ang_r), jnp.sin(ang_r), -jnp.sin(ang_c), jnp.sin(ang_c)], axis=1)
    return jnp.tile(cos, (1, 2)), jnp.tile(sin, (1, 2))


def kernel(x_prompt, x_sample, cache_ctx_k, cache_ctx_v, c, c_ctx, norm_mix, norm_ffn, w_ada, b_ada, w_in, qk_gain,
           rpb, sink, w_br_a, w_br_b, w_br_c, w_gate, w_out, w_route_group, w_route_expert, w_exp_gate, w_exp_up,
           w_exp_down):
    batch, seq_c, d = x_prompt.shape
    nb, seq_d, _ = x_sample.shape
    depth = w_in.shape[0]
    t_c, t_d = batch * seq_c, nb * seq_d
    t = t_c + t_d
    n_ctx = cache_ctx_k.shape[2]
    assert t_c % TM == 0 and seq_d % TM == 0 and TM % seq_c == 0 and nb + 1 <= 8 and 2 * t < (1 << 16)
    nt_c = t_c // TM
    tiles_per_seq = seq_d // TM
    nb_max = 2 * t // BM + N_EXPERTS
    nb_max = (nb_max + 7) // 8 * 8

    in_cols, gain_idx, gain_scale, ob_rows, oc_rows = _layouts()
    x_c, x_d, x_d_off = x_prompt.reshape(t_c, d), x_sample.reshape(t_d, d), 0
    ones_ctx = jnp.ones((nb, n_ctx, LANES), BF16)
    cvec = jnp.concatenate([c_ctx[None, :], c, jnp.zeros((8 - 1 - nb, d), F32)], axis=0)
    mods = _ada_call(cvec, w_ada, b_ada).reshape(depth, 8, 6, d)
    rope_tabs = _rope_tables(seq_d)
    bias_tabs = _bias_call(rpb)

    def mod_map(i):
        return (jnp.where(i < nt_c, 0, 1 + (i - nt_c) // tiles_per_seq), 0, 0)

    prev_kv = []
    for l in range(depth):
        w_in_p = w_in[l][:, in_cols].astype(BF16)
        gain_row = (qk_gain[l][gain_idx] * gain_scale[:, None]).reshape(1, -1)
        gain_row = jnp.concatenate([gain_row, jnp.ones((1, KV_COLS), F32)], axis=1)
        nm = norm_mix[l].reshape(1, d)
        nf = norm_ffn[l].reshape(1, d)

        q_c, k_c, v_c, k32, v32 = _qkv_call(x_c, mods[l], nm, w_in_p, gain_row, tile_off=0, n_tok=t_c,
                                            seq_len=seq_c, row_off=0, per_seq_mod=False, rope_tabs=None, emit_f32=True,
                                            prev_kv=tuple(prev_kv) if l == depth - 1 else ())
        prev_kv += [k32, v32]
        q_d, k_d, v_d = _qkv_call(x_d, mods[l], nm, w_in_p, gain_row, tile_off=x_d_off, n_tok=t_d,
                                  seq_len=seq_d, row_off=1, per_seq_mod=True, rope_tabs=rope_tabs, emit_f32=False)
        k_d = k_d.reshape(nb, seq_d, KV_COLS)
        v_d = v_d.reshape(nb, seq_d, V_EXT_COLS)

        o_ctx = _ctx_attn_call(sink[l], q_c, k_c, v_c, seq_c)
        ck = cache_ctx_k[:, l].reshape(nb, n_ctx, KV_COLS).astype(BF16)
        cv = cache_ctx_v[:, l].reshape(nb, n_ctx, KV_COLS).astype(BF16)
        b0, b1 = HA * HEAD_DIM, (HA + KVB) * HEAD_DIM
        cv = jnp.concatenate([cv[:, :, :b1], ones_ctx, cv[:, :, b1:], ones_ctx], axis=2)
        o_b = _dec_b_call(q_d, jnp.concatenate([k_d[:, :, b0:b1], ck[:, :, b0:b1]], axis=1),
                          jnp.concatenate([v_d[:, :, b0:b1 + LANES], cv[:, :, b0:b1 + LANES]], axis=1))
        o_c = _dec_c_call(sink[l], q_d, k_d, v_d, ck, cv)
        o_a = _dec_a_call(q_d, k_d, v_d, ck, cv, bias_tabs[l])

        w_route = jnp.concatenate([w_route_group[l], w_route_expert[l],
                                   jnp.zeros((d, LANES - N_GROUPS - N_EXPERTS), F32)], axis=1)
        wr_hi = w_route.astype(BF16)
        wr = jnp.concatenate([wr_hi, (w_route - wr_hi.astype(F32)).astype(BF16)], axis=1)
        x_new, h, logits = _merge_call(
            x_c, x_d, x_d_off, o_ctx, o_b, o_a, o_c, mods[l], nm, nf, w_gate[l].astype(BF16), w_br_b[l][ob_rows].astype(BF16),
            w_br_a[l].astype(BF16), w_br_c[l][oc_rows].astype(BF16), w_out[l].astype(BF16), wr, mod_map)

        route_i, route_g, tab, blk = _route_call(logits, nb_max)
        dest = _slots_call(route_i, tab)[:, :2].reshape(-1)
        xs = _dispatch_call(dest, tab[1], h, nb_max)
        yb = _expert_call(blk[:, 0], tab.reshape(-1), xs, w_exp_gate, w_exp_up, w_exp_down, l)
        outs = _combine_call(dest, x_new, route_g, mods[l], yb, mod_map, nt_c, split=(l == depth - 1))
        if l < depth - 1:
            x_c = x_d = outs[0]
            x_d_off = nt_c

    y_prompt = outs[0].reshape(batch, seq_c, d)
    y_sample = outs[1].reshape(nb, seq_d, d)
    new_k, new_v = (a.reshape(batch, depth, seq_c, HA + KVB + KVC, HEAD_DIM) for a in prev_kv[-2:])
    return (y_prompt, y_sample, new_k, new_v)
```

```python
import functools

import jax
import jax.numpy as jnp
import numpy as np
from jax import lax
from jax.experimental import pallas as pl
from jax.experimental.pallas import tpu as pltpu

GRID_W = 64
HEAD_DIM = 64
HA, HB, KVB, HC, KVC = 4, 8, 2, 4, 2
NA_WIN_R, NA_WIN_C = 8, 16
WINDOW = 128
ROPE_THETA = 10000.0
N_GROUPS, EXPERTS_PER_GROUP, N_EXPERTS = 4, 8, 32
QB_COLS, QA_COLS, QC_COLS = HB * HEAD_DIM, HA * HEAD_DIM, HC * HEAD_DIM
Q_COLS = QB_COLS + QA_COLS + QC_COLS
KV_COLS = (HA + KVB + KVC) * HEAD_DIM
EPS = 1e-6
NEG = -1e30
LOG2E = 1.4426950408889634

LANES = 128
V7X_VMEM_LIMIT = 56 * 1024 * 1024

TM = 512
TQ_B = 256
TK_B = 256
TQ_C = 256
TK_C = 256
ROWS_A = 8
BM = 512
ROW_CHAINS = 2

F32 = jnp.float32
BF16 = jnp.bfloat16


def _cparams(sem, vmem=V7X_VMEM_LIMIT):
    return pltpu.CompilerParams(dimension_semantics=sem, vmem_limit_bytes=vmem)


def _split_bf16(x):
    hi = x.astype(BF16)
    lo = (x - hi.astype(F32)).astype(BF16)
    return hi, lo


def _pack_bf16_pairs(x_bf16):
    n = x_bf16.shape[1] // 2
    lo = lax.bitcast_convert_type(x_bf16[:, :n].astype(F32), jnp.uint32)
    hi = lax.bitcast_convert_type(x_bf16[:, n:].astype(F32), jnp.uint32)
    return (lo >> 16) | hi


def _unpack_bf16_pairs(w_u32):
    lo = lax.bitcast_convert_type(w_u32 << 16, F32)
    hi = lax.bitcast_convert_type(w_u32 & jnp.uint32(0xFFFF0000), F32)
    return jnp.concatenate([lo, hi], axis=1).astype(BF16)


def _dot(a, b):
    return jnp.dot(a, b, preferred_element_type=F32)


def _dot_nt(a, b):
    return lax.dot_general(a, b, (((1,), (1,)), ((), ())), preferred_element_type=F32)


def _half_masks(dtype):
    lane = lax.broadcasted_iota(jnp.int32, (1, LANES), 1)
    lo = (lane < HEAD_DIM)
    return lo, jnp.where(lo, 1.0, 0.0).astype(dtype), jnp.where(lo, 0.0, 1.0).astype(dtype)


def _ada_kernel(c_ref, w_ref, b_ref, o_ref):
    c = c_ref[...]
    s = c * (1.0 / (1.0 + jnp.exp(-c)))
    s_hi, s_lo = _split_bf16(s)
    w_hi, w_lo = _split_bf16(w_ref[...])
    o_ref[...] = _dot(s_hi, w_hi) + _dot(s_lo, w_hi) + _dot(s_hi, w_lo) + b_ref[...]


def _ada_call(cvec, w_ada, b_ada):
    depth, d, n6 = w_ada.shape
    tn = 512
    return pl.pallas_call(
        _ada_kernel,
        grid=(depth, n6 // tn),
        in_specs=[
            pl.BlockSpec((8, d), lambda l, j: (0, 0)),
            pl.BlockSpec((None, d, tn), lambda l, j: (l, 0, j)),
            pl.BlockSpec((None, 1, tn), lambda l, j: (l, 0, j)),
        ],
        out_specs=pl.BlockSpec((None, 8, tn), lambda l, j: (l, 0, j)),
        out_shape=jax.ShapeDtypeStruct((depth, 8, n6), F32),
        compiler_params=_cparams(("arbitrary", "arbitrary")),
        name="ada_modulation",
    )(cvec, w_ada, b_ada.reshape(depth, 1, n6))


N_QCHUNK = 8
N_KCHUNK = 4
ROPE_QCHUNKS = (0, 1, 2, 3, 6, 7)
ROPE_KCHUNKS = (2, 3)
V_EXT_COLS = 6 * LANES


def _modulated_norm(x, gain, scale, shift):
    ms = jnp.mean(x * x, axis=-1, keepdims=True)
    return (x * lax.rsqrt(ms + EPS) * gain) * (1.0 + scale) + shift


def _qkv_kernel(*refs, rope, emit_f32, n_prev):
    x_ref, mod_ref, nm_ref, w_ref, gain_ref = refs[:5]
    pos = 5
    if rope:
        cos_ref, sin_ref, bd_ref = refs[pos:pos + 3]
        pos += 3
    prev_refs = refs[pos:pos + 2 * n_prev]
    pos += 2 * n_prev
    q_ref, k_ref, v_ref = refs[pos:pos + 3]
    pos += 3
    if emit_f32:
        k32_ref, v32_ref = refs[pos:pos + 2]

    def put_f32(ref, cols, val):
        if n_prev:
            ref[:, n_prev, :, cols] = val.reshape(ref.shape[0], ref.shape[2], val.shape[-1])
        else:
            ref[:, cols] = val

    for l in range(n_prev):
        for src, dst in ((prev_refs[2 * l], k32_ref), (prev_refs[2 * l + 1], v32_ref)):
            dst[:, l] = src[...].reshape(dst.shape[0], dst.shape[2], dst.shape[3])

    lane = lax.broadcasted_iota(jnp.int32, (1, LANES), 1)
    lo = lane < HEAD_DIM
    first = (lane % 32) < 16
    tm = x_ref.shape[0]
    n_chains = ROW_CHAINS if rope else 1

    for c in range(n_chains):
        rows = slice(c * tm // n_chains, (c + 1) * tm // n_chains)
        u = _modulated_norm(x_ref[rows, :], nm_ref[...], mod_ref[0, 1:2, :], mod_ref[0, 0:1, :])
        y = _dot(u.astype(BF16), w_ref[...])
        if rope:
            cos_t = cos_ref[rows, :]
            sin_t = sin_ref[rows, :]
        pair_sumsq = {}

        def head_sumsq(j):
            yj = y[:, j * LANES:(j + 1) * LANES]
            if not rope:
                sq = yj * yj
                s_lo = jnp.sum(jnp.where(lo, sq, 0.0), axis=-1, keepdims=True)
                s_hi = jnp.sum(jnp.where(lo, 0.0, sq), axis=-1, keepdims=True)
                return jnp.where(lo, s_lo, s_hi)
            p = j // 2
            if p not in pair_sumsq:
                y2 = y[:, 2 * p * LANES:(2 * p + 2) * LANES]
                sq_hi, sq_lo = _split_bf16(y2 * y2)
                pair_sumsq[p] = _dot(sq_hi, bd_ref[...]) + _dot(sq_lo, bd_ref[...])
            return pair_sumsq[p][:, (j % 2) * LANES:(j % 2 + 1) * LANES]

        def norm_chunk(j, use_rope):
            yj = y[:, j * LANES:(j + 1) * LANES]
            inv = lax.rsqrt(head_sumsq(j) * (1.0 / HEAD_DIM) + EPS)
            yj = (yj * inv) * gain_ref[:, j * LANES:(j + 1) * LANES]
            if use_rope:
                partner = jnp.where(first, pltpu.roll(yj, LANES - 16, axis=1), pltpu.roll(yj, 16, axis=1))
                yj = yj * cos_t + partner * sin_t
            return yj

        for j in range(N_QCHUNK):
            yj = norm_chunk(j, rope and j in ROPE_QCHUNKS)
            q_ref[rows, j * LANES:(j + 1) * LANES] = yj.astype(BF16)
        for j in range(N_KCHUNK):
            yj = norm_chunk(N_QCHUNK + j, rope and j in ROPE_KCHUNKS)
            k_ref[rows, j * LANES:(j + 1) * LANES] = yj.astype(BF16)
            if emit_f32:
                put_f32(k32_ref, slice(j * LANES, (j + 1) * LANES), yj)
        v = y[:, (N_QCHUNK + N_KCHUNK) * LANES:]
        if rope:
            ones = jnp.ones((v.shape[0], LANES), BF16)
            v_ref[rows, :3 * LANES] = v[:, :3 * LANES].astype(BF16)
            v_ref[rows, 3 * LANES:4 * LANES] = ones
            v_ref[rows, 4 * LANES:5 * LANES] = v[:, 3 * LANES:].astype(BF16)
            v_ref[rows, 5 * LANES:] = ones
        else:
            v_ref[...] = v.astype(BF16)
        if emit_f32:
            put_f32(v32_ref, slice(None), v)


def _qkv_call(x_all, mods, norm_mix_l, w_in_p, gain_row, *, tile_off, n_tok, seq_len, row_off,
              per_seq_mod, rope_tabs, emit_f32, prev_kv=()):
    d = x_all.shape[1]
    nt = n_tok // TM
    tiles_per_seq = max(seq_len // TM, 1)
    rope = rope_tabs is not None
    n_prev = len(prev_kv) // 2

    def mod_map(i):
        return (row_off + i // tiles_per_seq, 0, 0) if per_seq_mod else (row_off, 0, 0)

    in_specs = [
        pl.BlockSpec((TM, d), lambda i: (i + tile_off, 0)),
        pl.BlockSpec((1, 6, d), mod_map),
        pl.BlockSpec((1, d), lambda i: (0, 0)),
        pl.BlockSpec(w_in_p.shape, lambda i: (0, 0)),
        pl.BlockSpec((1, Q_COLS + 2 * KV_COLS), lambda i: (0, 0)),
    ]
    args = [x_all, mods, norm_mix_l, w_in_p, gain_row]
    if rope:
        in_specs += [pl.BlockSpec((TM, LANES), lambda i: (i % tiles_per_seq, 0))] * 2
        in_specs += [pl.BlockSpec((2 * LANES, 2 * LANES), lambda i: (0, 0))]
        head_of_lane = np.arange(2 * LANES) // HEAD_DIM
        args += list(rope_tabs) + [jnp.asarray(head_of_lane[:, None] == head_of_lane[None, :], BF16)]
    v_cols = V_EXT_COLS if rope else KV_COLS
    out_shape = [jax.ShapeDtypeStruct((n_tok, Q_COLS), BF16),
                 jax.ShapeDtypeStruct((n_tok, KV_COLS), BF16),
                 jax.ShapeDtypeStruct((n_tok, v_cols), BF16)]
    out_specs = [pl.BlockSpec((TM, Q_COLS), lambda i: (i, 0)),
                 pl.BlockSpec((TM, KV_COLS), lambda i: (i, 0)),
                 pl.BlockSpec((TM, v_cols), lambda i: (i, 0))]
    in_specs += [pl.BlockSpec((TM, KV_COLS), lambda i: (i, 0))] * (2 * n_prev)
    args += list(prev_kv)
    if emit_f32 and n_prev:
        req = TM // seq_len
        out_shape += [jax.ShapeDtypeStruct((n_tok // seq_len, n_prev + 1, seq_len, KV_COLS), F32)] * 2
        out_specs += [pl.BlockSpec((req, n_prev + 1, seq_len, KV_COLS), lambda i: (i, 0, 0, 0))] * 2
    elif emit_f32:
        out_shape += [jax.ShapeDtypeStruct((n_tok, KV_COLS), F32)] * 2
        out_specs += [pl.BlockSpec((TM, KV_COLS), lambda i: (i, 0))] * 2
    return pl.pallas_call(
        functools.partial(_qkv_kernel, rope=rope, emit_f32=emit_f32, n_prev=n_prev),
        grid=(nt,),
        in_specs=in_specs,
        out_specs=out_specs,
        out_shape=out_shape,
        compiler_params=_cparams(("parallel",)),
        name="qkv_rope" if rope else "qkv_ctx",
    )(*args)


def _softmax_rows(s, sink_col=None):
    m = jnp.max(s, axis=-1, keepdims=True)
    if sink_col is not None:
        m = jnp.maximum(m, sink_col)
    p = jnp.exp2(s - m)
    l = jnp.sum(p, axis=-1, keepdims=True)
    if sink_col is not None:
        l = l + jnp.exp2(sink_col - m)
    return p, l


def _ctx_attn_kernel(sink_ref, q_ref, k_ref, v_ref, o_ref):
    s_len = q_ref.shape[0]
    _, m_lo, m_hi = _half_masks(BF16)
    lo = lax.broadcasted_iota(jnp.int32, (1, LANES), 1) < HEAD_DIM

    def group(q_chunks, kc, sinks=None):
        n = len(q_chunks)
        k = k_ref[:, kc * LANES:(kc + 1) * LANES]
        v = v_ref[:, kc * LANES:(kc + 1) * LANES]
        qs = jnp.concatenate([q_ref[:, c * LANES:(c + 1) * LANES] * m for m in (m_lo, m_hi) for c in q_chunks],
                             axis=0)
        s = _dot_nt(qs, k)
        sink_col = None
        if sinks is not None:
            row = lax.broadcasted_iota(jnp.int32, (2 * n * s_len, 1), 0)
            sink_col = jnp.zeros((2 * n * s_len, 1), F32)
            for idx, sv in enumerate(sinks):
                sink_col = jnp.where((row >= idx * s_len) & (row < (idx + 1) * s_len), sv, sink_col)
        p, l = _softmax_rows(s, sink_col)
        o = _dot(p.astype(BF16), v) / l
        for idx, c in enumerate(q_chunks):
            o_c = jnp.where(lo, o[idx * s_len:(idx + 1) * s_len], o[(n + idx) * s_len:(n + idx + 1) * s_len])
            o_ref[:, c * LANES:(c + 1) * LANES] = o_c.astype(BF16)

    group([0, 1, 2, 3], 2)
    group([4], 0)
    group([5], 1)
    group([6, 7], 3, sinks=[sink_ref[h] * LOG2E for h in range(HC)])


def _ctx_attn_call(sink_l, q, k, v, s_len):
    batch = q.shape[0] // s_len
    return pl.pallas_call(
        _ctx_attn_kernel,
        grid_spec=pltpu.PrefetchScalarGridSpec(
            num_scalar_prefetch=1,
            grid=(batch,),
            in_specs=[pl.BlockSpec((s_len, Q_COLS), lambda b, s: (b, 0)),
                      pl.BlockSpec((s_len, KV_COLS), lambda b, s: (b, 0)),
                      pl.BlockSpec((s_len, KV_COLS), lambda b, s: (b, 0))],
            out_specs=pl.BlockSpec((s_len, Q_COLS), lambda b, s: (b, 0)),
        ),
        out_shape=jax.ShapeDtypeStruct((q.shape[0], Q_COLS), BF16),
        compiler_params=_cparams(("parallel",)),
        name="ctx_attention",
    )(sink_l, q, k, v)


def _dec_b_kernel(q_ref, k_ref, v_ref, o_ref):
    n_keys = k_ref.shape[0]
    n_chunks = n_keys // TK_B
    _, m_lo, m_hi = _half_masks(BF16)
    lo = lax.broadcasted_iota(jnp.int32, (1, LANES), 1) < HEAD_DIM

    def stacked_q(msk):
        return jnp.concatenate([q_ref[:, c * LANES:(c + 1) * LANES] * msk for c in range(4)], axis=0)

    def write(outs):
        for c in range(4):
            o_c = jnp.where(lo, outs[0][c * TQ_B:(c + 1) * TQ_B], outs[1][c * TQ_B:(c + 1) * TQ_B])
            o_ref[:, c * LANES:(c + 1) * LANES] = o_c.astype(BF16)

    outs, sums = [], []
    for msk in (m_lo, m_hi):
        qs = stacked_q(msk)
        s = _dot_nt(qs, k_ref[0:TK_B, :])
        m = jnp.max(s, axis=-1, keepdims=True)
        acc = _dot(jnp.exp2(s - m).astype(BF16), v_ref[0:TK_B, :])
        for t in range(1, n_chunks):
            s = _dot_nt(qs, k_ref[t * TK_B:(t + 1) * TK_B, :])
            acc = acc + _dot(jnp.exp2(s - m).astype(BF16), v_ref[t * TK_B:(t + 1) * TK_B, :])
        outs.append(acc[:, :LANES] / acc[:, LANES:])
        sums.append(jnp.sum(acc * 0.0))
    write(outs)

    @pl.when(jnp.logical_not(sums[0] + sums[1] == 0.0))
    def _():
        exact = []
        for msk in (m_lo, m_hi):
            s = _dot_nt(stacked_q(msk), k_ref[...])
            acc = _dot(jnp.exp2(s - jnp.max(s, axis=-1, keepdims=True)).astype(BF16), v_ref[...])
            exact.append(acc[:, :LANES] / acc[:, LANES:])
        write(exact)


def _dec_b_call(q, k_all, v_all):
    nb, n_keys, _ = k_all.shape
    seq = q.shape[0] // nb
    nq = seq // TQ_B
    return pl.pallas_call(
        _dec_b_kernel,
        grid=(nb, nq),
        in_specs=[pl.BlockSpec((TQ_B, QB_COLS), lambda b, i: (b * nq + i, 0)),
                  pl.BlockSpec((None, n_keys, LANES), lambda b, i: (b, 0, 0)),
                  pl.BlockSpec((None, n_keys, 2 * LANES), lambda b, i: (b, 0, 0))],
        out_specs=pl.BlockSpec((TQ_B, QB_COLS), lambda b, i: (b * nq + i, 0)),
        out_shape=jax.ShapeDtypeStruct((q.shape[0], QB_COLS), BF16),
        compiler_params=_cparams(("parallel", "parallel")),
        name="dec_dense_attention",
    )(q, k_all, v_all)


def _dec_c_kernel(sink_ref, q_ref, k_ref, v_ref, ck_ref, cv_ref, o_ref):
    qi = pl.program_id(1)
    seq = k_ref.shape[0]
    win = TQ_C + 2 * WINDOW
    _, m_lo, m_hi = _half_masks(BF16)
    lo = lax.broadcasted_iota(jnp.int32, (1, LANES), 1) < HEAD_DIM
    start = jnp.clip(qi * TQ_C - WINDOW, 0, seq - win)
    start = pl.multiple_of(start, WINDOW)
    k_win = k_ref[pl.ds(start, win), :]
    v_win = v_ref[pl.ds(start, win), :]
    k_ctx = ck_ref[...]
    v_ctx = cv_ref[...]
    row = lax.broadcasted_iota(jnp.int32, (2 * TQ_C, win), 0)
    col = lax.broadcasted_iota(jnp.int32, (2 * TQ_C, win), 1)
    qpos = qi * TQ_C + jnp.where(row >= TQ_C, row - TQ_C, row)
    ok = jnp.abs(qpos - (start + col)) <= WINDOW
    row1 = lax.broadcasted_iota(jnp.int32, (2 * TQ_C, 1), 0)
    def attend(exact_shift):
        outs, finite = [], []
        for kvh, msk in enumerate((m_lo, m_hi)):
            qs = jnp.concatenate([q_ref[:, c * LANES:(c + 1) * LANES] * msk for c in range(2)], axis=0)
            sink_col = jnp.where(row1 < TQ_C, sink_ref[2 * kvh] * LOG2E, sink_ref[2 * kvh + 1] * LOG2E)
            if exact_shift:
                s_loc = jnp.where(ok, _dot_nt(qs, k_win), NEG)
                s_ctx = _dot_nt(qs, k_ctx)
                m = jnp.maximum(jnp.maximum(jnp.max(s_loc, axis=-1, keepdims=True),
                                            jnp.max(s_ctx, axis=-1, keepdims=True)), sink_col)
                acc = (_dot(jnp.exp2(s_loc - m).astype(BF16), v_win)
                       + _dot(jnp.exp2(s_ctx - m).astype(BF16), v_ctx))
            else:
                m = sink_col
                acc = None
                for t in range(win // TK_C):
                    cols = slice(t * TK_C, (t + 1) * TK_C)
                    s = jnp.where(ok[:, cols], _dot_nt(qs, k_win[cols, :]), NEG)
                    part = _dot(jnp.exp2(s - m).astype(BF16), v_win[cols, :])
                    acc = part if acc is None else acc + part
                for t in range(k_ctx.shape[0] // TK_C):
                    cols = slice(t * TK_C, (t + 1) * TK_C)
                    acc = acc + _dot(jnp.exp2(_dot_nt(qs, k_ctx[cols, :]) - m).astype(BF16), v_ctx[cols, :])
            outs.append(acc[:, :LANES] / (acc[:, LANES:] + jnp.exp2(sink_col - m)))
            finite.append(jnp.sum(acc * 0.0))
        for c in range(2):
            o_c = jnp.where(lo, outs[0][c * TQ_C:(c + 1) * TQ_C], outs[1][c * TQ_C:(c + 1) * TQ_C])
            o_ref[:, c * LANES:(c + 1) * LANES] = o_c.astype(BF16)
        return finite[0] + finite[1]

    check = attend(exact_shift=False)

    @pl.when(jnp.logical_not(check == 0.0))
    def _():
        attend(exact_shift=True)


def _dec_c_call(sink_l, q, k_d, v_d, ck, cv):
    nb, seq, _ = k_d.shape
    n_ctx = ck.shape[1]
    nq = seq // TQ_C
    return pl.pallas_call(
        _dec_c_kernel,
        grid_spec=pltpu.PrefetchScalarGridSpec(
            num_scalar_prefetch=1,
            grid=(nb, nq),
            in_specs=[pl.BlockSpec((TQ_C, QC_COLS), lambda b, i, s: (b * nq + i, (QB_COLS + QA_COLS) // QC_COLS)),
                      pl.BlockSpec((None, seq, LANES), lambda b, i, s: (b, 0, 3)),
                      pl.BlockSpec((None, seq, 2 * LANES), lambda b, i, s: (b, 0, 2)),
                      pl.BlockSpec((None, n_ctx, LANES), lambda b, i, s: (b, 0, 3)),
                      pl.BlockSpec((None, n_ctx, 2 * LANES), lambda b, i, s: (b, 0, 2))],
            out_specs=pl.BlockSpec((TQ_C, QC_COLS), lambda b, i, s: (b * nq + i, 0)),
        ),
        out_shape=jax.ShapeDtypeStruct((q.shape[0], QC_COLS), BF16),
        compiler_params=_cparams(("parallel", "parallel")),
        name="dec_window_attention",
    )(sink_l, q, k_d, v_d, ck, cv)


N_DR = 2 * NA_WIN_R - 2


def _bias_kernel(rpb_ref, o_ref):
    lh = pl.program_id(0)
    n_dc = 2 * NA_WIN_C - 1
    qc = lax.broadcasted_iota(jnp.int32, (GRID_W, LANES), 0)
    kk = lax.broadcasted_iota(jnp.int32, (GRID_W, LANES), 1)
    half = kk >= GRID_W
    kc = jnp.where(half, kk - GRID_W, kk)
    delta = jnp.clip(kc - qc, -(NA_WIN_C - 1), NA_WIN_C - 1) + (NA_WIN_C - 1)
    col_start = jnp.clip(qc - NA_WIN_C // 2, 0, GRID_W - NA_WIN_C)
    ok = (kc >= col_start) & (kc < col_start + NA_WIN_C)
    for d in range(N_DR):
        acc = jnp.zeros((GRID_W, LANES), F32)
        base = (lh * (2 * NA_WIN_R - 1) + d) * n_dc
        for dc in range(n_dc):
            val = jnp.where(half, rpb_ref[base + n_dc + dc] * LOG2E, rpb_ref[base + dc] * LOG2E)
            acc = jnp.where(delta == dc, val, acc)
        o_ref[d] = jnp.where(ok, acc, NEG)


def _bias_call(rpb):
    depth = rpb.shape[0]
    return pl.pallas_call(
        _bias_kernel,
        grid_spec=pltpu.PrefetchScalarGridSpec(
            num_scalar_prefetch=1,
            grid=(depth * HA,),
            in_specs=[],
            out_specs=pl.BlockSpec((None, N_DR, GRID_W, LANES), lambda lh, r: (lh, 0, 0, 0)),
        ),
        out_shape=jax.ShapeDtypeStruct((depth * HA, N_DR, GRID_W, LANES), F32),
        compiler_params=_cparams(("arbitrary",)),
        name="na_bias_table",
    )(rpb.reshape(-1)).reshape(depth, HA, N_DR, GRID_W, LANES)


def _dec_a_kernel(q_ref, k_ref, v_ref, ck_ref, cv_ref, bias_ref, o_ref):
    rb = pl.program_id(2)
    rows = k_ref.shape[0] // GRID_W
    n_loc = NA_WIN_R * GRID_W
    _, m_lo, m_hi = _half_masks(BF16)
    lo = lax.broadcasted_iota(jnp.int32, (1, LANES), 1) < HEAD_DIM
    qs = [jnp.concatenate([q_ref[rr * GRID_W:(rr + 1) * GRID_W, :] * msk for msk in (m_lo, m_hi)], axis=0)
          for rr in range(ROWS_A)]
    s_ctx = _dot_nt(jnp.concatenate(qs, axis=0), ck_ref[...])
    s_loc, v_win = [], []
    for rr in range(ROWS_A):
        r = rb * ROWS_A + rr
        start = jnp.clip(r - NA_WIN_R // 2, 0, rows - NA_WIN_R)
        d0 = start - r + (NA_WIN_R - 1)
        off = pl.multiple_of(start * GRID_W, GRID_W)
        bias = jnp.concatenate(
            [jnp.concatenate([bias_ref[hh, d0 + 2 * i] for i in range(NA_WIN_R // 2)], axis=1) for hh in range(2)],
            axis=0)
        s_loc.append(_dot_nt(qs[rr], k_ref[pl.ds(off, n_loc), :]) + bias)
        v_win.append(v_ref[pl.ds(off, n_loc), :])
    m = jnp.maximum(jnp.concatenate([jnp.max(s, axis=-1, keepdims=True) for s in s_loc], axis=0),
                    jnp.max(s_ctx, axis=-1, keepdims=True))
    p_ctx = jnp.exp2(s_ctx - m)
    l_ctx = jnp.sum(p_ctx, axis=-1, keepdims=True)
    o_ctx = _dot(p_ctx.astype(BF16), cv_ref[...])
    for rr in range(ROWS_A):
        sl = slice(rr * 2 * GRID_W, (rr + 1) * 2 * GRID_W)
        p_loc = jnp.exp2(s_loc[rr] - m[sl])
        l = jnp.sum(p_loc, axis=-1, keepdims=True) + l_ctx[sl]
        o = (_dot(p_loc.astype(BF16), v_win[rr]) + o_ctx[sl]) / l
        o_c = jnp.where(lo, o[:GRID_W], o[GRID_W:])
        o_ref[rr * GRID_W:(rr + 1) * GRID_W, :] = o_c.astype(BF16)


def _dec_a_call(q, k_d, v_d, ck, cv, bias_tab):
    nb, seq, _ = k_d.shape
    n_ctx = ck.shape[1]
    tq = ROWS_A * GRID_W
    nq = seq // tq
    return pl.pallas_call(
        _dec_a_kernel,
        grid=(2, nb, nq),
        in_specs=[pl.BlockSpec((tq, LANES), lambda p, b, i: (b * nq + i, 4 + p)),
                  pl.BlockSpec((None, seq, LANES), lambda p, b, i: (b, 0, p)),
                  pl.BlockSpec((None, seq, LANES), lambda p, b, i: (b, 0, p)),
                  pl.BlockSpec((None, n_ctx, LANES), lambda p, b, i: (b, 0, p)),
                  pl.BlockSpec((None, n_ctx, LANES), lambda p, b, i: (b, 0, p)),
                  pl.BlockSpec((2, N_DR, GRID_W, LANES), lambda p, b, i: (p, 0, 0, 0))],
        out_specs=pl.BlockSpec((tq, LANES), lambda p, b, i: (b * nq + i, p)),
        out_shape=jax.ShapeDtypeStruct((q.shape[0], 2 * LANES), BF16),
        compiler_params=_cparams(("parallel", "parallel", "parallel")),
        name="dec_neighbourhood_attention",
    )(q, k_d, v_d, ck, cv, bias_tab)


def _merge_kernel(xc_ref, xd_ref, oc_ref, ob_ref, oa_ref, ow_ref, mod_ref, nm_ref, nf_ref, wg_ref, wbb_ref, wba_ref,
                  wbc_ref, wo_ref, wr_ref, xo_ref, h_ref, lg_ref, *, nt_c):
    d = xc_ref.shape[1]
    tm = xc_ref.shape[0]
    is_ctx = pl.program_id(0) < nt_c
    for c in range(ROW_CHAINS):
        rows = slice(c * tm // ROW_CHAINS, (c + 1) * tm // ROW_CHAINS)
        o_b = jnp.where(is_ctx, oc_ref[rows, :QB_COLS], ob_ref[rows, :])
        o_a = jnp.where(is_ctx, oc_ref[rows, QB_COLS:QB_COLS + QA_COLS], oa_ref[rows, :])
        o_c = jnp.where(is_ctx, oc_ref[rows, QB_COLS + QA_COLS:], ow_ref[rows, :])
        x = jnp.where(is_ctx, xc_ref[rows, :], xd_ref[rows, :])
        u = _modulated_norm(x, nm_ref[...], mod_ref[0, 1:2, :], mod_ref[0, 0:1, :])
        gates = _dot(u.astype(BF16), wg_ref[...])
        gates = 1.0 / (1.0 + jnp.exp(-gates))
        m = (gates[:, :d] * _dot(o_a, wba_ref[...])
             + gates[:, d:2 * d] * _dot(o_b, wbb_ref[...])
             + gates[:, 2 * d:] * _dot(o_c, wbc_ref[...]))
        x_new = x + mod_ref[0, 2:3, :] * _dot(m.astype(BF16), wo_ref[...])
        xo_ref[rows, :] = x_new
        h = _modulated_norm(x_new, nf_ref[...], mod_ref[0, 4:5, :], mod_ref[0, 3:4, :])
        h_hi, h_lo = _split_bf16(h)
        h_ref[rows, :] = _pack_bf16_pairs(h_hi)
        both = _dot(h_hi, wr_ref[...])
        lg_ref[rows, :] = both[:, :LANES] + both[:, LANES:] + _dot(h_lo, wr_ref[:, :LANES])


def _merge_call(x_c, x_d, x_d_tile_off, o_ctx, o_b, o_a, o_c, mods, nm, nf, wg, wbb, wba, wbc, wo, wr, mod_map):
    d = x_c.shape[1]
    nt_c = o_ctx.shape[0] // TM
    t = o_ctx.shape[0] + o_b.shape[0]
    const = lambda i: (0, 0)
    full = lambda a: pl.BlockSpec(a.shape, const)
    ctx_map = lambda i: (jnp.minimum(i, nt_c - 1), 0)
    dec_map = lambda i: (jnp.maximum(i - nt_c, 0), 0)
    return pl.pallas_call(
        functools.partial(_merge_kernel, nt_c=nt_c),
        grid=(t // TM,),
        in_specs=[pl.BlockSpec((TM, d), ctx_map),
                  pl.BlockSpec((TM, d), lambda i: (jnp.maximum(i - nt_c, 0) + x_d_tile_off, 0)),
                  pl.BlockSpec((TM, Q_COLS), ctx_map),
                  pl.BlockSpec((TM, QB_COLS), dec_map),
                  pl.BlockSpec((TM, QA_COLS), dec_map),
                  pl.BlockSpec((TM, QC_COLS), dec_map),
                  pl.BlockSpec((1, 6, d), mod_map),
                  full(nm), full(nf), full(wg), full(wbb), full(wba), full(wbc), full(wo), full(wr)],
        out_specs=[pl.BlockSpec((TM, d), lambda i: (i, 0)),
                   pl.BlockSpec((TM, d // 2), lambda i: (i, 0)),
                   pl.BlockSpec((TM, LANES), lambda i: (i, 0))],
        out_shape=[jax.ShapeDtypeStruct((t, d), F32),
                   jax.ShapeDtypeStruct((t, d // 2), jnp.uint32),
                   jax.ShapeDtypeStruct((t, LANES), F32)],
        compiler_params=_cparams(("parallel",)),
        name="merge_branches",
    )(x_c, x_d, o_ctx, o_b, o_a, o_c, mods, nm, nf, wg, wbb, wba, wbc, wo, wr)


def _route_kernel(lg_ref, ri_ref, rg_ref, tab_ref, blk_ref, cnt_sc, *, nb_max):
    i = pl.program_id(0)
    n = pl.num_programs(0)
    tm = lg_ref.shape[0]

    @pl.when(i == 0)
    def _():
        cnt_sc[...] = jnp.zeros(cnt_sc.shape, F32)

    lg = lg_ref[...]
    lane = lax.broadcasted_iota(jnp.int32, (tm, LANES), 1).astype(F32)
    big = float(LANES)
    neg_inf = -jnp.inf
    gl = jnp.where(lane < N_GROUPS, lg, neg_inf)
    gmax = jnp.max(gl, axis=-1, keepdims=True)
    g_sel = jnp.min(jnp.where(gl == gmax, lane, big), axis=-1, keepdims=True)
    g_w = 1.0 / jnp.sum(jnp.exp(gl - gmax), axis=-1, keepdims=True)
    e_lo = N_GROUPS + EXPERTS_PER_GROUP * g_sel
    el = jnp.where((lane >= e_lo) & (lane < e_lo + EXPERTS_PER_GROUP), lg, neg_inf)
    m1 = jnp.max(el, axis=-1, keepdims=True)
    i1 = jnp.min(jnp.where(el == m1, lane, big), axis=-1, keepdims=True)
    el2 = jnp.where(lane == i1, neg_inf, el)
    m2 = jnp.max(el2, axis=-1, keepdims=True)
    i2 = jnp.min(jnp.where(el2 == m2, lane, big), axis=-1, keepdims=True)
    e2x = jnp.exp(m2 - m1)
    den = 1.0 + e2x
    gate1 = g_w * (1.0 / den)
    gate2 = g_w * (e2x / den)
    hit1 = lane == i1
    hit2 = lane == i2
    onehot = jnp.where(hit1 | hit2, 1.0, 0.0).astype(BF16)
    rr = lax.broadcasted_iota(jnp.int32, (tm, tm), 0)
    cc = lax.broadcasted_iota(jnp.int32, (tm, tm), 1)
    tri = jnp.where(cc <= rr, 1.0, 0.0).astype(BF16)
    pref = _dot(tri, onehot) + cnt_sc[...]
    rank1 = jnp.sum(jnp.where(hit1, pref, 0.0), axis=-1, keepdims=True) - 1.0
    rank2 = jnp.sum(jnp.where(hit2, pref, 0.0), axis=-1, keepdims=True) - 1.0
    cnt_sc[...] = pref[tm - 1:tm, :]
    e1 = (i1 - N_GROUPS).astype(jnp.int32)
    e2 = (i2 - N_GROUPS).astype(jnp.int32)
    pk1 = (e1 << 16) | rank1.astype(jnp.int32)
    pk2 = (e2 << 16) | rank2.astype(jnp.int32)
    lane_i = lax.broadcasted_iota(jnp.int32, (tm, LANES), 1)
    ri_ref[...] = jnp.where(lane_i == 0, pk1, jnp.where(lane_i == 1, pk2, 0))
    rg_ref[...] = jnp.where(lane_i == 0, gate1, jnp.where(lane_i == 1, gate2, 0.0))

    @pl.when(i == n - 1)
    def _():
        cnt = cnt_sc[...]
        nblk = jnp.floor((cnt + (BM - 1.0)) * (1.0 / BM))
        l8 = lax.broadcasted_iota(jnp.int32, (8, LANES), 1)
        is_exp = (l8 >= N_GROUPS) & (l8 < N_GROUPS + N_EXPERTS)
        nblk8 = jnp.where(is_exp, jnp.broadcast_to(nblk, (8, LANES)), 0.0)
        r2 = lax.broadcasted_iota(jnp.int32, (LANES, LANES), 0)
        c2 = lax.broadcasted_iota(jnp.int32, (LANES, LANES), 1)
        upper = jnp.where(r2 <= c2, 1.0, 0.0).astype(BF16)
        hi, lo_ = _split_bf16(nblk8)
        blk_end = _dot(hi, upper) + _dot(lo_, upper)
        blk_start = blk_end - nblk8
        row8 = lax.broadcasted_iota(jnp.int32, (8, LANES), 0)
        tab = jnp.where(row8 == 0, blk_start * BM, jnp.where(row8 == 1, blk_end, cnt))
        tab_ref[...] = tab.astype(jnp.int32)
        bi = lax.broadcasted_iota(jnp.int32, (nb_max, LANES), 0).astype(F32)
        ends = blk_end[0:1, :]
        lb = lax.broadcasted_iota(jnp.int32, (nb_max, LANES), 1)
        done = jnp.where((lb >= N_GROUPS) & (lb < N_GROUPS + N_EXPERTS) & (ends <= bi), 1.0, 0.0)
        be = jnp.minimum(jnp.sum(done, axis=-1, keepdims=True), N_EXPERTS - 1.0)
        blk_ref[...] = jnp.broadcast_to(be, (nb_max, LANES)).astype(jnp.int32)


def _route_call(logits, nb_max):
    t = logits.shape[0]
    return pl.pallas_call(
        functools.partial(_route_kernel, nb_max=nb_max),
        grid=(t // TM,),
        in_specs=[pl.BlockSpec((TM, LANES), lambda i: (i, 0))],
        out_specs=[pl.BlockSpec((TM, LANES), lambda i: (i, 0)),
                   pl.BlockSpec((TM, LANES), lambda i: (i, 0)),
                   pl.BlockSpec((8, LANES), lambda i: (0, 0)),
                   pl.BlockSpec((nb_max, LANES), lambda i: (0, 0))],
        out_shape=[jax.ShapeDtypeStruct((t, LANES), jnp.int32),
                   jax.ShapeDtypeStruct((t, LANES), F32),
                   jax.ShapeDtypeStruct((8, LANES), jnp.int32),
                   jax.ShapeDtypeStruct((nb_max, LANES), jnp.int32)],
        scratch_shapes=[pltpu.VMEM((1, LANES), F32)],
        compiler_params=_cparams(("arbitrary",)),
        name="moe_route",
    )(logits)


def _slots_kernel(ri_ref, tab_ref, o_ref):
    tm = ri_ref.shape[0]
    lane = lax.broadcasted_iota(jnp.int32, (tm, LANES), 1)
    starts = tab_ref[0:1, :].astype(F32)
    pk = ri_ref[...]
    out = jnp.zeros((tm, LANES), jnp.int32)
    for k in range(2):
        pkk = pk[:, k:k + 1]
        first = jnp.sum(jnp.where(lane == (pkk >> 16) + N_GROUPS, starts, 0.0), axis=-1, keepdims=True)
        out = jnp.where(lane == k, first.astype(jnp.int32) + (pkk & 0xFFFF), out)
    o_ref[...] = out


def _slots_call(route_i, tab):
    t = route_i.shape[0]
    tile = next(c for c in (4 * TM, 2 * TM, TM) if t % c == 0)
    return pl.pallas_call(
        _slots_kernel,
        grid=(t // tile,),
        in_specs=[pl.BlockSpec((tile, LANES), lambda i: (i, 0)),
                  pl.BlockSpec((8, LANES), lambda i: (0, 0))],
        out_specs=pl.BlockSpec((tile, LANES), lambda i: (i, 0)),
        out_shape=jax.ShapeDtypeStruct((t, LANES), jnp.int32),
        compiler_params=_cparams(("parallel",)),
        name="moe_slots",
    )(route_i, tab)


N_DISPATCH_BUF = 3


def _dispatch_kernel(dest_ref, ends_ref, h_ref, xs_ref, buf, zbuf, fsem, ssem, zsem, *, min_used):
    i = pl.program_id(0)
    n = pl.num_programs(0)
    tm = buf.shape[1]

    def fetch(tile, slot):
        return pltpu.make_async_copy(h_ref.at[pl.ds(tile * tm, tm), :], buf.at[slot], fsem.at[slot])

    def scatter_wait(slot):
        for _ in range(2):
            pltpu.make_async_copy(buf.at[slot], xs_ref.at[pl.ds(0, tm), :], ssem.at[slot]).wait()

    @pl.when(i == 0)
    def _():
        fetch(0, 0).start()

        @pl.when(n > 1)
        def _():
            fetch(1, 1).start()

        zbuf[...] = jnp.zeros(zbuf.shape, zbuf.dtype)
        n_blocks = xs_ref.shape[0] // BM
        lane0 = N_GROUPS - 1
        n_used = ends_ref[lane0 + N_EXPERTS]
        targets = [(ends_ref[lane0 + e + 1] > ends_ref[lane0 + e], ends_ref[lane0 + e + 1] - 1)
                   for e in range(N_EXPERTS)]
        targets += [(n_used + j < n_blocks, n_used + j) for j in range(n_blocks - min_used)]
        for wait in (False, True):
            for cond, blk in targets:
                @pl.when(cond)
                def _():
                    cp = pltpu.make_async_copy(zbuf, xs_ref.at[pl.ds(blk * BM, BM), :], zsem)
                    cp.wait() if wait else cp.start()

    slot = i % N_DISPATCH_BUF
    fetch(i, slot).wait()
    base = 2 * i * tm
    for s in range(N_DISPATCH_BUF):
        @pl.when(slot == s)
        def _():
            for r in range(tm):
                for k in range(2):
                    dst = dest_ref[base + (2 * r + k)]
                    pltpu.make_async_copy(buf.at[s, pl.ds(r, 1), :], xs_ref.at[pl.ds(dst, 1), :],
                                          ssem.at[s]).start(priority=k)

    @pl.when(i >= 1)
    def _():
        scatter_wait((i - 1) % N_DISPATCH_BUF)

    @pl.when(i + 2 < n)
    def _():
        fetch(i + 2, (i + 2) % N_DISPATCH_BUF).start()

    @pl.when(i == n - 1)
    def _():
        scatter_wait(slot)


def _dispatch_call(dest, block_ends, h, n_blocks):
    t, dw = h.shape
    return pl.pallas_call(
        functools.partial(_dispatch_kernel, min_used=2 * t // BM),
        grid_spec=pltpu.PrefetchScalarGridSpec(
            num_scalar_prefetch=2,
            grid=(t // TM,),
            in_specs=[pl.BlockSpec(memory_space=pl.ANY)],
            out_specs=pl.BlockSpec(memory_space=pl.ANY),
            scratch_shapes=[pltpu.VMEM((N_DISPATCH_BUF, TM, dw), h.dtype),
                            pltpu.VMEM((BM, dw), h.dtype),
                            pltpu.SemaphoreType.DMA((N_DISPATCH_BUF,)),
                            pltpu.SemaphoreType.DMA((N_DISPATCH_BUF,)),
                            pltpu.SemaphoreType.DMA],
        ),
        out_shape=jax.ShapeDtypeStruct((n_blocks * BM, dw), h.dtype),
        compiler_params=_cparams(("arbitrary",)),
        name="moe_dispatch",
    )(dest, block_ends, h)


def _expert_kernel(blk_ref, tab_ref, xs_ref, wg_hbm, wu_hbm, wd_hbm, y_ref, wg_f32, wu_f32, wd_f32,
                   wg_sc, wu_sc, wd_sc, n_loaded, sem, *, layer):
    i = pl.program_id(0)
    n_used = tab_ref[LANES + N_GROUPS + N_EXPERTS - 1]
    last = blk_ref.shape[0] - 1
    e = blk_ref[i]

    def fetch(expert, slot):
        return [pltpu.make_async_copy(src.at[layer, expert], dst.at[slot], sem.at[slot])
                for src, dst in ((wg_hbm, wg_f32), (wu_hbm, wu_f32), (wd_hbm, wd_f32))]

    @pl.when((i == 0) & (n_used > 0))
    def _():
        n_loaded[0] = 0
        for cp in fetch(e, 0):
            cp.start()

    @pl.when((i < n_used) & ((i == 0) | (e != blk_ref[jnp.maximum(i - 1, 0)])))
    def _():
        slot = n_loaded[0] % 2
        for cp in fetch(e, slot):
            cp.wait()
        nxt = lax.while_loop(lambda j: (j < n_used) & (blk_ref[jnp.minimum(j, last)] == e), lambda j: j + 1, i + 1)

        @pl.when(nxt < n_used)
        def _():
            for cp in fetch(blk_ref[jnp.minimum(nxt, last)], 1 - slot):
                cp.start()

        wg_sc[...] = wg_f32[slot].astype(BF16)
        wu_sc[...] = wu_f32[slot].astype(BF16)
        wd_sc[...] = wd_f32[slot].astype(BF16)
        n_loaded[0] = n_loaded[0] + 1

    bm = xs_ref.shape[0]
    quarter = bm // 4
    before = tab_ref[LANES + N_GROUPS - 1 + e]
    valid = tab_ref[2 * LANES + N_GROUPS + e] - (i - before) * bm
    n_quarters = jnp.clip((valid + quarter - 1) // quarter, 1, 4)
    for nq in range(1, 5):
        @pl.when((i < n_used) & (n_quarters == nq))
        def _():
            rows = nq * quarter
            x = _unpack_bf16_pairs(xs_ref[0:rows, :])
            g = _dot(x, wg_sc[...])
            u = _dot(x, wu_sc[...])
            hdn = (g * (1.0 / (1.0 + jnp.exp(-g)))) * u
            y_ref[0:rows, :] = _dot(hdn.astype(BF16), wd_sc[...])
            if rows < bm:
                y_ref[rows:, :] = jnp.zeros((bm - rows, y_ref.shape[1]), F32)

    @pl.when(i >= n_used)
    def _():
        y_ref[...] = jnp.zeros(y_ref.shape, F32)


def _expert_call(blk_expert, tab_flat, xs, wg, wu, wd, layer):
    p, dw = xs.shape
    d, ff = wg.shape[-2:]
    return pl.pallas_call(
        functools.partial(_expert_kernel, layer=layer),
        grid_spec=pltpu.PrefetchScalarGridSpec(
            num_scalar_prefetch=2,
            grid=(p // BM,),
            in_specs=[pl.BlockSpec((BM, dw), lambda i, b, n: (i, 0)),
                      pl.BlockSpec(memory_space=pl.ANY),
                      pl.BlockSpec(memory_space=pl.ANY),
                      pl.BlockSpec(memory_space=pl.ANY)],
            out_specs=pl.BlockSpec((BM, d), lambda i, b, n: (i, 0)),
            scratch_shapes=[pltpu.VMEM((2, d, ff), F32), pltpu.VMEM((2, d, ff), F32), pltpu.VMEM((2, ff, d), F32),
                            pltpu.VMEM((d, ff), BF16), pltpu.VMEM((d, ff), BF16), pltpu.VMEM((ff, d), BF16),
                            pltpu.SMEM((1,), jnp.int32), pltpu.SemaphoreType.DMA((2,))],
        ),
        out_shape=jax.ShapeDtypeStruct((p, d), F32),
        compiler_params=_cparams(("arbitrary",)),
        name="moe_experts",
    )(blk_expert, tab_flat, xs, wg, wu, wd)


def _combine_kernel(dest_ref, x_ref, rg_ref, mod_ref, y_ref, *rest, nt_c):
    out_refs, (buf, sem) = rest[:-2], rest[-2:]
    j = pl.program_id(0)
    n_tiles = pl.num_programs(0) - 1
    tm = x_ref.shape[0]

    for s in range(2):
        @pl.when((j < n_tiles) & (j % 2 == s))
        def _():
            base = 2 * j * tm
            for r in range(tm):
                for k in range(2):
                    src = dest_ref[base + (2 * r + k)]
                    pltpu.make_async_copy(y_ref.at[pl.ds(src, 1), :], buf.at[s, k, pl.ds(r, 1), :],
                                          sem.at[s]).start(priority=k)

    @pl.when(j >= 1)
    def _():
        slot = (j - 1) % 2
        for k in range(2):
            pltpu.make_async_copy(y_ref.at[pl.ds(0, tm), :], buf.at[slot, k], sem.at[slot]).wait()
        rg = rg_ref[...]
        y = buf[slot, 0] * rg[:, 0:1] + buf[slot, 1] * rg[:, 1:2]
        res = x_ref[...] + mod_ref[0, 5:6, :] * y
        if len(out_refs) == 1:
            out_refs[0][...] = res
        else:
            @pl.when(j - 1 < nt_c)
            def _():
                out_refs[0][...] = res

            @pl.when(j - 1 >= nt_c)
            def _():
                out_refs[1][...] = res


def _combine_call(dest, x_new, route_g, mods, yb, mod_map, nt_c, split):
    t, d = x_new.shape
    prev = lambda j: jnp.maximum(j - 1, 0)
    if split:
        out_specs = [pl.BlockSpec((TM, d), lambda j, p: (jnp.minimum(prev(j), nt_c - 1), 0)),
                     pl.BlockSpec((TM, d), lambda j, p: (jnp.maximum(prev(j) - nt_c, 0), 0))]
        out_shape = [jax.ShapeDtypeStruct((nt_c * TM, d), F32), jax.ShapeDtypeStruct((t - nt_c * TM, d), F32)]
    else:
        out_specs = [pl.BlockSpec((TM, d), lambda j, p: (prev(j), 0))]
        out_shape = [jax.ShapeDtypeStruct((t, d), F32)]
    return pl.pallas_call(
        functools.partial(_combine_kernel, nt_c=nt_c),
        grid_spec=pltpu.PrefetchScalarGridSpec(
            num_scalar_prefetch=1,
            grid=(t // TM + 1,),
            in_specs=[pl.BlockSpec((TM, d), lambda j, p: (prev(j), 0)),
                      pl.BlockSpec((TM, LANES), lambda j, p: (prev(j), 0)),
                      pl.BlockSpec((1, 6, d), lambda j, p: mod_map(prev(j))),
                      pl.BlockSpec(memory_space=pl.ANY)],
            out_specs=out_specs,
            scratch_shapes=[pltpu.VMEM((2, 2, TM, d), F32), pltpu.SemaphoreType.DMA((2,))],
        ),
        out_shape=out_shape,
        compiler_params=_cparams(("arbitrary",)),
        name="moe_combine",
    )(dest, x_new, route_g, mods, yb)


def _head_cols(base, heads):
    return np.concatenate([base + h * HEAD_DIM + np.arange(HEAD_DIM) for h in heads])


def _layouts():
    wa, wb, wc = HA * HEAD_DIM, HB * HEAD_DIM, HC * HEAD_DIM
    qa0, ka0, va0 = 0, wa, 2 * wa
    qb0 = 3 * wa
    kb0 = qb0 + wb
    vb0 = kb0 + KVB * HEAD_DIM
    qc0 = vb0 + KVB * HEAD_DIM
    kc0 = qc0 + wc
    vc0 = kc0 + KVC * HEAD_DIM
    b_heads = [h for c in range(4) for h in (c, c + 4)]
    c_heads = [0, 2, 1, 3]
    q_cols = np.concatenate([_head_cols(qb0, b_heads), _head_cols(qa0, range(HA)), _head_cols(qc0, c_heads)])
    k_cols = np.concatenate([ka0 + np.arange(wa), kb0 + np.arange(KVB * HEAD_DIM), kc0 + np.arange(KVC * HEAD_DIM)])
    v_cols = np.concatenate([va0 + np.arange(wa), vb0 + np.arange(KVB * HEAD_DIM), vc0 + np.arange(KVC * HEAD_DIM)])
    in_cols = np.concatenate([q_cols, k_cols, v_cols])
    gain_idx = np.concatenate([np.full(HB, 2), np.full(HA, 0), np.full(HC, 4),
                               np.full(HA, 1), np.full(KVB, 3), np.full(KVC, 5)])
    gain_scale = np.concatenate([np.full(HB + HA + HC, HEAD_DIM ** -0.5 * LOG2E), np.ones(HA + KVB + KVC)])
    return in_cols, gain_idx, gain_scale.astype(np.float32), _head_cols(0, b_heads), _head_cols(0, c_heads)


def _rope_tables(seq):
    quarter = HEAD_DIM // 4
    t = jnp.arange(seq)
    rows, cols = t // GRID_W, t % GRID_W
    freqs = ROPE_THETA ** (-jnp.arange(quarter, dtype=F32) / quarter)
    ang_r = rows.astype(F32)[:, None] * freqs[None, :]
    ang_c = cols.astype(F32)[:, None] * freqs[None, :]
    cos = jnp.concatenate([jnp.cos(ang_r), jnp.cos(ang_r), jnp.cos(ang_c), jnp.cos(ang_c)], axis=1)
    sin = jnp.concatenate([-jnp.sin(ang_r), jnp.sin(ang_r), -jnp.sin(ang_c), jnp.sin(ang_c)], axis=1)
    return jnp.tile(cos, (1, 2)), jnp.tile(sin, (1, 2))


def kernel(x_prompt, x_sample, cache_ctx_k, cache_ctx_v, c, c_ctx, norm_mix, norm_ffn, w_ada, b_ada, w_in, qk_gain,
           rpb, sink, w_br_a, w_br_b, w_br_c, w_gate, w_out, w_route_group, w_route_expert, w_exp_gate, w_exp_up,
           w_exp_down):
    batch, seq_c, d = x_prompt.shape
    nb, seq_d, _ = x_sample.shape
    depth = w_in.shape[0]
    t_c, t_d = batch * seq_c, nb * seq_d
    t = t_c + t_d
    n_ctx = cache_ctx_k.shape[2]
    assert t_c % TM == 0 and seq_d % TM == 0 and TM % seq_c == 0 and nb + 1 <= 8 and 2 * t < (1 << 16)
    nt_c = t_c // TM
    tiles_per_seq = seq_d // TM
    nb_max = 2 * t // BM + N_EXPERTS
    nb_max = (nb_max + 7) // 8 * 8

    in_cols, gain_idx, gain_scale, ob_rows, oc_rows = _layouts()
    x_c, x_d, x_d_off = x_prompt.reshape(t_c, d), x_sample.reshape(t_d, d), 0
    ones_ctx = jnp.ones((nb, n_ctx, LANES), BF16)
    cvec = jnp.concatenate([c_ctx[None, :], c, jnp.zeros((8 - 1 - nb, d), F32)], axis=0)
    mods = _ada_call(cvec, w_ada, b_ada).reshape(depth, 8, 6, d)
    rope_tabs = _rope_tables(seq_d)
    bias_tabs = _bias_call(rpb)

    def mod_map(i):
        return (jnp.where(i < nt_c, 0, 1 + (i - nt_c) // tiles_per_seq), 0, 0)

    prev_kv = []
    for l in range(depth):
        w_in_p = w_in[l][:, in_cols].astype(BF16)
        gain_row = (qk_gain[l][gain_idx] * gain_scale[:, None]).reshape(1, -1)
        gain_row = jnp.concatenate([gain_row, jnp.ones((1, KV_COLS), F32)], axis=1)
        nm = norm_mix[l].reshape(1, d)
        nf = norm_ffn[l].reshape(1, d)

        q_c, k_c, v_c, k32, v32 = _qkv_call(x_c, mods[l], nm, w_in_p, gain_row, tile_off=0, n_tok=t_c,
                                            seq_len=seq_c, row_off=0, per_seq_mod=False, rope_tabs=None, emit_f32=True,
                                            prev_kv=tuple(prev_kv) if l == depth - 1 else ())
        prev_kv += [k32, v32]
        q_d, k_d, v_d = _qkv_call(x_d, mods[l], nm, w_in_p, gain_row, tile_off=x_d_off, n_tok=t_d,
                                  seq_len=seq_d, row_off=1, per_seq_mod=True, rope_tabs=rope_tabs, emit_f32=False)
        k_d = k_d.reshape(nb, seq_d, KV_COLS)
        v_d = v_d.reshape(nb, seq_d, V_EXT_COLS)

        o_ctx = _ctx_attn_call(sink[l], q_c, k_c, v_c, seq_c)
        ck = cache_ctx_k[:, l].reshape(nb, n_ctx, KV_COLS).astype(BF16)
        cv = cache_ctx_v[:, l].reshape(nb, n_ctx, KV_COLS).astype(BF16)
        b0, b1 = HA * HEAD_DIM, (HA + KVB) * HEAD_DIM
        cv = jnp.concatenate([cv[:, :, :b1], ones_ctx, cv[:, :, b1:], ones_ctx], axis=2)
        o_b = _dec_b_call(q_d, jnp.concatenate([k_d[:, :, b0:b1], ck[:, :, b0:b1]], axis=1),
                          jnp.concatenate([v_d[:, :, b0:b1 + LANES], cv[:, :, b0:b1 + LANES]], axis=1))
        o_c = _dec_c_call(sink[l], q_d, k_d, v_d, ck, cv)
        o_a = _dec_a_call(q_d, k_d, v_d, ck, cv, bias_tabs[l])

        w_route = jnp.concatenate([w_route_group[l], w_route_expert[l],
                                   jnp.zeros((d, LANES - N_GROUPS - N_EXPERTS), F32)], axis=1)
        wr_hi = w_route.astype(BF16)
        wr = jnp.concatenate([wr_hi, (w_route - wr_hi.astype(F32)).astype(BF16)], axis=1)
        x_new, h, logits = _merge_call(
            x_c, x_d, x_d_off, o_ctx, o_b, o_a, o_c, mods[l], nm, nf, w_gate[l].astype(BF16), w_br_b[l][ob_rows].astype(BF16),
            w_br_a[l].astype(BF16), w_br_c[l][oc_rows].astype(BF16), w_out[l].astype(BF16), wr, mod_map)

        route_i, route_g, tab, blk = _route_call(logits, nb_max)
        dest = _slots_call(route_i, tab)[:, :2].reshape(-1)
        xs = _dispatch_call(dest, tab[1], h, nb_max)
        yb = _expert_call(blk[:, 0], tab.reshape(-1), xs, w_exp_gate, w_exp_up, w_exp_down, l)
        outs = _combine_call(dest, x_new, route_g, mods[l], yb, mod_map, nt_c, split=(l == depth - 1))
        if l < depth - 1:
            x_c = x_d = outs[0]
            x_d_off = nt_c

    y_prompt = outs[0].reshape(batch, seq_c, d)
    y_sample = outs[1].reshape(nb, seq_d, d)
    new_k, new_v = (a.reshape(batch, depth, seq_c, HA + KVB + KVC, HEAD_DIM) for a in prev_kv[-2:])
    return (y_prompt, y_sample, new_k, new_v)
```
